```python
import jax, jax.numpy as jnp
from jax import lax
import numpy as np

D_MODEL = 1024
BATCH = 8
SEQ = 4096
DEPTH = 2

GRID_W = 64
CTX_LEN = 256
N_MIXERS = 2
N_HEADS = 8
N_KV_HEADS = 2
GROUP = N_HEADS // N_KV_HEADS
HEAD_DIM = 128
ATTN_DIM = N_HEADS * HEAD_DIM
QKV_DIM = (N_HEADS + 2 * N_KV_HEADS) * HEAD_DIM
WINDOW = 128
BLOCK = 128
AXIS_DIM = HEAD_DIM // 2
ROPE_BASE = 10000.0
LRU_WIDTH = D_MODEL
LRU_BLOCKS = 8
LRU_BLOCK_DIM = LRU_WIDTH // LRU_BLOCKS
CONV_WIDTH = 4
CONV_PAD_LEFT = 2
CONV_PAD_RIGHT = 1
LRU_C = 8.0
D_FF = 3584
N_EXPERTS = 8
TOP_K = 2
N_EVEN = (DEPTH + 1) // 2
N_ODD = DEPTH // 2
ALPHA = (2 * DEPTH) ** 0.25
BETA = (8 * DEPTH) ** -0.25
LN_EPS = 1e-5
NEG_INF = -1e30

kernel_name = "hybrid_swa_rglru_moe_diffusion_trunk"


def layer_norm(h, g, b):
    hf = h.astype(jnp.float32)
    mu = jnp.mean(hf, axis=-1, keepdims=True)
    var = jnp.mean(jnp.square(hf - mu), axis=-1, keepdims=True)
    return ((hf - mu) * lax.rsqrt(var + LN_EPS) * g + b).astype(h.dtype)


def modulate(h, shift, scale):
    return h * (1 + scale) + shift


def grid_rope_angles(L):
    rows = L // GRID_W
    t = jnp.arange(rows * GRID_W)
    row = (t // GRID_W).astype(jnp.float32)
    col = (t % GRID_W).astype(jnp.float32)
    freqs = ROPE_BASE ** (-jnp.arange(0, AXIS_DIM, 2, dtype=jnp.float32) / AXIS_DIM)
    return row[:, None] * freqs, col[:, None] * freqs


def rotate_axis(u, ang):
    half = AXIS_DIM // 2
    cos = jnp.cos(ang)[None, :, None, :].astype(u.dtype)
    sin = jnp.sin(ang)[None, :, None, :].astype(u.dtype)
    u1, u2 = u[..., :half], u[..., half:]
    return jnp.concatenate([u1 * cos - u2 * sin, u2 * cos + u1 * sin], axis=-1)


def apply_rope_2d(u, row_ang, col_ang):
    return jnp.concatenate([rotate_axis(u[..., :AXIS_DIM], row_ang),
                            rotate_axis(u[..., AXIS_DIM:], col_ang)], axis=-1)


def windowed_gqa(hx, hc, w_qkv, w_o, sink, row_ang, col_ang, need_ctx):
    B, L, _ = hx.shape
    Lc = hc.shape[1]
    scale = HEAD_DIM ** -0.5

    def split_heads(u, n):
        q, k, v = jnp.split(u, [ATTN_DIM, ATTN_DIM + N_KV_HEADS * HEAD_DIM], axis=-1)
        return (q.reshape(B, n, N_HEADS, HEAD_DIM), k.reshape(B, n, N_KV_HEADS, HEAD_DIM),
                v.reshape(B, n, N_KV_HEADS, HEAD_DIM))

    q, k, v = split_heads(hx @ w_qkv, L)
    q = apply_rope_2d(q, row_ang, col_ang) * scale
    k = apply_rope_2d(k, row_ang, col_ang)
    qc, kc, vc = split_heads(hc @ w_qkv, Lc)
    sink_g = sink.reshape(N_KV_HEADS, GROUP).astype(jnp.float32)

    nb = L // BLOCK
    qb = q.reshape(B, nb, BLOCK, N_KV_HEADS, GROUP, HEAD_DIM)
    pad = ((0, 0), (BLOCK, BLOCK), (0, 0), (0, 0))
    kp = jnp.pad(k, pad).reshape(B, nb + 2, BLOCK, N_KV_HEADS, HEAD_DIM)
    vp = jnp.pad(v, pad).reshape(B, nb + 2, BLOCK, N_KV_HEADS, HEAD_DIM)
    k_band = jnp.concatenate([kp[:, :-2], kp[:, 1:-1], kp[:, 2:]], axis=2)
    v_band = jnp.concatenate([vp[:, :-2], vp[:, 1:-1], vp[:, 2:]], axis=2)
    s_loc = jnp.einsum('bnqkgd,bnjkd->bnkgqj', qb, k_band).astype(jnp.float32)
    qi = jnp.arange(BLOCK)[:, None]
    kj = jnp.arange(3 * BLOCK)[None, :]
    rel = kj - BLOCK - qi
    key_pos = jnp.arange(nb)[:, None, None] * BLOCK - BLOCK + kj[None]
    valid = (jnp.abs(rel) <= WINDOW)[None] & (key_pos >= 0) & (key_pos < L)
    s_loc = jnp.where(valid[None, :, None, None], s_loc, NEG_INF)
    s_ctx = jnp.einsum('bnqkgd,bckd->bnkgqc', qb, kc).astype(jnp.float32)
    s_sink = jnp.broadcast_to(sink_g[None, None, :, :, None, None],
                              (B, nb, N_KV_HEADS, GROUP, BLOCK, 1))
    p = jax.nn.softmax(jnp.concatenate([s_loc, s_ctx, s_sink], axis=-1), axis=-1).astype(hx.dtype)
    o = (jnp.einsum('bnkgqj,bnjkd->bnqkgd', p[..., :3 * BLOCK], v_band)
         + jnp.einsum('bnkgqc,bckd->bnqkgd', p[..., 3 * BLOCK:3 * BLOCK + Lc], vc))
    out_x = o.reshape(B, L, ATTN_DIM) @ w_o

    if need_ctx:
        qcs = qc.reshape(B, Lc, N_KV_HEADS, GROUP, HEAD_DIM) * scale
        sc = jnp.einsum('bqkgd,bckd->bkgqc', qcs, kc).astype(jnp.float32)
        sc_sink = jnp.broadcast_to(sink_g[None, :, :, None, None], (B, N_KV_HEADS, GROUP, Lc, 1))
        pc = jax.nn.softmax(jnp.concatenate([sc, sc_sink], axis=-1), axis=-1).astype(hc.dtype)
        oc = jnp.einsum('bkgqc,bckd->bqkgd', pc[..., :Lc], vc).reshape(B, Lc, ATTN_DIM)
        out_c = oc @ w_o
    else:
        out_c = None
    return out_x, out_c


def centred_dwconv(u, w, b):
    y = lax.conv_general_dilated(u, w[:, None, :].astype(u.dtype), window_strides=(1,),
                                 padding=[(CONV_PAD_LEFT, CONV_PAD_RIGHT)],
                                 dimension_numbers=('NWC', 'WIO', 'NWC'),
                                 feature_group_count=u.shape[-1])
    return y + b


def block_diag(u, w):
    ub = u.reshape(u.shape[:-1] + (LRU_BLOCKS, LRU_BLOCK_DIM))
    return jnp.einsum('blnd,nde->blne', ub, w).reshape(u.shape)


def linear_scan(a, b, h0):
    b = b.at[:, 0].add(a[:, 0] * h0)

    def combine(e1, e2):
        a1, b1 = e1
        a2, b2 = e2
        return a1 * a2, a2 * b1 + b2

    _, h = lax.associative_scan(combine, (a, b), axis=1)
    return h


def reverse_scan(a, b, h0):
    return jnp.flip(linear_scan(jnp.flip(a, axis=1), jnp.flip(b, axis=1), h0), axis=1)


def bidir_rglru(hx, hc, w_in, conv_w, conv_b, lam, w_r, b_r, w_i, b_i, w_out, need_ctx):
    B = hx.shape[0]
    R = LRU_WIDTH
    gate_x, xb = jnp.split(hx @ w_in, 2, axis=-1)
    if need_ctx:
        gate_c, xb_c = jnp.split(hc @ w_in, 2, axis=-1)
    else:
        xb_c = hc @ w_in[:, R:]
    u_x = centred_dwconv(xb, conv_w, conv_b)
    u_c = centred_dwconv(xb_c, conv_w, conv_b)

    def rg_lru_coeffs(u, d):
        r = jax.nn.sigmoid(block_diag(u, w_r[d]) + b_r[d]).astype(jnp.float32)
        ig = jax.nn.sigmoid(block_diag(u, w_i[d]) + b_i[d])
        log_a = -LRU_C * r * jax.nn.softplus(-lam[d].astype(jnp.float32))
        a = jnp.exp(log_a)
        bterm = jnp.sqrt(-jnp.expm1(2.0 * log_a)) * (ig * u).astype(jnp.float32)
        return a, bterm

    h0 = jnp.zeros((B, R), jnp.float32)
    a_c, b_c = rg_lru_coeffs(u_c, 0)
    hc_f = linear_scan(a_c, b_c, h0)
    a_x, b_x = rg_lru_coeffs(u_x, 0)
    hx_f = linear_scan(a_x, b_x, hc_f[:, -1])
    a_c, b_c = rg_lru_coeffs(u_c, 1)
    hc_b = reverse_scan(a_c, b_c, h0)
    a_x, b_x = rg_lru_coeffs(u_x, 1)
    hx_b = reverse_scan(a_x, b_x, hc_b[:, 0])

    out_x = (jax.nn.gelu(gate_x) * (hx_f + hx_b).astype(hx.dtype)) @ w_out
    if need_ctx:
        out_c = (jax.nn.gelu(gate_c) * (hc_f + hc_b).astype(hc.dtype)) @ w_out
    else:
        out_c = None
    return out_x, out_c


def swiglu(h, w_gu, w_dn):
    g, u = jnp.split(h @ w_gu, 2, axis=-1)
    return (jax.nn.silu(g) * u) @ w_dn


def moe_swiglu(h, router, w_gu, w_dn):
    logits = (h @ router).astype(jnp.float32)
    top_v, top_i = lax.top_k(logits, TOP_K)
    top_w = jax.nn.softmax(top_v, axis=-1)
    gates = jnp.einsum('blk,blke->ble', top_w,
                       jax.nn.one_hot(top_i, N_EXPERTS, dtype=jnp.float32)).astype(h.dtype)
    y = jnp.zeros_like(h)
    for e in range(N_EXPERTS):
        y = y + gates[..., e:e + 1] * swiglu(h, w_gu[e], w_dn[e])
    return y


def setup_inputs(seed: int = 0) -> dict:
    key = jax.random.key(seed)
    ks = iter(jax.random.split(key, 32))

    def nrm(shape, s):
        return jax.random.normal(next(ks), shape, jnp.float32) * s

    D, R, F = D_MODEL, LRU_WIDTH, D_FF
    a8 = jax.random.uniform(next(ks), (N_ODD, 2, R), jnp.float32, 0.9, 0.999)
    sig = a8 ** (1.0 / LRU_C)
    lru_lambda = jnp.log(sig) - jnp.log1p(-sig)
    return {
        "x": nrm((BATCH, SEQ, D), 1.0),
        "c": nrm((BATCH, D), 1.0),
        "ctx": nrm((BATCH, CTX_LEN, D), 1.0),
        "c_ctx": nrm((D,), 1.0),
        "w_mod": nrm((DEPTH, D, 6 * D), D ** -0.5),
        "b_mod": nrm((DEPTH, 6 * D), 0.02),
        "ln_g": 1.0 + nrm((DEPTH, 2, D), 0.02),
        "ln_b": nrm((DEPTH, 2, D), 0.02),
        "attn_w_qkv": nrm((N_EVEN, D, QKV_DIM), D ** -0.5),
        "attn_w_o": nrm((N_EVEN, ATTN_DIM, D), ATTN_DIM ** -0.5 * BETA),
        "attn_sink": nrm((N_EVEN, N_HEADS), 0.5),
        "ffn_w_gu": nrm((N_EVEN, D, 2 * F), D ** -0.5),
        "ffn_w_dn": nrm((N_EVEN, F, D), F ** -0.5 * BETA),
        "lru_w_in": nrm((N_ODD, D, 2 * R), D ** -0.5),
        "lru_conv_w": nrm((N_ODD, CONV_WIDTH, R), CONV_WIDTH ** -0.5),
        "lru_conv_b": nrm((N_ODD, R), 0.02),
        "lru_lambda": lru_lambda,
        "lru_w_r": nrm((N_ODD, 2, LRU_BLOCKS, LRU_BLOCK_DIM, LRU_BLOCK_DIM), LRU_BLOCK_DIM ** -0.5),
        "lru_b_r": nrm((N_ODD, 2, R), 0.02),
        "lru_w_i": nrm((N_ODD, 2, LRU_BLOCKS, LRU_BLOCK_DIM, LRU_BLOCK_DIM), LRU_BLOCK_DIM ** -0.5),
        "lru_b_i": nrm((N_ODD, 2, R), 0.02),
        "lru_w_out": nrm((N_ODD, R, D), R ** -0.5 * BETA),
        "moe_router": nrm((N_ODD, D, N_EXPERTS), D ** -0.5),
        "moe_w_gu": nrm((N_ODD, N_EXPERTS, D, 2 * F), D ** -0.5),
        "moe_w_dn": nrm((N_ODD, N_EXPERTS, F, D), F ** -0.5 * BETA),
    }


def reference(x, c, ctx, c_ctx, w_mod, b_mod, ln_g, ln_b,
              attn_w_qkv, attn_w_o, attn_sink, ffn_w_gu, ffn_w_dn,
              lru_w_in, lru_conv_w, lru_conv_b, lru_lambda, lru_w_r, lru_b_r, lru_w_i, lru_b_i,
              lru_w_out, moe_router, moe_w_gu, moe_w_dn):
    L = x.shape[1]
    Lc = ctx.shape[1]
    row_ang, col_ang = grid_rope_angles(L)
    silu_c = jax.nn.silu(c)
    silu_cc = jax.nn.silu(c_ctx)
    for i in range(DEPTH):
        need_ctx = i < DEPTH - 1
        j = i // N_MIXERS
        mod_x = (silu_c @ w_mod[i] + b_mod[i])[:, None, :]
        mod_c = (silu_cc @ w_mod[i] + b_mod[i])[None, None, :]
        sh1, sc1, g1, sh2, sc2, g2 = jnp.split(mod_x, 6, axis=-1)
        csh1, csc1, cg1, csh2, csc2, cg2 = jnp.split(mod_c, 6, axis=-1)

        hx = modulate(x, sh1, sc1)
        hc = modulate(ctx, csh1, csc1)
        if i % N_MIXERS == 0:
            ox, oc = windowed_gqa(hx, hc, attn_w_qkv[j], attn_w_o[j], attn_sink[j],
                                  row_ang, col_ang, need_ctx)
        else:
            ox, oc = bidir_rglru(hx, hc, lru_w_in[j], lru_conv_w[j], lru_conv_b[j], lru_lambda[j],
                                 lru_w_r[j], lru_b_r[j], lru_w_i[j], lru_b_i[j], lru_w_out[j],
                                 need_ctx)
        x = layer_norm(ALPHA * x + g1 * ox, ln_g[i, 0], ln_b[i, 0])

        hx = modulate(x, sh2, sc2)
        if need_ctx:
            ctx = layer_norm(ALPHA * ctx + cg1 * oc, ln_g[i, 0], ln_b[i, 0])
            h = jnp.concatenate([modulate(ctx, csh2, csc2), hx], axis=1)
        else:
            h = hx
        if i % 2 == 0:
            f = swiglu(h, ffn_w_gu[j], ffn_w_dn[j])
        else:
            f = moe_swiglu(h, moe_router[j], moe_w_gu[j], moe_w_dn[j])
        if need_ctx:
            ctx = layer_norm(ALPHA * ctx + cg2 * f[:, :Lc], ln_g[i, 1], ln_b[i, 1])
            fx = f[:, Lc:]
        else:
            fx = f
        x = layer_norm(ALPHA * x + g2 * fx, ln_g[i, 1], ln_b[i, 1])
    return x
```

```python
import functools

import jax
import jax.numpy as jnp
from jax import lax
from jax.experimental import pallas as pl
from jax.experimental.pallas import tpu as pltpu

N_HEADS = 8
N_KV_HEADS = 2
GROUP = N_HEADS // N_KV_HEADS
HEAD_DIM = 128
BLOCK = 128
AXIS_DIM = HEAD_DIM // 2
ROPE_BASE = 10000.0
LRU_BLOCKS = 8
LRU_BLOCK_DIM = 128
LRU_C = 8.0
N_EXPERTS = 8
TOP_K = 2
DEPTH = 2
ALPHA = (2 * DEPTH) ** 0.25
LN_EPS = 1e-5
NEG_INF = -1e30

LANES = 128
SUBLANES = 8
VMEM_LIMIT_BYTES = 56 * 1024 * 1024

F32 = jnp.float32
BF16 = jnp.bfloat16


def _tile_sizes():
    return dict(
        qkv_rows=512,
        ffn_rows=512,
        ffn_cols=512,
        proj_rows=512,
        scan_steps=64,
        route_rows=512,
        moe_rows=512,
        combine_rows=256,
    )


def _cparams(sem, vmem=None):
    return pltpu.CompilerParams(dimension_semantics=sem, vmem_limit_bytes=vmem)


def _layer_norm(h, g, b):
    mu = jnp.mean(h, axis=-1, keepdims=True)
    d = h - mu
    var = jnp.mean(d * d, axis=-1, keepdims=True)
    return d * lax.rsqrt(var + LN_EPS) * g + b


def _mod_slice(mod_ref, k, d):
    return mod_ref[:, k * d:(k + 1) * d]


def _mod_kernel(c_ref, w_ref, b_ref, o_ref):
    c = c_ref[...]
    s = c * jax.nn.sigmoid(c)
    o_ref[...] = jnp.dot(s, w_ref[...], preferred_element_type=F32,
                         precision=lax.Precision.HIGHEST) + b_ref[...]


def _mod_call(cond, w_mod, b_mod):
    rows, d = cond.shape
    depth, _, n = w_mod.shape
    tn = 1536
    return pl.pallas_call(
        _mod_kernel,
        grid=(depth, n // tn),
        in_specs=[
            pl.BlockSpec((rows, d), lambda i, j: (0, 0)),
            pl.BlockSpec((None, d, tn), lambda i, j: (i, 0, j)),
            pl.BlockSpec((None, 1, tn), lambda i, j: (i, 0, j)),
        ],
        out_specs=pl.BlockSpec((None, rows, tn), lambda i, j: (i, 0, j)),
        out_shape=jax.ShapeDtypeStruct((depth, rows, n), F32),
        compiler_params=_cparams(("arbitrary", "arbitrary")),
        name="mod_vectors",
    )(cond, w_mod, b_mod)


def _qkv_kernel(x_ref, mod_ref, w_ref, cos_ref, sa_ref, sb_ref, q_ref, k_ref, v_ref):
    d = x_ref.shape[-1]
    h = x_ref[...] * (1.0 + _mod_slice(mod_ref, 1, d)) + _mod_slice(mod_ref, 0, d)
    qkv = jnp.dot(h.astype(BF16), w_ref[...], preferred_element_type=F32)
    cos, sa, sb = cos_ref[...], sa_ref[...], sb_ref[...]
    scale = HEAD_DIM ** -0.5
    for head in range(N_HEADS + N_KV_HEADS):
        u = qkv[:, head * HEAD_DIM:(head + 1) * HEAD_DIM]
        r = (u * cos + pltpu.roll(u, HEAD_DIM - AXIS_DIM // 2, axis=1) * sa
             + pltpu.roll(u, AXIS_DIM // 2, axis=1) * sb)
        if head < N_HEADS:
            q_ref[:, head * HEAD_DIM:(head + 1) * HEAD_DIM] = (r * scale).astype(BF16)
        else:
            kh = head - N_HEADS
            k_ref[:, kh * HEAD_DIM:(kh + 1) * HEAD_DIM] = r.astype(BF16)
    kv = N_KV_HEADS * HEAD_DIM
    v_ref[...] = qkv[:, N_HEADS * HEAD_DIM + kv:].astype(BF16)


def _qkv_call(x, mod, w_qkv, tabs, rows):
    b, l, d = x.shape
    n_q, n_kv = N_HEADS * HEAD_DIM, N_KV_HEADS * HEAD_DIM
    tm = min(rows, l)
    tab_spec = pl.BlockSpec((tm, HEAD_DIM), lambda i, j: (j, 0))
    return pl.pallas_call(
        _qkv_kernel,
        grid=(b, l // tm),
        in_specs=[
            pl.BlockSpec((None, tm, d), lambda i, j: (i, j, 0)),
            pl.BlockSpec((None, 1, 6 * d), lambda i, j: (i, 0, 0)),
            pl.BlockSpec(w_qkv.shape, lambda i, j: (0, 0)),
            tab_spec, tab_spec, tab_spec,
        ],
        out_specs=[
            pl.BlockSpec((None, tm, n_q), lambda i, j: (i, j, 0)),
            pl.BlockSpec((None, tm, n_kv), lambda i, j: (i, j, 0)),
            pl.BlockSpec((None, tm, n_kv), lambda i, j: (i, j, 0)),
        ],
        out_shape=[
            jax.ShapeDtypeStruct((b, l, n_q), BF16),
            jax.ShapeDtypeStruct((b, l, n_kv), BF16),
            jax.ShapeDtypeStruct((b, l, n_kv), BF16),
        ],
        compiler_params=_cparams(("arbitrary", "arbitrary"), VMEM_LIMIT_BYTES),
        name="qkv_rope",
    )(x, mod, w_qkv, *tabs)


def _rope_tables(l, grid_w):
    t = jnp.arange(l)
    pos = jnp.stack([(t // grid_w).astype(F32), (t % grid_w).astype(F32)], axis=1)
    freqs = ROPE_BASE ** (-jnp.arange(0, AXIS_DIM, 2, dtype=F32) / AXIS_DIM)
    lane = jnp.arange(HEAD_DIM)
    ang = pos[:, lane // AXIS_DIM] * freqs[lane % (AXIS_DIM // 2)][None, :]
    first = (lane % AXIS_DIM) < AXIS_DIM // 2
    cos, sin = jnp.cos(ang), jnp.sin(ang)
    return cos, jnp.where(first, -sin, 0.0), jnp.where(first, 0.0, sin)


def _no_rope_tables(l):
    z = jnp.zeros((l, HEAD_DIM), F32)
    return jnp.ones((l, HEAD_DIM), F32), z, z


def _nt_dot(a, b):
    return lax.dot_general(a, b, (((1,), (1,)), ((), ())), preferred_element_type=F32)


def _attend(qs, segs, sink_col):
    scores = []
    for k, _, mask in segs:
        s = _nt_dot(qs, k)
        if mask is not None:
            s = jnp.where(mask, s, NEG_INF)
        scores.append(s)
    m = sink_col
    for s in scores:
        m = jnp.maximum(m, jnp.max(s, axis=-1, keepdims=True))
    denom = jnp.exp(sink_col - m)
    o = None
    for s, (_, v, _) in zip(scores, segs):
        p = jnp.exp(s - m)
        denom = denom + jnp.sum(p, axis=-1, keepdims=True)
        pv = jnp.dot(p.astype(BF16), v, preferred_element_type=F32)
        o = pv if o is None else o + pv
    return o / denom


def _attn_epilogue(o_scr, wo_ref, x_ref, mod_ref, g_ref, b_ref, out_ref):
    d = x_ref.shape[-1]
    proj = jnp.dot(o_scr[...], wo_ref[...], preferred_element_type=F32)
    out_ref[...] = _layer_norm(ALPHA * x_ref[...] + _mod_slice(mod_ref, 2, d) * proj,
                               g_ref[...], b_ref[...])


def _sink_column(sink_ref, kvh):
    return jnp.concatenate(
        [jnp.full((BLOCK, 1), sink_ref[kvh * GROUP + g], F32) for g in range(GROUP)], axis=0)


def _stack_group(q_ref, kvh):
    return jnp.concatenate(
        [q_ref[:, (kvh * GROUP + g) * HEAD_DIM:(kvh * GROUP + g + 1) * HEAD_DIM]
         for g in range(GROUP)], axis=0)


def _unstack_group(o, o_scr, kvh):
    for g in range(GROUP):
        col = (kvh * GROUP + g) * HEAD_DIM
        o_scr[:, col:col + HEAD_DIM] = o[g * BLOCK:(g + 1) * BLOCK].astype(BF16)


def _band_attn_kernel(sink_ref, q_ref, kp_ref, kc_ref, kn_ref, vp_ref, vc_ref, vn_ref,
                      kx_ref, vx_ref, wo_ref, x_ref, mod_ref, g_ref, b_ref, out_ref, o_scr):
    n, nb = pl.program_id(1), pl.num_programs(1)
    row = lax.broadcasted_iota(jnp.int32, (GROUP * BLOCK, BLOCK), 0) % BLOCK
    col = lax.broadcasted_iota(jnp.int32, (GROUP * BLOCK, BLOCK), 1)
    mask_prev = (col >= row) & (n > 0)
    mask_next = (col <= row) & (n < nb - 1)
    for kvh in range(N_KV_HEADS):
        hs = slice(kvh * HEAD_DIM, (kvh + 1) * HEAD_DIM)
        segs = [(kp_ref[:, hs], vp_ref[:, hs], mask_prev),
                (kc_ref[:, hs], vc_ref[:, hs], None),
                (kn_ref[:, hs], vn_ref[:, hs], mask_next),
                (kx_ref[:, hs], vx_ref[:, hs], None)]
        o = _attend(_stack_group(q_ref, kvh), segs, _sink_column(sink_ref, kvh))
        _unstack_group(o, o_scr, kvh)
    _attn_epilogue(o_scr, wo_ref, x_ref, mod_ref, g_ref, b_ref, out_ref)


def _ctx_attn_kernel(sink_ref, q_ref, kx_ref, vx_ref, wo_ref, x_ref, mod_ref, g_ref, b_ref,
                     out_ref, o_scr):
    for kvh in range(N_KV_HEADS):
        hs = slice(kvh * HEAD_DIM, (kvh + 1) * HEAD_DIM)
        o = _attend(_stack_group(q_ref, kvh), [(kx_ref[:, hs], vx_ref[:, hs], None)],
                    _sink_column(sink_ref, kvh))
        _unstack_group(o, o_scr, kvh)
    _attn_epilogue(o_scr, wo_ref, x_ref, mod_ref, g_ref, b_ref, out_ref)


def _attn_call(q, k, v, kx, vx, sink, w_o, x, mod, ln_g, ln_b, band):
    b, l, d = x.shape
    lc = kx.shape[1]
    nb = l // BLOCK
    n_q, n_kv = N_HEADS * HEAD_DIM, N_KV_HEADS * HEAD_DIM
    kv_blk = lambda f: pl.BlockSpec((None, BLOCK, n_kv), f)
    common_in = [
        pl.BlockSpec((None, lc, n_kv), lambda i, j: (i, 0, 0)),
        pl.BlockSpec((None, lc, n_kv), lambda i, j: (i, 0, 0)),
        pl.BlockSpec(w_o.shape, lambda i, j: (0, 0)),
        pl.BlockSpec((None, BLOCK, d), lambda i, j: (i, j, 0)),
        pl.BlockSpec((None, 1, 6 * d), lambda i, j: (i, 0, 0)),
        pl.BlockSpec((1, d), lambda i, j: (0, 0)),
        pl.BlockSpec((1, d), lambda i, j: (0, 0)),
    ]
    head_in = [pl.BlockSpec(memory_space=pltpu.SMEM),
               pl.BlockSpec((None, BLOCK, n_q), lambda i, j: (i, j, 0))]
    if band:
        prev = lambda i, j: (i, jnp.maximum(j - 1, 0), 0)
        cur = lambda i, j: (i, j, 0)
        nxt = lambda i, j: (i, jnp.minimum(j + 1, nb - 1), 0)
        in_specs = head_in + [kv_blk(prev), kv_blk(cur), kv_blk(nxt)] * 2 + common_in
        args = (sink, q, k, k, k, v, v, v, kx, vx, w_o, x, mod, ln_g, ln_b)
        body, name = _band_attn_kernel, "band_attention"
    else:
        in_specs = head_in + common_in
        args = (sink, q, kx, vx, w_o, x, mod, ln_g, ln_b)
        body, name = _ctx_attn_kernel, "context_attention"
    return pl.pallas_call(
        body,
        grid=(b, nb),
        in_specs=in_specs,
        out_specs=pl.BlockSpec((None, BLOCK, d), lambda i, j: (i, j, 0)),
        out_shape=jax.ShapeDtypeStruct((b, l, d), F32),
        scratch_shapes=[pltpu.VMEM((BLOCK, n_q), BF16)],
        compiler_params=_cparams(("arbitrary", "arbitrary"), VMEM_LIMIT_BYTES),
        name=name,
    )(*args)


def _swiglu_chunk(hb, wg_ref, wu_ref, wd_ref):
    g = jnp.dot(hb, wg_ref[...], preferred_element_type=F32)
    u = jnp.dot(hb, wu_ref[...], preferred_element_type=F32)
    a = (g * jax.nn.sigmoid(g)) * u
    return jnp.dot(a.astype(BF16), wd_ref[...], preferred_element_type=F32)


def _ffn_kernel(x_ref, mod_ref, wg_ref, wu_ref, wd_ref, g_ref, b_ref, out_ref, hb_scr, acc_scr):
    f, nf = pl.program_id(2), pl.num_programs(2)
    d = x_ref.shape[-1]

    @pl.when(f == 0)
    def _():
        h = x_ref[...] * (1.0 + _mod_slice(mod_ref, 4, d)) + _mod_slice(mod_ref, 3, d)
        hb_scr[...] = h.astype(BF16)
        acc_scr[...] = jnp.zeros_like(acc_scr)

    acc_scr[...] += _swiglu_chunk(hb_scr[...], wg_ref, wu_ref, wd_ref)

    @pl.when(f == nf - 1)
    def _():
        out_ref[...] = _layer_norm(ALPHA * x_ref[...] + _mod_slice(mod_ref, 5, d) * acc_scr[...],
                                   g_ref[...], b_ref[...])


def _ffn_call(x, mod, w_gu, w_dn, ln_g, ln_b, rows, cols):
    b, l, d = x.shape
    ff = w_dn.shape[0]
    tm, fc = min(rows, l), cols
    nf = ff // fc
    return pl.pallas_call(
        _ffn_kernel,
        grid=(b, l // tm, nf),
        in_specs=[
            pl.BlockSpec((None, tm, d), lambda i, j, f: (i, j, 0)),
            pl.BlockSpec((None, 1, 6 * d), lambda i, j, f: (i, 0, 0)),
            pl.BlockSpec((d, fc), lambda i, j, f: (0, f)),
            pl.BlockSpec((d, fc), lambda i, j, f: (0, nf + f)),
            pl.BlockSpec((fc, d), lambda i, j, f: (f, 0)),
            pl.BlockSpec((1, d), lambda i, j, f: (0, 0)),
            pl.BlockSpec((1, d), lambda i, j, f: (0, 0)),
        ],
        out_specs=pl.BlockSpec((None, tm, d), lambda i, j, f: (i, j, 0)),
        out_shape=jax.ShapeDtypeStruct((b, l, d), F32),
        scratch_shapes=[pltpu.VMEM((tm, d), BF16), pltpu.VMEM((tm, d), F32)],
        compiler_params=_cparams(("arbitrary", "arbitrary", "arbitrary"), VMEM_LIMIT_BYTES),
        name="dense_swiglu",
    )(x, mod, w_gu, w_gu, w_dn, ln_g, ln_b)


def _inproj_kernel(x_ref, mod_ref, w_ref, gate_ref, xb_ref):
    d = x_ref.shape[-1]
    r = xb_ref.shape[-1]
    h = x_ref[...] * (1.0 + _mod_slice(mod_ref, 1, d)) + _mod_slice(mod_ref, 0, d)
    y = jnp.dot(h.astype(BF16), w_ref[...], preferred_element_type=F32)
    gate_ref[...] = y[:, :r]
    xb_ref[...] = y[:, r:]


def _inproj_call(x, mod, w_in, rows):
    b, l, d = x.shape
    r = w_in.shape[1] // 2
    tm = min(rows, l)
    return pl.pallas_call(
        _inproj_kernel,
        grid=(b, l // tm),
        in_specs=[
            pl.BlockSpec((None, tm, d), lambda i, j: (i, j, 0)),
            pl.BlockSpec((None, 1, 6 * d), lambda i, j: (i, 0, 0)),
            pl.BlockSpec(w_in.shape, lambda i, j: (0, 0)),
        ],
        out_specs=[
            pl.BlockSpec((None, tm, r), lambda i, j: (i, j, 0)),
            pl.BlockSpec((tm, r), lambda i, j: (j, i)),
        ],
        out_shape=[jax.ShapeDtypeStruct((b, l, r), F32), jax.ShapeDtypeStruct((l, b * r), F32)],
        compiler_params=_cparams(("arbitrary", "arbitrary"), VMEM_LIMIT_BYTES),
        name="lru_in_proj",
    )(x, mod, w_in)


def _lru_coeffs(main_ref, prev_ref, next_ref, first, last, cw_ref, cb_ref, wd_ref, br_ref, bi_ref,
                lam_ref, a_scr, b_scr):
    tl, nb, r = main_ref.shape
    prev = jnp.where(first, 0.0, prev_ref[...])
    nxt = jnp.where(last, 0.0, next_ref[...])
    ext = jnp.concatenate([prev, main_ref[...], nxt], axis=0)
    u = cb_ref[...][None]
    for k in range(4):
        u = u + cw_ref[k:k + 1, :][None] * ext[k:k + tl]
    u = u.reshape(tl * nb, r)
    ub = u.astype(BF16)
    lam = lam_ref[...]
    softplus_neg = jnp.maximum(-lam, 0.0) + jnp.log1p(jnp.exp(-jnp.abs(lam)))
    pw = 2 * LRU_BLOCK_DIM
    for p in range(LRU_BLOCKS // 2):
        cs = slice(p * pw, (p + 1) * pw)
        z = jnp.dot(ub[:, cs], wd_ref[p], preferred_element_type=F32)
        rg = jax.nn.sigmoid(z[:, :pw] + br_ref[:, cs])
        ig = jax.nn.sigmoid(z[:, pw:] + bi_ref[:, cs])
        log_a = -LRU_C * rg * softplus_neg[:, cs]
        a = jnp.exp(log_a)
        gain = jnp.sqrt(-jnp.tanh(log_a) * (a * a + 1.0))
        a_scr[:, :, cs] = a.reshape(tl, nb, pw)
        b_scr[:, :, cs] = (gain * (ig * u[:, cs])).reshape(tl, nb, pw)


def _scan_kernel(fm_ref, fp_ref, fn_ref, bm_ref, bp_ref, bn_ref, h0f_ref, h0b_ref,
                 cw_ref, cb_ref, wdf_ref, wdb_ref, br_ref, bi_ref, lam_ref,
                 hf_ref, hb_ref, lastf_ref, lastb_ref,
                 af_scr, bf_scr, ab_scr, bb_scr, sf_scr, sb_scr):
    c, nc = pl.program_id(0), pl.num_programs(0)
    tl = fm_ref.shape[0]

    @pl.when(c == 0)
    def _():
        sf_scr[...] = h0f_ref[...]
        sb_scr[...] = h0b_ref[...]

    _lru_coeffs(fm_ref, fp_ref, fn_ref, c == 0, c == nc - 1, cw_ref, cb_ref, wdf_ref,
                br_ref.at[0:1], bi_ref.at[0:1], lam_ref.at[0:1], af_scr, bf_scr)
    _lru_coeffs(bm_ref, bp_ref, bn_ref, c == nc - 1, c == 0, cw_ref, cb_ref, wdb_ref,
                br_ref.at[1:2], bi_ref.at[1:2], lam_ref.at[1:2], ab_scr, bb_scr)

    def step(t, carry):
        hf, hb = carry
        hf = af_scr[t] * hf + bf_scr[t]
        hf_ref[t] = hf
        tb = tl - 1 - t
        hb = ab_scr[tb] * hb + bb_scr[tb]
        hb_ref[tb] = hb
        return hf, hb

    hf, hb = lax.fori_loop(0, tl, step, (sf_scr[...], sb_scr[...]))
    sf_scr[...] = hf
    sb_scr[...] = hb

    @pl.when(c == nc - 1)
    def _():
        lastf_ref[...] = hf
        lastb_ref[...] = hb


def _scan_call(xb, h0f, h0b, conv_w, conv_b, wd, b_r, b_i, lam, steps):
    ls, nb, r = xb.shape
    tl = min(steps, ls)
    nc = ls // tl
    fwd, bwd = (lambda c: c), (lambda c: nc - 1 - c)
    main = lambda ch: pl.BlockSpec((tl, nb, r), lambda c: (ch(c), 0, 0))
    prev2 = lambda ch: pl.BlockSpec((2, nb, r), lambda c: (jnp.maximum(ch(c) * (tl // 2) - 1, 0), 0, 0))
    next1 = lambda ch: pl.BlockSpec((1, nb, r), lambda c: (jnp.minimum((ch(c) + 1) * tl, ls - 1), 0, 0))
    full = lambda a: pl.BlockSpec(a.shape, lambda c: (0,) * a.ndim)
    state = pl.BlockSpec((nb, r), lambda c: (0, 0))
    wd_spec = lambda dirn: pl.BlockSpec((None,) + wd.shape[1:], lambda c: (dirn, 0, 0, 0))
    return pl.pallas_call(
        _scan_kernel,
        grid=(nc,),
        in_specs=[main(fwd), prev2(fwd), next1(fwd), main(bwd), prev2(bwd), next1(bwd),
                  state, state, full(conv_w), full(conv_b), wd_spec(0), wd_spec(1),
                  full(b_r), full(b_i), full(lam)],
        out_specs=[main(fwd), main(bwd), state, state],
        out_shape=[jax.ShapeDtypeStruct((ls, nb, r), F32), jax.ShapeDtypeStruct((ls, nb, r), F32),
                   jax.ShapeDtypeStruct((nb, r), F32), jax.ShapeDtypeStruct((nb, r), F32)],
        scratch_shapes=[pltpu.VMEM((tl, nb, r), F32)] * 4 + [pltpu.VMEM((nb, r), F32)] * 2,
        compiler_params=_cparams(("arbitrary",), VMEM_LIMIT_BYTES),
        name="rglru_scan",
    )(xb, xb, xb, xb, xb, xb, h0f, h0b, conv_w, conv_b, wd, wd, b_r, b_i, lam)


def _paired_block_diag(w_r, w_i):
    def pair(w):
        ndir = w.shape[0]
        w = w.reshape(ndir, LRU_BLOCKS // 2, 2, LRU_BLOCK_DIM, LRU_BLOCK_DIM)
        z = jnp.zeros_like(w[:, :, 0])
        top = jnp.concatenate([w[:, :, 0], z], axis=-1)
        bot = jnp.concatenate([z, w[:, :, 1]], axis=-1)
        return jnp.concatenate([top, bot], axis=-2)
    return jnp.concatenate([pair(w_r), pair(w_i)], axis=-1).astype(BF16)


def _outproj_kernel(gate_ref, hf_ref, hb_ref, w_ref, x_ref, mod_ref, g_ref, b_ref, out_ref):
    d = x_ref.shape[-1]
    gt = gate_ref[...]
    gelu = 0.5 * gt * (1.0 + jnp.tanh(0.7978845608028654 * (gt + 0.044715 * (gt * gt * gt))))
    y = gelu * (hf_ref[...] + hb_ref[...])
    proj = jnp.dot(y.astype(BF16), w_ref[...], preferred_element_type=F32)
    out_ref[...] = _layer_norm(ALPHA * x_ref[...] + _mod_slice(mod_ref, 2, d) * proj,
                               g_ref[...], b_ref[...])


def _outproj_call(gate, hf, hb, w_out, x, mod, ln_g, ln_b, rows):
    b, l, d = x.shape
    r = gate.shape[-1]
    tm = min(rows, l)
    tmajor = pl.BlockSpec((tm, r), lambda i, j: (j, i))
    return pl.pallas_call(
        _outproj_kernel,
        grid=(b, l // tm),
        in_specs=[
            pl.BlockSpec((None, tm, r), lambda i, j: (i, j, 0)),
            tmajor, tmajor,
            pl.BlockSpec(w_out.shape, lambda i, j: (0, 0)),
            pl.BlockSpec((None, tm, d), lambda i, j: (i, j, 0)),
            pl.BlockSpec((None, 1, 6 * d), lambda i, j: (i, 0, 0)),
            pl.BlockSpec((1, d), lambda i, j: (0, 0)),
            pl.BlockSpec((1, d), lambda i, j: (0, 0)),
        ],
        out_specs=pl.BlockSpec((None, tm, d), lambda i, j: (i, j, 0)),
        out_shape=jax.ShapeDtypeStruct((b, l, d), F32),
        compiler_params=_cparams(("arbitrary", "arbitrary"), VMEM_LIMIT_BYTES),
        name="lru_out_proj",
    )(gate, hf, hb, w_out, x, mod, ln_g, ln_b)


ROUTE_IDX_LANE = N_EXPERTS
ROUTE_W_LANE = N_EXPERTS + TOP_K


def _router_kernel(x_ref, mod_ref, wr_ref, h_ref, info_ref):
    d = x_ref.shape[-1]
    h = x_ref[...] * (1.0 + _mod_slice(mod_ref, 4, d)) + _mod_slice(mod_ref, 3, d)
    h_ref[...] = h
    logits = jnp.dot(h, wr_ref[...], preferred_element_type=F32, precision=lax.Precision.HIGHEST)
    lane = lax.broadcasted_iota(jnp.int32, logits.shape, 1)
    logits = jnp.where(lane < N_EXPERTS, logits, -jnp.inf)
    m1 = jnp.max(logits, axis=-1, keepdims=True)
    i1 = jnp.min(jnp.where(logits == m1, lane, LANES), axis=-1, keepdims=True)
    rest = jnp.where(lane == i1, -jnp.inf, logits)
    m2 = jnp.max(rest, axis=-1, keepdims=True)
    i2 = jnp.min(jnp.where(rest == m2, lane, LANES), axis=-1, keepdims=True)
    e2 = jnp.exp(m2 - m1)
    w1 = 1.0 / (1.0 + e2)
    w2 = e2 / (1.0 + e2)
    info = jnp.where(lane == i1, w1, jnp.where(lane == i2, w2, 0.0))
    info = jnp.where(lane == ROUTE_IDX_LANE, i1.astype(F32), info)
    info = jnp.where(lane == ROUTE_IDX_LANE + 1, i2.astype(F32), info)
    info = jnp.where(lane == ROUTE_W_LANE, w1, info)
    info = jnp.where(lane == ROUTE_W_LANE + 1, w2, info)
    info_ref[...] = info


def _router_call(x, mod, router_padded, rows):
    b, l, d = x.shape
    tm = min(rows, l)
    nt = l // tm
    return pl.pallas_call(
        _router_kernel,
        grid=(b, nt),
        in_specs=[
            pl.BlockSpec((None, tm, d), lambda i, j: (i, j, 0)),
            pl.BlockSpec((None, 1, 6 * d), lambda i, j: (i, 0, 0)),
            pl.BlockSpec(router_padded.shape, lambda i, j: (0, 0)),
        ],
        out_specs=[
            pl.BlockSpec((tm, d), lambda i, j: (i * nt + j, 0)),
            pl.BlockSpec((tm, LANES), lambda i, j: (i * nt + j, 0)),
        ],
        out_shape=[jax.ShapeDtypeStruct((b * l, d), F32), jax.ShapeDtypeStruct((b * l, LANES), F32)],
        compiler_params=_cparams(("arbitrary", "arbitrary"), VMEM_LIMIT_BYTES),
        name="moe_router",
    )(x, mod, router_padded)


def _row_copy(src_hbm, row, dst_buf, slot_row, sem):
    return pltpu.make_async_copy(src_hbm.at[pl.ds(row, 1)], dst_buf.at[pl.ds(slot_row, 1)], sem)


def _start_row_gather(idx_ref, src_hbm, dst_buf, sem):
    def body(r, _):
        _row_copy(src_hbm, idx_ref[0, r], dst_buf, r, sem).start()
        return 0
    lax.fori_loop(0, dst_buf.shape[0], body, 0)


def _wait_row_gather(src_hbm, dst_buf, sem):
    def body(r, _):
        _row_copy(src_hbm, 0, dst_buf, r, sem).wait()
        return 0
    lax.fori_loop(0, dst_buf.shape[0], body, 0)


def _expert_kernel(te_ref, nv_ref, cur_ref, nxt_ref, h_hbm, wg_ref, wu_ref, wd_ref, y_ref,
                   xbuf, sems, hb_scr, acc_scr):
    i, f = pl.program_id(0), pl.program_id(1)
    nt, nf = pl.num_programs(0), pl.num_programs(1)
    slot = i % 2
    valid = i < nv_ref[0]

    @pl.when((f == 0) & (i == 0))
    def _():
        _start_row_gather(cur_ref, h_hbm, xbuf.at[0], sems.at[0])

    @pl.when(f == 0)
    def _():
        @pl.when(i + 1 < nt)
        def _():
            _start_row_gather(nxt_ref, h_hbm, xbuf.at[1 - slot], sems.at[1 - slot])

        _wait_row_gather(h_hbm, xbuf.at[slot], sems.at[slot])
        hb_scr[...] = xbuf[slot].astype(BF16)
        acc_scr[...] = jnp.zeros_like(acc_scr)

    @pl.when(valid)
    def _():
        acc_scr[...] += _swiglu_chunk(hb_scr[...], wg_ref, wu_ref, wd_ref)

    @pl.when(f == nf - 1)
    def _():
        y_ref[...] = acc_scr[...]


def _expert_call(tile_expert, n_valid, src_rows, h, w_gu, w_dn, rows, cols):
    nt = src_rows.shape[0]
    d = h.shape[1]
    ff = w_dn.shape[1]
    fc = cols
    nf = ff // fc
    idx_spec = lambda f: pl.BlockSpec((None, 1, rows), f, memory_space=pltpu.SMEM)
    grid_spec = pltpu.PrefetchScalarGridSpec(
        num_scalar_prefetch=2,
        grid=(nt, nf),
        in_specs=[
            idx_spec(lambda i, f, te, nv: (i, 0, 0)),
            idx_spec(lambda i, f, te, nv: (jnp.minimum(i + 1, nt - 1), 0, 0)),
            pl.BlockSpec(memory_space=pl.ANY),
            pl.BlockSpec((None, d, fc), lambda i, f, te, nv: (te[i], 0, f)),
            pl.BlockSpec((None, d, fc), lambda i, f, te, nv: (te[i], 0, nf + f)),
            pl.BlockSpec((None, fc, d), lambda i, f, te, nv: (te[i], f, 0)),
        ],
        out_specs=pl.BlockSpec((rows, d), lambda i, f, te, nv: (i, 0)),
        scratch_shapes=[pltpu.VMEM((2, rows, d), F32), pltpu.SemaphoreType.DMA((2,)),
                        pltpu.VMEM((rows, d), BF16), pltpu.VMEM((rows, d), F32)],
    )
    return pl.pallas_call(
        _expert_kernel,
        grid_spec=grid_spec,
        out_shape=jax.ShapeDtypeStruct((nt * rows, d), F32),
        compiler_params=_cparams(("arbitrary", "arbitrary"), VMEM_LIMIT_BYTES),
        name="expert_swiglu",
    )(tile_expert, n_valid, src_rows, src_rows, h, w_gu, w_gu, w_dn)


def _combine_kernel(cur_ref, nxt_ref, y_hbm, info_ref, x_ref, mod_ref, g_ref, b_ref, out_ref,
                    ybuf, sems):
    b, j = pl.program_id(0), pl.program_id(1)
    nj = pl.num_programs(1)
    i = b * nj + j
    nt = pl.num_programs(0) * nj
    slot = i % 2
    d = x_ref.shape[-1]
    tm = x_ref.shape[0]

    @pl.when(i == 0)
    def _():
        _start_row_gather(cur_ref, y_hbm, ybuf.at[0], sems.at[0])

    @pl.when(i + 1 < nt)
    def _():
        _start_row_gather(nxt_ref, y_hbm, ybuf.at[1 - slot], sems.at[1 - slot])

    _wait_row_gather(y_hbm, ybuf.at[slot], sems.at[slot])
    info = info_ref[...]
    w1 = info[:, ROUTE_W_LANE:ROUTE_W_LANE + 1]
    w2 = info[:, ROUTE_W_LANE + 1:ROUTE_W_LANE + 2]
    y = ybuf[slot]
    f = w1 * y[:tm] + w2 * y[tm:]
    out_ref[...] = _layer_norm(ALPHA * x_ref[...] + _mod_slice(mod_ref, 5, d) * f,
                               g_ref[...], b_ref[...])


def _combine_call(pos, y, info, x, mod, ln_g, ln_b, rows):
    b, l, d = x.shape
    tm = rows
    nj = l // tm
    nt = b * nj
    idx_spec = lambda f: pl.BlockSpec((None, 1, 2 * tm), f, memory_space=pltpu.SMEM)
    return pl.pallas_call(
        _combine_kernel,
        grid=(b, nj),
        in_specs=[
            idx_spec(lambda i, j: (i * nj + j, 0, 0)),
            idx_spec(lambda i, j: (jnp.minimum(i * nj + j + 1, nt - 1), 0, 0)),
            pl.BlockSpec(memory_space=pl.ANY),
            pl.BlockSpec((tm, LANES), lambda i, j: (i * nj + j, 0)),
            pl.BlockSpec((None, tm, d), lambda i, j: (i, j, 0)),
            pl.BlockSpec((None, 1, 6 * d), lambda i, j: (i, 0, 0)),
            pl.BlockSpec((1, d), lambda i, j: (0, 0)),
            pl.BlockSpec((1, d), lambda i, j: (0, 0)),
        ],
        out_specs=pl.BlockSpec((None, tm, d), lambda i, j: (i, j, 0)),
        out_shape=jax.ShapeDtypeStruct((b, l, d), F32),
        scratch_shapes=[pltpu.VMEM((2, 2 * tm, d), F32), pltpu.SemaphoreType.DMA((2,))],
        compiler_params=_cparams(("arbitrary", "arbitrary"), VMEM_LIMIT_BYTES),
        name="expert_combine",
    )(pos, pos, y, info, x, mod, ln_g, ln_b)


def _dispatch_plan(info, rows, combine_rows):
    n = info.shape[0]
    experts = info[:, ROUTE_IDX_LANE:ROUTE_IDX_LANE + TOP_K].astype(jnp.int32).reshape(-1)
    onehot = (experts[:, None] == jnp.arange(N_EXPERTS)[None, :]).astype(jnp.int32)
    rank = jnp.take_along_axis(jnp.cumsum(onehot, axis=0) - onehot, experts[:, None], axis=1)[:, 0]
    counts = jnp.sum(onehot, axis=0)
    padded = ((counts + rows - 1) // rows) * rows
    ends = jnp.cumsum(padded)
    starts = ends - padded
    pos = starts[experts] + rank
    n_tiles = (TOP_K * n) // rows + N_EXPERTS
    src = jnp.zeros((n_tiles * rows,), jnp.int32).at[pos].set(jnp.arange(TOP_K * n) // TOP_K)
    tile_start = jnp.arange(n_tiles) * rows
    tile_expert = jnp.minimum(jnp.searchsorted(ends, tile_start, side="right"), N_EXPERTS - 1)
    n_valid = (ends[-1] // rows).reshape(1)
    pos2 = pos.reshape(n // combine_rows, combine_rows, TOP_K)
    pos2 = jnp.transpose(pos2, (0, 2, 1)).reshape(n // combine_rows, 1, TOP_K * combine_rows)
    return (tile_expert.astype(jnp.int32), n_valid.astype(jnp.int32),
            src.reshape(n_tiles, 1, rows), pos2.astype(jnp.int32))


def _attention_layer(x, ctx, mod_x, mod_c, ln_g, ln_b, w_qkv, w_o, sink, w_gu, w_dn, grid_w, ts):
    b, l, d = x.shape
    lc = ctx.shape[1]
    g0, b0, g1, b1 = ln_g[0:1], ln_b[0:1], ln_g[1:2], ln_b[1:2]
    w_qkv, w_o, w_gu, w_dn = (w.astype(BF16) for w in (w_qkv, w_o, w_gu, w_dn))
    ctx_flat = ctx.reshape(1, b * lc, d)
    n_kv = N_KV_HEADS * HEAD_DIM

    q, k, v = _qkv_call(x, mod_x, w_qkv, _rope_tables(l, grid_w), ts["qkv_rows"])
    qc, kc, vc = _qkv_call(ctx_flat, mod_c, w_qkv, _no_rope_tables(b * lc), ts["qkv_rows"])
    qc = qc.reshape(b, lc, -1)
    kc, vc = kc.reshape(b, lc, n_kv), vc.reshape(b, lc, n_kv)
    mod_cb = jnp.broadcast_to(mod_c, (b,) + mod_c.shape[1:])

    x = _attn_call(q, k, v, kc, vc, sink, w_o, x, mod_x, g0, b0, band=True)
    ctx = _attn_call(qc, None, None, kc, vc, sink, w_o, ctx, mod_cb, g0, b0, band=False)

    x = _ffn_call(x, mod_x, w_gu, w_dn, g1, b1, ts["ffn_rows"], ts["ffn_cols"])
    ctx = _ffn_call(ctx.reshape(1, b * lc, d), mod_c, w_gu, w_dn, g1, b1,
                    2 * ts["ffn_rows"], ts["ffn_cols"]).reshape(b, lc, d)
    return x, ctx


def _recurrent_layer(x, ctx, mod_x, mod_c, ln_g, ln_b, w_in, conv_w, conv_b, lam, w_r, b_r, w_i,
                     b_i, w_out, router, w_gu, w_dn, ts):
    b, l, d = x.shape
    g0, b0, g1, b1 = ln_g[0:1], ln_b[0:1], ln_g[1:2], ln_b[1:2]
    w_in, w_out, w_gu, w_dn = (w.astype(BF16) for w in (w_in, w_out, w_gu, w_dn))
    wd = _paired_block_diag(w_r, w_i)
    conv_b = conv_b.reshape(1, -1)
    mod_cb = jnp.broadcast_to(mod_c, (b,) + mod_c.shape[1:])

    r = w_out.shape[0]
    gate, xb = _inproj_call(x, mod_x, w_in, ts["proj_rows"])
    _, xb_c = _inproj_call(ctx, mod_cb, w_in, ts["proj_rows"])
    zero = jnp.zeros((b, r), F32)
    scan = functools.partial(_scan_call, conv_w=conv_w, conv_b=conv_b, wd=wd, b_r=b_r, b_i=b_i,
                             lam=lam, steps=ts["scan_steps"])
    _, _, cf, cb = scan(xb_c.reshape(-1, b, r), zero, zero)
    hf, hb, _, _ = scan(xb.reshape(l, b, r), cf, cb)
    x = _outproj_call(gate, hf.reshape(l, b * r), hb.reshape(l, b * r), w_out, x, mod_x, g0, b0,
                      ts["proj_rows"])

    router_padded = jnp.pad(router, ((0, 0), (0, LANES - N_EXPERTS)))
    h, info = _router_call(x, mod_x, router_padded, ts["route_rows"])
    tile_expert, n_valid, src, pos = _dispatch_plan(info, ts["moe_rows"], ts["combine_rows"])
    y = _expert_call(tile_expert, n_valid, src, h, w_gu, w_dn, ts["moe_rows"], ts["ffn_cols"])
    return _combine_call(pos, y, info, x, mod_x, g1, b1, ts["combine_rows"])


def kernel(x, c, ctx, c_ctx, w_mod, b_mod, ln_g, ln_b, attn_w_qkv, attn_w_o, attn_sink, ffn_w_gu,
           ffn_w_dn, lru_w_in, lru_conv_w, lru_conv_b, lru_lambda, lru_w_r, lru_b_r, lru_w_i,
           lru_b_i, lru_w_out, moe_router, moe_w_gu, moe_w_dn):
    b, l, d = x.shape
    assert w_mod.shape[0] == DEPTH and l % BLOCK == 0 and ctx.shape[1] % BLOCK == 0
    grid_w = 64
    ts = _tile_sizes()

    rows = -(-(b + 1) // SUBLANES) * SUBLANES
    cond = jnp.zeros((rows, d), F32).at[:b].set(c).at[b].set(c_ctx)
    mod = _mod_call(cond, w_mod, b_mod[:, None, :])
    mod_x = lambda i: mod[i, :b, None, :]
    mod_c = lambda i: mod[i, b:b + 1, None, :]

    x, ctx = _attention_layer(x, ctx, mod_x(0), mod_c(0), ln_g[0], ln_b[0], attn_w_qkv[0],
                              attn_w_o[0], attn_sink[0], ffn_w_gu[0], ffn_w_dn[0], grid_w, ts)
    return _recurrent_layer(x, ctx, mod_x(1), mod_c(1), ln_g[1], ln_b[1], lru_w_in[0],
                            lru_conv_w[0], lru_conv_b[0], lru_lambda[0], lru_w_r[0], lru_b_r[0],
                            lru_w_i[0], lru_b_i[0], lru_w_out[0], moe_router[0], moe_w_gu[0],
                            moe_w_dn[0], ts)
```

```python
import functools

import jax
import jax.numpy as jnp
from jax import lax
from jax.experimental import pallas as pl
from jax.experimental.pallas import tpu as pltpu

N_HEADS = 8
N_KV_HEADS = 2
GROUP = N_HEADS // N_KV_HEADS
HEAD_DIM = 128
BLOCK = 128
AXIS_DIM = HEAD_DIM // 2
ROPE_BASE = 10000.0
LRU_BLOCKS = 8
LRU_BLOCK_DIM = 128
LRU_C = 8.0
N_EXPERTS = 8
TOP_K = 2
DEPTH = 2
ALPHA = (2 * DEPTH) ** 0.25
LN_EPS = 1e-5
NEG_INF = -1e30

LANES = 128
SUBLANES = 8
VMEM_LIMIT_BYTES = 56 * 1024 * 1024

F32 = jnp.float32
BF16 = jnp.bfloat16


def _tile_sizes():
    return dict(
        qkv_rows=512,
        ffn_rows=1024,
        ffn_cols=512,
        proj_rows=512,
        scan_steps=64,
        route_rows=512,
        moe_window=512,
        moe_rows=512,
    )


def _cparams(sem, vmem=None):
    return pltpu.CompilerParams(dimension_semantics=sem, vmem_limit_bytes=vmem)


def _layer_norm(h, g, b):
    mu = jnp.mean(h, axis=-1, keepdims=True)
    d = h - mu
    var = jnp.mean(d * d, axis=-1, keepdims=True)
    return d * lax.rsqrt(var + LN_EPS) * g + b


def _mod_slice(mod_ref, k, d):
    return mod_ref[:, k * d:(k + 1) * d]


def _mod_kernel(c_ref, w_ref, b_ref, o_ref):
    c = c_ref[...]
    s = c * jax.nn.sigmoid(c)
    o_ref[...] = jnp.dot(s, w_ref[...], preferred_element_type=F32,
                         precision=lax.Precision.HIGHEST) + b_ref[...]


def _mod_call(cond, w_mod, b_mod):
    rows, d = cond.shape
    depth, _, n = w_mod.shape
    tn = 1536
    return pl.pallas_call(
        _mod_kernel,
        grid=(depth, n // tn),
        in_specs=[
            pl.BlockSpec((rows, d), lambda i, j: (0, 0)),
            pl.BlockSpec((None, d, tn), lambda i, j: (i, 0, j)),
            pl.BlockSpec((None, 1, tn), lambda i, j: (i, 0, j)),
        ],
        out_specs=pl.BlockSpec((None, rows, tn), lambda i, j: (i, 0, j)),
        out_shape=jax.ShapeDtypeStruct((depth, rows, n), F32),
        compiler_params=_cparams(("arbitrary", "arbitrary")),
        name="mod_vectors",
    )(cond, w_mod, b_mod)


def _qkv_kernel(x_ref, mod_ref, w_ref, cos_ref, sa_ref, sb_ref, q_ref, k_ref, v_ref):
    d = x_ref.shape[-1]
    h = x_ref[...] * (1.0 + _mod_slice(mod_ref, 1, d)) + _mod_slice(mod_ref, 0, d)
    qkv = jnp.dot(h.astype(BF16), w_ref[...], preferred_element_type=F32)
    cos, sa, sb = cos_ref[...], sa_ref[...], sb_ref[...]
    scale = HEAD_DIM ** -0.5
    for head in range(N_HEADS + N_KV_HEADS):
        u = qkv[:, head * HEAD_DIM:(head + 1) * HEAD_DIM]
        r = (u * cos + pltpu.roll(u, HEAD_DIM - AXIS_DIM // 2, axis=1) * sa
             + pltpu.roll(u, AXIS_DIM // 2, axis=1) * sb)
        if head < N_HEADS:
            q_ref[:, head * HEAD_DIM:(head + 1) * HEAD_DIM] = (r * scale).astype(BF16)
        else:
            kh = head - N_HEADS
            k_ref[:, kh * HEAD_DIM:(kh + 1) * HEAD_DIM] = r.astype(BF16)
    kv = N_KV_HEADS * HEAD_DIM
    v_ref[...] = qkv[:, N_HEADS * HEAD_DIM + kv:].astype(BF16)


def _qkv_call(x, mod, w_qkv, tabs, rows):
    b, l, d = x.shape
    n_q, n_kv = N_HEADS * HEAD_DIM, N_KV_HEADS * HEAD_DIM
    tm = min(rows, l)
    tab_spec = pl.BlockSpec((tm, HEAD_DIM), lambda i, j: (j, 0))
    return pl.pallas_call(
        _qkv_kernel,
        grid=(b, l // tm),
        in_specs=[
            pl.BlockSpec((None, tm, d), lambda i, j: (i, j, 0)),
            pl.BlockSpec((None, 1, 6 * d), lambda i, j: (i, 0, 0)),
            pl.BlockSpec(w_qkv.shape, lambda i, j: (0, 0)),
            tab_spec, tab_spec, tab_spec,
        ],
        out_specs=[
            pl.BlockSpec((None, tm, n_q), lambda i, j: (i, j, 0)),
            pl.BlockSpec((None, tm, n_kv), lambda i, j: (i, j, 0)),
            pl.BlockSpec((None, tm, n_kv), lambda i, j: (i, j, 0)),
        ],
        out_shape=[
            jax.ShapeDtypeStruct((b, l, n_q), BF16),
            jax.ShapeDtypeStruct((b, l, n_kv), BF16),
            jax.ShapeDtypeStruct((b, l, n_kv), BF16),
        ],
        compiler_params=_cparams(("arbitrary", "arbitrary"), VMEM_LIMIT_BYTES),
        name="qkv_rope",
    )(x, mod, w_qkv, *tabs)


def _rope_tables(l, grid_w):
    t = jnp.arange(l)
    pos = jnp.stack([(t // grid_w).astype(F32), (t % grid_w).astype(F32)], axis=1)
    freqs = ROPE_BASE ** (-jnp.arange(0, AXIS_DIM, 2, dtype=F32) / AXIS_DIM)
    lane = jnp.arange(HEAD_DIM)
    ang = pos[:, lane // AXIS_DIM] * freqs[lane % (AXIS_DIM // 2)][None, :]
    first = (lane % AXIS_DIM) < AXIS_DIM // 2
    cos, sin = jnp.cos(ang), jnp.sin(ang)
    return cos, jnp.where(first, -sin, 0.0), jnp.where(first, 0.0, sin)


def _no_rope_tables(l):
    z = jnp.zeros((l, HEAD_DIM), F32)
    return jnp.ones((l, HEAD_DIM), F32), z, z


def _nt_dot(a, b):
    return lax.dot_general(a, b, (((1,), (1,)), ((), ())), preferred_element_type=F32)


def _attend(qs, segs, sink_col):
    k_all = jnp.concatenate([k for k, _, _ in segs], axis=0)
    v_all = jnp.concatenate([v for _, v, _ in segs], axis=0)
    v_ext = jnp.concatenate([v_all, jnp.ones_like(v_all)], axis=1)
    s_all = _nt_dot(qs, k_all)
    tiles, col = [], 0
    for k, _, mask in segs:
        for c in range(col, col + k.shape[0], LANES):
            s = s_all[:, c:c + LANES]
            tiles.append(s if mask is None else jnp.where(mask, s, NEG_INF))
        col += k.shape[0]
    m_tile = tiles[0]
    for s in tiles[1:]:
        m_tile = jnp.maximum(m_tile, s)
    m = jnp.maximum(sink_col, jnp.max(m_tile, axis=-1, keepdims=True))
    p_all = jnp.concatenate([jnp.exp(s - m).astype(BF16) for s in tiles], axis=1)
    res = jnp.dot(p_all, v_ext, preferred_element_type=F32)
    hd = v_all.shape[1]
    return res[:, :hd] / (res[:, hd:hd + 1] + jnp.exp(sink_col - m))


def _attn_epilogue(o_scr, wo_ref, x_ref, mod_ref, g_ref, b_ref, out_ref):
    d = x_ref.shape[-1]
    proj = jnp.dot(o_scr[...], wo_ref[...], preferred_element_type=F32)
    out_ref[...] = _layer_norm(ALPHA * x_ref[...] + _mod_slice(mod_ref, 2, d) * proj,
                               g_ref[...], b_ref[...])


def _sink_column(sink_ref, kvh):
    return jnp.concatenate(
        [jnp.full((BLOCK, 1), sink_ref[kvh * GROUP + g], F32) for g in range(GROUP)], axis=0)


def _stack_group(q_ref, kvh):
    return jnp.concatenate(
        [q_ref[:, (kvh * GROUP + g) * HEAD_DIM:(kvh * GROUP + g + 1) * HEAD_DIM]
         for g in range(GROUP)], axis=0)


def _unstack_group(o, o_scr, kvh):
    for g in range(GROUP):
        col = (kvh * GROUP + g) * HEAD_DIM
        o_scr[:, col:col + HEAD_DIM] = o[g * BLOCK:(g + 1) * BLOCK].astype(BF16)


def _band_attn_kernel(sink_ref, q_ref, kp_ref, kc_ref, kn_ref, vp_ref, vc_ref, vn_ref,
                      kx_ref, vx_ref, wo_ref, x_ref, mod_ref, g_ref, b_ref, out_ref, o_scr):
    n, nb = pl.program_id(1), pl.num_programs(1)
    row = lax.broadcasted_iota(jnp.int32, (GROUP * BLOCK, BLOCK), 0) % BLOCK
    col = lax.broadcasted_iota(jnp.int32, (GROUP * BLOCK, BLOCK), 1)
    mask_prev = (col >= row) & (n > 0)
    mask_next = (col <= row) & (n < nb - 1)
    for kvh in range(N_KV_HEADS):
        hs = slice(kvh * HEAD_DIM, (kvh + 1) * HEAD_DIM)
        segs = [(kp_ref[:, hs], vp_ref[:, hs], mask_prev),
                (kc_ref[:, hs], vc_ref[:, hs], None),
                (kn_ref[:, hs], vn_ref[:, hs], mask_next),
                (kx_ref[:, hs], vx_ref[:, hs], None)]
        o = _attend(_stack_group(q_ref, kvh), segs, _sink_column(sink_ref, kvh))
        _unstack_group(o, o_scr, kvh)
    _attn_epilogue(o_scr, wo_ref, x_ref, mod_ref, g_ref, b_ref, out_ref)


def _ctx_attn_kernel(sink_ref, q_ref, kx_ref, vx_ref, wo_ref, x_ref, mod_ref, g_ref, b_ref,
                     out_ref, o_scr):
    for kvh in range(N_KV_HEADS):
        hs = slice(kvh * HEAD_DIM, (kvh + 1) * HEAD_DIM)
        o = _attend(_stack_group(q_ref, kvh), [(kx_ref[:, hs], vx_ref[:, hs], None)],
                    _sink_column(sink_ref, kvh))
        _unstack_group(o, o_scr, kvh)
    _attn_epilogue(o_scr, wo_ref, x_ref, mod_ref, g_ref, b_ref, out_ref)


def _attn_call(q, k, v, kx, vx, sink, w_o, x, mod, ln_g, ln_b, band):
    b, l, d = x.shape
    lc = kx.shape[1]
    nb = l // BLOCK
    n_q, n_kv = N_HEADS * HEAD_DIM, N_KV_HEADS * HEAD_DIM
    kv_blk = lambda f: pl.BlockSpec((None, BLOCK, n_kv), f)
    common_in = [
        pl.BlockSpec((None, lc, n_kv), lambda i, j: (i, 0, 0)),
        pl.BlockSpec((None, lc, n_kv), lambda i, j: (i, 0, 0)),
        pl.BlockSpec(w_o.shape, lambda i, j: (0, 0)),
        pl.BlockSpec((None, BLOCK, d), lambda i, j: (i, j, 0)),
        pl.BlockSpec((None, 1, 6 * d), lambda i, j: (i, 0, 0)),
        pl.BlockSpec((1, d), lambda i, j: (0, 0)),
        pl.BlockSpec((1, d), lambda i, j: (0, 0)),
    ]
    head_in = [pl.BlockSpec(memory_space=pltpu.SMEM),
               pl.BlockSpec((None, BLOCK, n_q), lambda i, j: (i, j, 0))]
    if band:
        prev = lambda i, j: (i, jnp.maximum(j - 1, 0), 0)
        cur = lambda i, j: (i, j, 0)
        nxt = lambda i, j: (i, jnp.minimum(j + 1, nb - 1), 0)
        in_specs = head_in + [kv_blk(prev), kv_blk(cur), kv_blk(nxt)] * 2 + common_in
        args = (sink, q, k, k, k, v, v, v, kx, vx, w_o, x, mod, ln_g, ln_b)
        body, name = _band_attn_kernel, "band_attention"
    else:
        in_specs = head_in + common_in
        args = (sink, q, kx, vx, w_o, x, mod, ln_g, ln_b)
        body, name = _ctx_attn_kernel, "context_attention"
    return pl.pallas_call(
        body,
        grid=(b, nb),
        in_specs=in_specs,
        out_specs=pl.BlockSpec((None, BLOCK, d), lambda i, j: (i, j, 0)),
        out_shape=jax.ShapeDtypeStruct((b, l, d), F32),
        scratch_shapes=[pltpu.VMEM((BLOCK, n_q), BF16)],
        compiler_params=_cparams(("arbitrary", "arbitrary"), VMEM_LIMIT_BYTES),
        name=name,
    )(*args)


def _swiglu_chunk(hb, wg_ref, wu_ref, wd_ref):
    g = jnp.dot(hb, wg_ref[...], preferred_element_type=F32)
    u = jnp.dot(hb, wu_ref[...], preferred_element_type=F32)
    a = (g * jax.nn.sigmoid(g)) * u
    return jnp.dot(a.astype(BF16), wd_ref[...], preferred_element_type=F32)


def _ffn_kernel(x_ref, mod_ref, wg_ref, wu_ref, wd_ref, g_ref, b_ref, out_ref, hb_scr, acc_scr):
    f, nf = pl.program_id(2), pl.num_programs(2)
    d = x_ref.shape[-1]

    @pl.when(f == 0)
    def _():
        h = x_ref[...] * (1.0 + _mod_slice(mod_ref, 4, d)) + _mod_slice(mod_ref, 3, d)
        hb_scr[...] = h.astype(BF16)
        acc_scr[...] = jnp.zeros_like(acc_scr)

    acc_scr[...] += _swiglu_chunk(hb_scr[...], wg_ref, wu_ref, wd_ref)

    @pl.when(f == nf - 1)
    def _():
        out_ref[...] = _layer_norm(ALPHA * x_ref[...] + _mod_slice(mod_ref, 5, d) * acc_scr[...],
                                   g_ref[...], b_ref[...])


def _ffn_call(x, mod, w_gu, w_dn, ln_g, ln_b, rows, cols):
    b, l, d = x.shape
    ff = w_dn.shape[0]
    tm, fc = min(rows, l), cols
    nf = ff // fc
    return pl.pallas_call(
        _ffn_kernel,
        grid=(b, l // tm, nf),
        in_specs=[
            pl.BlockSpec((None, tm, d), lambda i, j, f: (i, j, 0)),
            pl.BlockSpec((None, 1, 6 * d), lambda i, j, f: (i, 0, 0)),
            pl.BlockSpec((d, fc), lambda i, j, f: (0, f)),
            pl.BlockSpec((d, fc), lambda i, j, f: (0, nf + f)),
            pl.BlockSpec((fc, d), lambda i, j, f: (f, 0)),
            pl.BlockSpec((1, d), lambda i, j, f: (0, 0)),
            pl.BlockSpec((1, d), lambda i, j, f: (0, 0)),
        ],
        out_specs=pl.BlockSpec((None, tm, d), lambda i, j, f: (i, j, 0)),
        out_shape=jax.ShapeDtypeStruct((b, l, d), F32),
        scratch_shapes=[pltpu.VMEM((tm, d), BF16), pltpu.VMEM((tm, d), F32)],
        compiler_params=_cparams(("arbitrary", "arbitrary", "arbitrary"), VMEM_LIMIT_BYTES),
        name="dense_swiglu",
    )(x, mod, w_gu, w_gu, w_dn, ln_g, ln_b)


def _inproj_kernel(x_ref, mod_ref, w_ref, gate_ref, xb_ref):
    d = x_ref.shape[-1]
    r = xb_ref.shape[-1]
    h = x_ref[...] * (1.0 + _mod_slice(mod_ref, 1, d)) + _mod_slice(mod_ref, 0, d)
    y = jnp.dot(h.astype(BF16), w_ref[...], preferred_element_type=F32)
    gate_ref[...] = y[:, :r]
    xb_ref[...] = y[:, r:]


def _inproj_call(x, mod, w_in, rows):
    b, l, d = x.shape
    r = w_in.shape[1] // 2
    tm = min(rows, l)
    return pl.pallas_call(
        _inproj_kernel,
        grid=(b, l // tm),
        in_specs=[
            pl.BlockSpec((None, tm, d), lambda i, j: (i, j, 0)),
            pl.BlockSpec((None, 1, 6 * d), lambda i, j: (i, 0, 0)),
            pl.BlockSpec(w_in.shape, lambda i, j: (0, 0)),
        ],
        out_specs=[
            pl.BlockSpec((None, tm, r), lambda i, j: (i, j, 0)),
            pl.BlockSpec((tm, r), lambda i, j: (j, i)),
        ],
        out_shape=[jax.ShapeDtypeStruct((b, l, r), F32), jax.ShapeDtypeStruct((l, b * r), F32)],
        compiler_params=_cparams(("arbitrary", "arbitrary"), VMEM_LIMIT_BYTES),
        name="lru_in_proj",
    )(x, mod, w_in)


def _lru_coeffs(main_ref, prev_ref, next_ref, first, last, cw_ref, cb_ref, wd_ref, br_ref, bi_ref,
                lam_ref, a_scr, b_scr):
    tl, nb, r = main_ref.shape
    prev = jnp.where(first, 0.0, prev_ref[...])
    nxt = jnp.where(last, 0.0, next_ref[...])
    ext = jnp.concatenate([prev, main_ref[...], nxt], axis=0)
    u = cb_ref[...][None]
    for k in range(4):
        u = u + cw_ref[k:k + 1, :][None] * ext[k:k + tl]
    u = u.reshape(tl * nb, r)
    ub = u.astype(BF16)
    lam = lam_ref[...]
    softplus_neg = jnp.maximum(-lam, 0.0) + jnp.log1p(jnp.exp(-jnp.abs(lam)))
    pw = 2 * LRU_BLOCK_DIM
    for p in range(LRU_BLOCKS // 2):
        cs = slice(p * pw, (p + 1) * pw)
        z = jnp.dot(ub[:, cs], wd_ref[p], preferred_element_type=F32)
        rg = 0.5 + 0.5 * jnp.tanh(0.5 * (z[:, :pw] + br_ref[:, cs]))
        ig = 0.5 + 0.5 * jnp.tanh(0.5 * (z[:, pw:] + bi_ref[:, cs]))
        log_a = -LRU_C * rg * softplus_neg[:, cs]
        a = jnp.exp(log_a)
        gain = jnp.sqrt(-jnp.tanh(log_a) * (a * a + 1.0))
        a_scr[:, :, cs] = a.reshape(tl, nb, pw)
        b_scr[:, :, cs] = (gain * (ig * u[:, cs])).reshape(tl, nb, pw)


def _scan_kernel(fm_ref, fp_ref, fn_ref, bm_ref, bp_ref, bn_ref, h0f_ref, h0b_ref,
                 cw_ref, cb_ref, wdf_ref, wdb_ref, br_ref, bi_ref, lam_ref,
                 hf_ref, hb_ref, lastf_ref, lastb_ref,
                 af_scr, bf_scr, ab_scr, bb_scr, sf_scr, sb_scr):
    c, nc = pl.program_id(0), pl.num_programs(0)
    tl = fm_ref.shape[0]

    @pl.when(c == 0)
    def _():
        sf_scr[...] = h0f_ref[...]
        sb_scr[...] = h0b_ref[...]

    _lru_coeffs(fm_ref, fp_ref, fn_ref, c == 0, c == nc - 1, cw_ref, cb_ref, wdf_ref,
                br_ref.at[0:1], bi_ref.at[0:1], lam_ref.at[0:1], af_scr, bf_scr)
    _lru_coeffs(bm_ref, bp_ref, bn_ref, c == nc - 1, c == 0, cw_ref, cb_ref, wdb_ref,
                br_ref.at[1:2], bi_ref.at[1:2], lam_ref.at[1:2], ab_scr, bb_scr)

    def step(t, carry):
        hf, hb = carry
        hf = af_scr[t] * hf + bf_scr[t]
        hf_ref[t] = hf
        tb = tl - 1 - t
        hb = ab_scr[tb] * hb + bb_scr[tb]
        hb_ref[tb] = hb
        return hf, hb

    hf, hb = lax.fori_loop(0, tl, step, (sf_scr[...], sb_scr[...]))
    sf_scr[...] = hf
    sb_scr[...] = hb

    @pl.when(c == nc - 1)
    def _():
        lastf_ref[...] = hf
        lastb_ref[...] = hb


def _scan_call(xb, h0f, h0b, conv_w, conv_b, wd, b_r, b_i, lam, steps):
    ls, nb, r = xb.shape
    tl = min(steps, ls)
    nc = ls // tl
    fwd, bwd = (lambda c: c), (lambda c: nc - 1 - c)
    main = lambda ch: pl.BlockSpec((tl, nb, r), lambda c: (ch(c), 0, 0))
    prev2 = lambda ch: pl.BlockSpec((2, nb, r), lambda c: (jnp.maximum(ch(c) * (tl // 2) - 1, 0), 0, 0))
    next1 = lambda ch: pl.BlockSpec((1, nb, r), lambda c: (jnp.minimum((ch(c) + 1) * tl, ls - 1), 0, 0))
    full = lambda a: pl.BlockSpec(a.shape, lambda c: (0,) * a.ndim)
    state = pl.BlockSpec((nb, r), lambda c: (0, 0))
    wd_spec = lambda dirn: pl.BlockSpec((None,) + wd.shape[1:], lambda c: (dirn, 0, 0, 0))
    return pl.pallas_call(
        _scan_kernel,
        grid=(nc,),
        in_specs=[main(fwd), prev2(fwd), next1(fwd), main(bwd), prev2(bwd), next1(bwd),
                  state, state, full(conv_w), full(conv_b), wd_spec(0), wd_spec(1),
                  full(b_r), full(b_i), full(lam)],
        out_specs=[main(fwd), main(bwd), state, state],
        out_shape=[jax.ShapeDtypeStruct((ls, nb, r), F32), jax.ShapeDtypeStruct((ls, nb, r), F32),
                   jax.ShapeDtypeStruct((nb, r), F32), jax.ShapeDtypeStruct((nb, r), F32)],
        scratch_shapes=[pltpu.VMEM((tl, nb, r), F32)] * 4 + [pltpu.VMEM((nb, r), F32)] * 2,
        compiler_params=_cparams(("arbitrary",), VMEM_LIMIT_BYTES),
        name="rglru_scan",
    )(xb, xb, xb, xb, xb, xb, h0f, h0b, conv_w, conv_b, wd, wd, b_r, b_i, lam)


def _paired_block_diag(w_r, w_i):
    def pair(w):
        ndir = w.shape[0]
        w = w.reshape(ndir, LRU_BLOCKS // 2, 2, LRU_BLOCK_DIM, LRU_BLOCK_DIM)
        z = jnp.zeros_like(w[:, :, 0])
        top = jnp.concatenate([w[:, :, 0], z], axis=-1)
        bot = jnp.concatenate([z, w[:, :, 1]], axis=-1)
        return jnp.concatenate([top, bot], axis=-2)
    return jnp.concatenate([pair(w_r), pair(w_i)], axis=-1).astype(BF16)


def _outproj_kernel(gate_ref, hf_ref, hb_ref, w_ref, x_ref, mod_ref, g_ref, b_ref, out_ref):
    d = x_ref.shape[-1]
    gt = gate_ref[...]
    gelu = 0.5 * gt * (1.0 + jnp.tanh(0.7978845608028654 * (gt + 0.044715 * (gt * gt * gt))))
    y = gelu * (hf_ref[...] + hb_ref[...])
    proj = jnp.dot(y.astype(BF16), w_ref[...], preferred_element_type=F32)
    out_ref[...] = _layer_norm(ALPHA * x_ref[...] + _mod_slice(mod_ref, 2, d) * proj,
                               g_ref[...], b_ref[...])


def _outproj_call(gate, hf, hb, w_out, x, mod, ln_g, ln_b, rows):
    b, l, d = x.shape
    r = gate.shape[-1]
    tm = min(rows, l)
    tmajor = pl.BlockSpec((tm, r), lambda i, j: (j, i))
    return pl.pallas_call(
        _outproj_kernel,
        grid=(b, l // tm),
        in_specs=[
            pl.BlockSpec((None, tm, r), lambda i, j: (i, j, 0)),
            tmajor, tmajor,
            pl.BlockSpec(w_out.shape, lambda i, j: (0, 0)),
            pl.BlockSpec((None, tm, d), lambda i, j: (i, j, 0)),
            pl.BlockSpec((None, 1, 6 * d), lambda i, j: (i, 0, 0)),
            pl.BlockSpec((1, d), lambda i, j: (0, 0)),
            pl.BlockSpec((1, d), lambda i, j: (0, 0)),
        ],
        out_specs=pl.BlockSpec((None, tm, d), lambda i, j: (i, j, 0)),
        out_shape=jax.ShapeDtypeStruct((b, l, d), F32),
        compiler_params=_cparams(("arbitrary", "arbitrary"), VMEM_LIMIT_BYTES),
        name="lru_out_proj",
    )(gate, hf, hb, w_out, x, mod, ln_g, ln_b)


ROUTE_IDX_LANE = N_EXPERTS
ROUTE_W_LANE = N_EXPERTS + TOP_K
SEG_ALIGN = SUBLANES
MAX_SEG_PAD = N_EXPERTS * (SEG_ALIGN - 1)
BF16_ROWS = 2 * SUBLANES


def _window_rows(window):
    return -(-(TOP_K * window + MAX_SEG_PAD) // BF16_ROWS) * BF16_ROWS


def _router_kernel(x_ref, mod_ref, wr_ref, h_ref, info_ref):
    d = x_ref.shape[-1]
    h = x_ref[...] * (1.0 + _mod_slice(mod_ref, 4, d)) + _mod_slice(mod_ref, 3, d)
    h_ref[...] = h.astype(BF16)
    logits = jnp.dot(h, wr_ref[...], preferred_element_type=F32, precision=lax.Precision.HIGHEST)
    lane = lax.broadcasted_iota(jnp.int32, logits.shape, 1)
    logits = jnp.where(lane < N_EXPERTS, logits, -jnp.inf)
    m1 = jnp.max(logits, axis=-1, keepdims=True)
    i1 = jnp.min(jnp.where(logits == m1, lane, LANES), axis=-1, keepdims=True)
    rest = jnp.where(lane == i1, -jnp.inf, logits)
    m2 = jnp.max(rest, axis=-1, keepdims=True)
    i2 = jnp.min(jnp.where(rest == m2, lane, LANES), axis=-1, keepdims=True)
    e2 = jnp.exp(m2 - m1)
    w1 = 1.0 / (1.0 + e2)
    w2 = e2 / (1.0 + e2)
    info = jnp.where(lane == i1, w1, jnp.where(lane == i2, w2, 0.0))
    info = jnp.where(lane == ROUTE_IDX_LANE, i1.astype(F32), info)
    info = jnp.where(lane == ROUTE_IDX_LANE + 1, i2.astype(F32), info)
    info = jnp.where(lane == ROUTE_W_LANE, w1, info)
    info = jnp.where(lane == ROUTE_W_LANE + 1, w2, info)
    info_ref[...] = info


def _router_call(x, mod, router_padded, rows):
    b, l, d = x.shape
    tm = min(rows, l)
    nt = l // tm
    return pl.pallas_call(
        _router_kernel,
        grid=(b, nt),
        in_specs=[
            pl.BlockSpec((None, tm, d), lambda i, j: (i, j, 0)),
            pl.BlockSpec((None, 1, 6 * d), lambda i, j: (i, 0, 0)),
            pl.BlockSpec(router_padded.shape, lambda i, j: (0, 0)),
        ],
        out_specs=[
            pl.BlockSpec((tm, d), lambda i, j: (i * nt + j, 0)),
            pl.BlockSpec((tm, LANES), lambda i, j: (i * nt + j, 0)),
        ],
        out_shape=[jax.ShapeDtypeStruct((b * l, d), BF16), jax.ShapeDtypeStruct((b * l, LANES), F32)],
        compiler_params=_cparams(("arbitrary", "arbitrary"), VMEM_LIMIT_BYTES),
        name="moe_router",
    )(x, mod, router_padded)


def _for_each_piece(seg_refs, w, window, fn):
    loff_ref, goff_ref, cpad_ref = seg_refs
    for e in range(N_EXPERTS):
        lo, go, c = (ref[w * N_EXPERTS + e] for ref in (loff_ref, goff_ref, cpad_ref))
        size = window
        while size >= SEG_ALIGN:
            done = c & ~(2 * size - 1)

            @pl.when((c & size) != 0)
            def _(lo=lo, go=go, done=done, size=size):
                fn(pl.multiple_of(lo + done, SEG_ALIGN), pl.multiple_of(go + done, SEG_ALIGN), size)

            size //= 2


def _dispatch_kernel(loff_ref, goff_ref, cpad_ref, dloc_ref, wts_ref, h_ref, zeros_hbm, xs_hbm,
                     buf, sems):
    del zeros_hbm
    w, nw = pl.program_id(0), pl.num_programs(0)
    slot = w % 2
    window, d = h_ref.shape
    r = buf.shape[1]
    segs = (loff_ref, goff_ref, cpad_ref)

    def copy(buf_slot, lrow, grow, size):
        return pltpu.make_async_copy(buf.at[buf_slot, pl.ds(lrow, size)],
                                     xs_hbm.at[pl.ds(grow, size)], sems.at[buf_slot])

    row = lax.broadcasted_iota(jnp.int32, (r, window), 0)
    hit0 = row == dloc_ref[0:1, :]
    hit1 = row == dloc_ref[1:2, :]
    perm = jnp.where(hit0 | hit1, 1.0, 0.0).astype(BF16)
    buf[slot, :, :d] = jnp.dot(perm, h_ref[...], preferred_element_type=F32)
    roww = jnp.sum(jnp.where(hit0, wts_ref[0:1, :], 0.0) + jnp.where(hit1, wts_ref[1:2, :], 0.0),
                   axis=1, keepdims=True)
    buf[slot, :, d:] = jnp.broadcast_to(roww, (r, LANES))

    _for_each_piece(segs, w, window, lambda l, g, s: copy(slot, l, g, s).start())

    @pl.when(w > 0)
    def _():
        _for_each_piece(segs, w - 1, window, lambda l, g, s: copy(1 - slot, l, g, s).wait())

    @pl.when(w == nw - 1)
    def _():
        _for_each_piece(segs, w, window, lambda l, g, s: copy(slot, l, g, s).wait())


def _dispatch_call(plan, h, window, n_rows):
    n, d = h.shape
    nw = n // window
    r = _window_rows(window)
    width = d + LANES
    grid_spec = pltpu.PrefetchScalarGridSpec(
        num_scalar_prefetch=3,
        grid=(nw,),
        in_specs=[
            pl.BlockSpec((None, TOP_K, window), lambda w, *_: (w, 0, 0)),
            pl.BlockSpec((None, TOP_K, window), lambda w, *_: (w, 0, 0)),
            pl.BlockSpec((window, d), lambda w, *_: (w, 0)),
            pl.BlockSpec(memory_space=pl.ANY),
        ],
        out_specs=pl.BlockSpec(memory_space=pl.ANY),
        scratch_shapes=[pltpu.VMEM((2, r, width), F32), pltpu.SemaphoreType.DMA((2,))],
    )
    return pl.pallas_call(
        _dispatch_kernel,
        grid_spec=grid_spec,
        out_shape=jax.ShapeDtypeStruct((n_rows, width), F32),
        input_output_aliases={6: 0},
        compiler_params=_cparams(("arbitrary",), VMEM_LIMIT_BYTES),
        name="expert_dispatch",
    )(plan["loff"], plan["goff"], plan["cpad"], plan["dloc_rows"], plan["wts_rows"], h,
      jnp.zeros((n_rows, width), F32))


def _expert_kernel(te_ref, nv_ref, xs_ref, wg_ref, wu_ref, wd_ref, y_ref, hb_scr, acc_scr):
    i, f, nf = pl.program_id(0), pl.program_id(1), pl.num_programs(1)
    d = y_ref.shape[-1]

    @pl.when(f == 0)
    def _():
        hb_scr[...] = xs_ref[:, :d].astype(BF16)
        acc_scr[...] = jnp.zeros_like(acc_scr)

    @pl.when(i < nv_ref[0])
    def _():
        acc_scr[...] += _swiglu_chunk(hb_scr[...], wg_ref, wu_ref, wd_ref)

    @pl.when(f == nf - 1)
    def _():
        y_ref[...] = acc_scr[...] * xs_ref[:, d:d + 1]


def _expert_call(plan, xs, w_gu, w_dn, rows, cols):
    n_rows, width = xs.shape
    d = width - LANES
    nt = n_rows // rows
    ff = w_dn.shape[1]
    fc = cols
    nf = ff // fc
    grid_spec = pltpu.PrefetchScalarGridSpec(
        num_scalar_prefetch=2,
        grid=(nt, nf),
        in_specs=[
            pl.BlockSpec((rows, width), lambda i, f, te, nv: (i, 0)),
            pl.BlockSpec((None, d, fc), lambda i, f, te, nv: (te[i], 0, f)),
            pl.BlockSpec((None, d, fc), lambda i, f, te, nv: (te[i], 0, nf + f)),
            pl.BlockSpec((None, fc, d), lambda i, f, te, nv: (te[i], f, 0)),
        ],
        out_specs=pl.BlockSpec((rows, d), lambda i, f, te, nv: (i, 0)),
        scratch_shapes=[pltpu.VMEM((rows, d), BF16), pltpu.VMEM((rows, d), F32)],
    )
    return pl.pallas_call(
        _expert_kernel,
        grid_spec=grid_spec,
        out_shape=jax.ShapeDtypeStruct((n_rows, d), F32),
        compiler_params=_cparams(("arbitrary", "arbitrary"), VMEM_LIMIT_BYTES),
        name="expert_swiglu",
    )(plan["tile_expert"], plan["n_valid"], xs, w_gu, w_gu, w_dn)


def _combine_kernel(loff_ref, goff_ref, cpad_ref, dcol_ref, y_hbm, x_ref, mod_ref, g_ref, b_ref,
                    out_ref, ybuf, sems):
    w, nw = pl.program_id(0), pl.num_programs(0)
    slot = w % 2
    window, d = x_ref.shape
    r = ybuf.shape[1]
    segs = (loff_ref, goff_ref, cpad_ref)

    def copy(buf_slot, lrow, grow, size):
        return pltpu.make_async_copy(y_hbm.at[pl.ds(grow, size)],
                                     ybuf.at[buf_slot, pl.ds(lrow, size)], sems.at[buf_slot])

    @pl.when(w == 0)
    def _():
        ybuf[...] = jnp.zeros_like(ybuf)
        _for_each_piece(segs, 0, window, lambda l, g, s: copy(0, l, g, s).start())

    @pl.when(w + 1 < nw)
    def _():
        _for_each_piece(segs, w + 1, window, lambda l, g, s: copy(1 - slot, l, g, s).start())

    _for_each_piece(segs, w, window, lambda l, g, s: copy(slot, l, g, s).wait())

    lane = lax.broadcasted_iota(jnp.int32, (window, r), 1)
    hit = (lane == dcol_ref[:, 0:1]) | (lane == dcol_ref[:, 1:2])
    perm = jnp.where(hit, 1.0, 0.0).astype(BF16)
    y = ybuf[slot]
    hi = y.astype(BF16)
    lo = (y - hi.astype(F32)).astype(BF16)
    f = (jnp.dot(perm, hi, preferred_element_type=F32)
         + jnp.dot(perm, lo, preferred_element_type=F32))
    out_ref[...] = _layer_norm(ALPHA * x_ref[...] + _mod_slice(mod_ref, 5, d) * f,
                               g_ref[...], b_ref[...])


def _combine_call(plan, y, x, mod, ln_g, ln_b, window):
    b, l, d = x.shape
    wpb = l // window
    r = _window_rows(window)
    grid_spec = pltpu.PrefetchScalarGridSpec(
        num_scalar_prefetch=3,
        grid=(b * wpb,),
        in_specs=[
            pl.BlockSpec((window, TOP_K), lambda w, *_: (w, 0)),
            pl.BlockSpec(memory_space=pl.ANY),
            pl.BlockSpec((None, window, d), lambda w, *_: (w // wpb, w % wpb, 0)),
            pl.BlockSpec((None, 1, 6 * d), lambda w, *_: (w // wpb, 0, 0)),
            pl.BlockSpec((1, d), lambda w, *_: (0, 0)),
            pl.BlockSpec((1, d), lambda w, *_: (0, 0)),
        ],
        out_specs=pl.BlockSpec((None, window, d), lambda w, *_: (w // wpb, w % wpb, 0)),
        scratch_shapes=[pltpu.VMEM((2, r, d), F32), pltpu.SemaphoreType.DMA((2,))],
    )
    return pl.pallas_call(
        _combine_kernel,
        grid_spec=grid_spec,
        out_shape=jax.ShapeDtypeStruct((b, l, d), F32),
        compiler_params=_cparams(("arbitrary",), VMEM_LIMIT_BYTES),
        name="expert_combine",
    )(plan["loff"], plan["goff"], plan["cpad"], plan["dloc_cols"], y, x, mod, ln_g, ln_b)


def _dispatch_plan(info, window, rows):
    n = info.shape[0]
    nw = n // window
    experts = info[:, ROUTE_IDX_LANE:ROUTE_IDX_LANE + TOP_K].astype(jnp.int32)
    experts = experts.reshape(nw, window, TOP_K)
    is_e = experts[..., None] == jnp.arange(N_EXPERTS)
    chosen = jnp.any(is_e, axis=2).astype(jnp.int32)
    rank = jnp.cumsum(chosen, axis=1) - chosen
    count = jnp.sum(chosen, axis=1)
    cpad = -(-count // SEG_ALIGN) * SEG_ALIGN
    loff = jnp.cumsum(cpad, axis=1) - cpad
    group = -(-jnp.sum(cpad, axis=0) // rows) * rows
    gend = jnp.cumsum(group)
    goff = (gend - group)[None, :] + jnp.cumsum(cpad, axis=0) - cpad
    dloc = jnp.sum(jnp.where(is_e, (loff[:, None, :] + rank)[:, :, None, :], 0), axis=-1)
    n_tiles = -(-(TOP_K * n + nw * MAX_SEG_PAD) // rows) + N_EXPERTS
    tile_start = jnp.arange(n_tiles) * rows
    tile_expert = jnp.minimum(jnp.sum(tile_start[:, None] >= gend[None, :], axis=1), N_EXPERTS - 1)
    wts = info[:, ROUTE_W_LANE:ROUTE_W_LANE + TOP_K].reshape(nw, window, TOP_K)
    flat = lambda a: a.reshape(-1).astype(jnp.int32)
    return dict(
        loff=flat(loff), goff=flat(goff), cpad=flat(cpad),
        dloc_rows=jnp.transpose(dloc, (0, 2, 1)).astype(jnp.int32),
        dloc_cols=dloc.reshape(n, TOP_K).astype(jnp.int32),
        wts_rows=jnp.transpose(wts, (0, 2, 1)),
        tile_expert=tile_expert.astype(jnp.int32),
        n_valid=(gend[-1] // rows).reshape(1).astype(jnp.int32),
        n_rows=n_tiles * rows,
    )


def _attention_layer(x, ctx, mod_x, mod_c, ln_g, ln_b, w_qkv, w_o, sink, w_gu, w_dn, grid_w, ts):
    b, l, d = x.shape
    lc = ctx.shape[1]
    g0, b0, g1, b1 = ln_g[0:1], ln_b[0:1], ln_g[1:2], ln_b[1:2]
    w_qkv, w_o, w_gu, w_dn = (w.astype(BF16) for w in (w_qkv, w_o, w_gu, w_dn))
    ctx_flat = ctx.reshape(1, b * lc, d)
    n_kv = N_KV_HEADS * HEAD_DIM

    q, k, v = _qkv_call(x, mod_x, w_qkv, _rope_tables(l, grid_w), ts["qkv_rows"])
    qc, kc, vc = _qkv_call(ctx_flat, mod_c, w_qkv, _no_rope_tables(b * lc), ts["qkv_rows"])
    qc = qc.reshape(b, lc, -1)
    kc, vc = kc.reshape(b, lc, n_kv), vc.reshape(b, lc, n_kv)
    mod_cb = jnp.broadcast_to(mod_c, (b,) + mod_c.shape[1:])

    x = _attn_call(q, k, v, kc, vc, sink, w_o, x, mod_x, g0, b0, band=True)
    ctx = _attn_call(qc, None, None, kc, vc, sink, w_o, ctx, mod_cb, g0, b0, band=False)

    x = _ffn_call(x, mod_x, w_gu, w_dn, g1, b1, ts["ffn_rows"], ts["ffn_cols"])
    ctx = _ffn_call(ctx.reshape(1, b * lc, d), mod_c, w_gu, w_dn, g1, b1,
                    ts["ffn_rows"], ts["ffn_cols"]).reshape(b, lc, d)
    return x, ctx


def _recurrent_layer(x, ctx, mod_x, mod_c, ln_g, ln_b, w_in, conv_w, conv_b, lam, w_r, b_r, w_i,
                     b_i, w_out, router, w_gu, w_dn, ts):
    b, l, d = x.shape
    g0, b0, g1, b1 = ln_g[0:1], ln_b[0:1], ln_g[1:2], ln_b[1:2]
    w_in, w_out, w_gu, w_dn = (w.astype(BF16) for w in (w_in, w_out, w_gu, w_dn))
    wd = _paired_block_diag(w_r, w_i)
    conv_b = conv_b.reshape(1, -1)
    mod_cb = jnp.broadcast_to(mod_c, (b,) + mod_c.shape[1:])

    r = w_out.shape[0]
    gate, xb = _inproj_call(x, mod_x, w_in, ts["proj_rows"])
    _, xb_c = _inproj_call(ctx, mod_cb, w_in, ts["proj_rows"])
    zero = jnp.zeros((b, r), F32)
    scan = functools.partial(_scan_call, conv_w=conv_w, conv_b=conv_b, wd=wd, b_r=b_r, b_i=b_i,
                             lam=lam, steps=ts["scan_steps"])
    _, _, cf, cb = scan(xb_c.reshape(-1, b, r), zero, zero)
    hf, hb, _, _ = scan(xb.reshape(l, b, r), cf, cb)
    x = _outproj_call(gate, hf.reshape(l, b * r), hb.reshape(l, b * r), w_out, x, mod_x, g0, b0,
                      ts["proj_rows"])

    router_padded = jnp.pad(router, ((0, 0), (0, LANES - N_EXPERTS)))
    h, info = _router_call(x, mod_x, router_padded, ts["route_rows"])
    plan = _dispatch_plan(info, ts["moe_window"], ts["moe_rows"])
    xs = _dispatch_call(plan, h, ts["moe_window"], plan["n_rows"])
    y = _expert_call(plan, xs, w_gu, w_dn, ts["moe_rows"], ts["ffn_cols"])
    return _combine_call(plan, y, x, mod_x, g1, b1, ts["moe_window"])


def kernel(x, c, ctx, c_ctx, w_mod, b_mod, ln_g, ln_b, attn_w_qkv, attn_w_o, attn_sink, ffn_w_gu,
           ffn_w_dn, lru_w_in, lru_conv_w, lru_conv_b, lru_lambda, lru_w_r, lru_b_r, lru_w_i,
           lru_b_i, lru_w_out, moe_router, moe_w_gu, moe_w_dn):
    b, l, d = x.shape
    assert w_mod.shape[0] == DEPTH and l % BLOCK == 0 and ctx.shape[1] % BLOCK == 0
    grid_w = 64
    ts = _tile_sizes()

    rows = -(-(b + 1) // SUBLANES) * SUBLANES
    cond = jnp.zeros((rows, d), F32).at[:b].set(c).at[b].set(c_ctx)
    mod = _mod_call(cond, w_mod, b_mod[:, None, :])
    mod_x = lambda i: mod[i, :b, None, :]
    mod_c = lambda i: mod[i, b:b + 1, None, :]

    x, ctx = _attention_layer(x, ctx, mod_x(0), mod_c(0), ln_g[0], ln_b[0], attn_w_qkv[0],
                              attn_w_o[0], attn_sink[0], ffn_w_gu[0], ffn_w_dn[0], grid_w, ts)
    return _recurrent_layer(x, ctx, mod_x(1), mod_c(1), ln_g[1], ln_b[1], lru_w_in[0],
                            lru_conv_w[0], lru_conv_b[0], lru_lambda[0], lru_w_r[0], lru_b_r[0],
                            lru_w_i[0], lru_b_i[0], lru_w_out[0], moe_router[0], moe_w_gu[0],
                            moe_w_dn[0], ts)
```

```python
import functools

import jax
import jax.numpy as jnp
from jax import lax
from jax.experimental import pallas as pl
from jax.experimental.pallas import tpu as pltpu

N_HEADS = 8
N_KV_HEADS = 2
GROUP = N_HEADS // N_KV_HEADS
HEAD_DIM = 128
BLOCK = 128
AXIS_DIM = HEAD_DIM // 2
ROPE_BASE = 10000.0
LRU_BLOCKS = 8
LRU_BLOCK_DIM = 128
LRU_C = 8.0
N_EXPERTS = 8
TOP_K = 2
DEPTH = 2
ALPHA = (2 * DEPTH) ** 0.25
LN_EPS = 1e-5
NEG_INF = -1e30

LANES = 128
SUBLANES = 8
VMEM_LIMIT_BYTES = 56 * 1024 * 1024

F32 = jnp.float32
BF16 = jnp.bfloat16


def _tile_sizes():
    return dict(
        qkv_rows=512,
        ffn_rows=1024,
        ffn_cols=512,
        proj_rows=512,
        scan_steps=64,
        moe_window=512,
        moe_rows=1024,
    )


def _cparams(sem, vmem=None):
    return pltpu.CompilerParams(dimension_semantics=sem, vmem_limit_bytes=vmem)


def _layer_norm(h, g, b):
    mu = jnp.mean(h, axis=-1, keepdims=True)
    d = h - mu
    var = jnp.mean(d * d, axis=-1, keepdims=True)
    return d * lax.rsqrt(var + LN_EPS) * g + b


def _mod_slice(mod_ref, k, d):
    return mod_ref[:, k * d:(k + 1) * d]


def _mod_kernel(c_ref, w_ref, b_ref, o_ref):
    c = c_ref[...]
    s = c * jax.nn.sigmoid(c)
    o_ref[...] = jnp.dot(s, w_ref[...], preferred_element_type=F32,
                         precision=lax.Precision.HIGHEST) + b_ref[...]


def _mod_call(cond, w_mod, b_mod):
    rows, d = cond.shape
    depth, _, n = w_mod.shape
    tn = 1536
    return pl.pallas_call(
        _mod_kernel,
        grid=(depth, n // tn),
        in_specs=[
            pl.BlockSpec((rows, d), lambda i, j: (0, 0)),
            pl.BlockSpec((None, d, tn), lambda i, j: (i, 0, j)),
            pl.BlockSpec((None, 1, tn), lambda i, j: (i, 0, j)),
        ],
        out_specs=pl.BlockSpec((None, rows, tn), lambda i, j: (i, 0, j)),
        out_shape=jax.ShapeDtypeStruct((depth, rows, n), F32),
        compiler_params=_cparams(("arbitrary", "arbitrary")),
        name="mod_vectors",
    )(cond, w_mod, b_mod)


def _qkv_kernel(x_ref, mod_ref, w_ref, cos_ref, sa_ref, sb_ref, q_ref, k_ref, v_ref):
    d = x_ref.shape[-1]
    h = x_ref[...] * (1.0 + _mod_slice(mod_ref, 1, d)) + _mod_slice(mod_ref, 0, d)
    qkv = jnp.dot(h.astype(BF16), w_ref[...], preferred_element_type=F32)
    cos, sa, sb = cos_ref[...], sa_ref[...], sb_ref[...]
    scale = HEAD_DIM ** -0.5
    for head in range(N_HEADS + N_KV_HEADS):
        u = qkv[:, head * HEAD_DIM:(head + 1) * HEAD_DIM]
        r = (u * cos + pltpu.roll(u, HEAD_DIM - AXIS_DIM // 2, axis=1) * sa
             + pltpu.roll(u, AXIS_DIM // 2, axis=1) * sb)
        if head < N_HEADS:
            q_ref[:, head * HEAD_DIM:(head + 1) * HEAD_DIM] = (r * scale).astype(BF16)
        else:
            kh = head - N_HEADS
            k_ref[:, kh * HEAD_DIM:(kh + 1) * HEAD_DIM] = r.astype(BF16)
    kv = N_KV_HEADS * HEAD_DIM
    v_ref[...] = qkv[:, N_HEADS * HEAD_DIM + kv:].astype(BF16)


def _qkv_call(x, mod, w_qkv, tabs, rows):
    b, l, d = x.shape
    n_q, n_kv = N_HEADS * HEAD_DIM, N_KV_HEADS * HEAD_DIM
    tm = min(rows, l)
    tab_spec = pl.BlockSpec((tm, HEAD_DIM), lambda i, j: (j, 0))
    return pl.pallas_call(
        _qkv_kernel,
        grid=(b, l // tm),
        in_specs=[
            pl.BlockSpec((None, tm, d), lambda i, j: (i, j, 0)),
            pl.BlockSpec((None, 1, 6 * d), lambda i, j: (i, 0, 0)),
            pl.BlockSpec(w_qkv.shape, lambda i, j: (0, 0)),
            tab_spec, tab_spec, tab_spec,
        ],
        out_specs=[
            pl.BlockSpec((None, tm, n_q), lambda i, j: (i, j, 0)),
            pl.BlockSpec((None, tm, n_kv), lambda i, j: (i, j, 0)),
            pl.BlockSpec((None, tm, n_kv), lambda i, j: (i, j, 0)),
        ],
        out_shape=[
            jax.ShapeDtypeStruct((b, l, n_q), BF16),
            jax.ShapeDtypeStruct((b, l, n_kv), BF16),
            jax.ShapeDtypeStruct((b, l, n_kv), BF16),
        ],
        compiler_params=_cparams(("arbitrary", "arbitrary"), VMEM_LIMIT_BYTES),
        name="qkv_rope",
    )(x, mod, w_qkv, *tabs)


def _rope_tables(l, grid_w):
    t = jnp.arange(l)
    pos = jnp.stack([(t // grid_w).astype(F32), (t % grid_w).astype(F32)], axis=1)
    freqs = ROPE_BASE ** (-jnp.arange(0, AXIS_DIM, 2, dtype=F32) / AXIS_DIM)
    lane = jnp.arange(HEAD_DIM)
    ang = pos[:, lane // AXIS_DIM] * freqs[lane % (AXIS_DIM // 2)][None, :]
    first = (lane % AXIS_DIM) < AXIS_DIM // 2
    cos, sin = jnp.cos(ang), jnp.sin(ang)
    return cos, jnp.where(first, -sin, 0.0), jnp.where(first, 0.0, sin)


def _no_rope_tables(l):
    z = jnp.zeros((l, HEAD_DIM), F32)
    return jnp.ones((l, HEAD_DIM), F32), z, z


def _nt_dot(a, b):
    return lax.dot_general(a, b, (((1,), (1,)), ((), ())), preferred_element_type=F32)


def _attend(qs, segs, sink_col):
    k_all = jnp.concatenate([k for k, _, _ in segs], axis=0)
    v_all = jnp.concatenate([v for _, v, _ in segs], axis=0)
    v_ext = jnp.concatenate([v_all, jnp.ones_like(v_all)], axis=1)
    s_all = _nt_dot(qs, k_all)
    tiles, col = [], 0
    for k, _, mask in segs:
        for c in range(col, col + k.shape[0], LANES):
            s = s_all[:, c:c + LANES]
            tiles.append(s if mask is None else jnp.where(mask, s, NEG_INF))
        col += k.shape[0]
    m_tile = tiles[0]
    for s in tiles[1:]:
        m_tile = jnp.maximum(m_tile, s)
    m = jnp.maximum(sink_col, jnp.max(m_tile, axis=-1, keepdims=True))
    p_all = jnp.concatenate([jnp.exp(s - m).astype(BF16) for s in tiles], axis=1)
    res = jnp.dot(p_all, v_ext, preferred_element_type=F32)
    hd = v_all.shape[1]
    return res[:, :hd] / (res[:, hd:hd + 1] + jnp.exp(sink_col - m))


def _attn_epilogue(o_scr, wo_ref, x_ref, mod_ref, g_ref, b_ref, out_ref):
    d = x_ref.shape[-1]
    proj = jnp.dot(o_scr[...], wo_ref[...], preferred_element_type=F32)
    out_ref[...] = _layer_norm(ALPHA * x_ref[...] + _mod_slice(mod_ref, 2, d) * proj,
                               g_ref[...], b_ref[...])


def _sink_column(sink_ref, kvh):
    return jnp.concatenate(
        [jnp.full((BLOCK, 1), sink_ref[kvh * GROUP + g], F32) for g in range(GROUP)], axis=0)


def _stack_group(q_ref, kvh):
    return jnp.concatenate(
        [q_ref[:, (kvh * GROUP + g) * HEAD_DIM:(kvh * GROUP + g + 1) * HEAD_DIM]
         for g in range(GROUP)], axis=0)


def _unstack_group(o, o_scr, kvh):
    for g in range(GROUP):
        col = (kvh * GROUP + g) * HEAD_DIM
        o_scr[:, col:col + HEAD_DIM] = o[g * BLOCK:(g + 1) * BLOCK].astype(BF16)


def _band_attn_kernel(sink_ref, q_ref, kp_ref, kc_ref, kn_ref, vp_ref, vc_ref, vn_ref,
                      kx_ref, vx_ref, wo_ref, x_ref, mod_ref, g_ref, b_ref, out_ref, o_scr):
    n, nb = pl.program_id(1), pl.num_programs(1)
    row = lax.broadcasted_iota(jnp.int32, (GROUP * BLOCK, BLOCK), 0) % BLOCK
    col = lax.broadcasted_iota(jnp.int32, (GROUP * BLOCK, BLOCK), 1)
    mask_prev = (col >= row) & (n > 0)
    mask_next = (col <= row) & (n < nb - 1)
    for kvh in range(N_KV_HEADS):
        hs = slice(kvh * HEAD_DIM, (kvh + 1) * HEAD_DIM)
        segs = [(kp_ref[:, hs], vp_ref[:, hs], mask_prev),
                (kc_ref[:, hs], vc_ref[:, hs], None),
                (kn_ref[:, hs], vn_ref[:, hs], mask_next),
                (kx_ref[:, hs], vx_ref[:, hs], None)]
        o = _attend(_stack_group(q_ref, kvh), segs, _sink_column(sink_ref, kvh))
        _unstack_group(o, o_scr, kvh)
    _attn_epilogue(o_scr, wo_ref, x_ref, mod_ref, g_ref, b_ref, out_ref)


def _ctx_attn_kernel(sink_ref, q_ref, kx_ref, vx_ref, wo_ref, x_ref, mod_ref, g_ref, b_ref,
                     out_ref, o_scr):
    for kvh in range(N_KV_HEADS):
        hs = slice(kvh * HEAD_DIM, (kvh + 1) * HEAD_DIM)
        o = _attend(_stack_group(q_ref, kvh), [(kx_ref[:, hs], vx_ref[:, hs], None)],
                    _sink_column(sink_ref, kvh))
        _unstack_group(o, o_scr, kvh)
    _attn_epilogue(o_scr, wo_ref, x_ref, mod_ref, g_ref, b_ref, out_ref)


def _attn_call(q, k, v, kx, vx, sink, w_o, x, mod, ln_g, ln_b, band):
    b, l, d = x.shape
    lc = kx.shape[1]
    nb = l // BLOCK
    n_q, n_kv = N_HEADS * HEAD_DIM, N_KV_HEADS * HEAD_DIM
    kv_blk = lambda f: pl.BlockSpec((None, BLOCK, n_kv), f)
    common_in = [
        pl.BlockSpec((None, lc, n_kv), lambda i, j: (i, 0, 0)),
        pl.BlockSpec((None, lc, n_kv), lambda i, j: (i, 0, 0)),
        pl.BlockSpec(w_o.shape, lambda i, j: (0, 0)),
        pl.BlockSpec((None, BLOCK, d), lambda i, j: (i, j, 0)),
        pl.BlockSpec((None, 1, 6 * d), lambda i, j: (i, 0, 0)),
        pl.BlockSpec((1, d), lambda i, j: (0, 0)),
        pl.BlockSpec((1, d), lambda i, j: (0, 0)),
    ]
    head_in = [pl.BlockSpec(memory_space=pltpu.SMEM),
               pl.BlockSpec((None, BLOCK, n_q), lambda i, j: (i, j, 0))]
    if band:
        prev = lambda i, j: (i, jnp.maximum(j - 1, 0), 0)
        cur = lambda i, j: (i, j, 0)
        nxt = lambda i, j: (i, jnp.minimum(j + 1, nb - 1), 0)
        in_specs = head_in + [kv_blk(prev), kv_blk(cur), kv_blk(nxt)] * 2 + common_in
        args = (sink, q, k, k, k, v, v, v, kx, vx, w_o, x, mod, ln_g, ln_b)
        body, name = _band_attn_kernel, "band_attention"
    else:
        in_specs = head_in + common_in
        args = (sink, q, kx, vx, w_o, x, mod, ln_g, ln_b)
        body, name = _ctx_attn_kernel, "context_attention"
    return pl.pallas_call(
        body,
        grid=(b, nb),
        in_specs=in_specs,
        out_specs=pl.BlockSpec((None, BLOCK, d), lambda i, j: (i, j, 0)),
        out_shape=jax.ShapeDtypeStruct((b, l, d), F32),
        scratch_shapes=[pltpu.VMEM((BLOCK, n_q), BF16)],
        compiler_params=_cparams(("arbitrary", "arbitrary"), VMEM_LIMIT_BYTES),
        name=name,
    )(*args)


def _swiglu_chunk(hb, wg, wu, wd):
    g = jnp.dot(hb, wg, preferred_element_type=F32)
    u = jnp.dot(hb, wu, preferred_element_type=F32)
    a = (g * jax.nn.sigmoid(g)) * u
    return jnp.dot(a.astype(BF16), wd, preferred_element_type=F32)


def _ffn_kernel(x_ref, mod_ref, wg_ref, wu_ref, wd_ref, g_ref, b_ref, out_ref, hb_scr, acc_scr):
    f, nf = pl.program_id(2), pl.num_programs(2)
    d = x_ref.shape[-1]

    @pl.when(f == 0)
    def _():
        h = x_ref[...] * (1.0 + _mod_slice(mod_ref, 4, d)) + _mod_slice(mod_ref, 3, d)
        hb_scr[...] = h.astype(BF16)
        acc_scr[...] = jnp.zeros_like(acc_scr)

    acc_scr[...] += _swiglu_chunk(hb_scr[...], wg_ref[...], wu_ref[...], wd_ref[...])

    @pl.when(f == nf - 1)
    def _():
        out_ref[...] = _layer_norm(ALPHA * x_ref[...] + _mod_slice(mod_ref, 5, d) * acc_scr[...],
                                   g_ref[...], b_ref[...])


def _ffn_call(x, mod, w_gu, w_dn, ln_g, ln_b, rows, cols):
    b, l, d = x.shape
    ff = w_dn.shape[0]
    tm, fc = min(rows, l), cols
    nf = ff // fc
    return pl.pallas_call(
        _ffn_kernel,
        grid=(b, l // tm, nf),
        in_specs=[
            pl.BlockSpec((None, tm, d), lambda i, j, f: (i, j, 0)),
            pl.BlockSpec((None, 1, 6 * d), lambda i, j, f: (i, 0, 0)),
            pl.BlockSpec((d, fc), lambda i, j, f: (0, f)),
            pl.BlockSpec((d, fc), lambda i, j, f: (0, nf + f)),
            pl.BlockSpec((fc, d), lambda i, j, f: (f, 0)),
            pl.BlockSpec((1, d), lambda i, j, f: (0, 0)),
            pl.BlockSpec((1, d), lambda i, j, f: (0, 0)),
        ],
        out_specs=pl.BlockSpec((None, tm, d), lambda i, j, f: (i, j, 0)),
        out_shape=jax.ShapeDtypeStruct((b, l, d), F32),
        scratch_shapes=[pltpu.VMEM((tm, d), BF16), pltpu.VMEM((tm, d), F32)],
        compiler_params=_cparams(("arbitrary", "arbitrary", "arbitrary"), VMEM_LIMIT_BYTES),
        name="dense_swiglu",
    )(x, mod, w_gu, w_gu, w_dn, ln_g, ln_b)


def _inproj_kernel(x_ref, mod_ref, w_ref, gate_ref, xb_ref):
    d = x_ref.shape[-1]
    r = xb_ref.shape[-1]
    h = x_ref[...] * (1.0 + _mod_slice(mod_ref, 1, d)) + _mod_slice(mod_ref, 0, d)
    y = jnp.dot(h.astype(BF16), w_ref[...], preferred_element_type=F32)
    gate_ref[...] = y[:, :r]
    xb_ref[...] = y[:, r:]


def _inproj_call(x, mod, w_in, rows):
    b, l, d = x.shape
    r = w_in.shape[1] // 2
    tm = min(rows, l)
    return pl.pallas_call(
        _inproj_kernel,
        grid=(b, l // tm),
        in_specs=[
            pl.BlockSpec((None, tm, d), lambda i, j: (i, j, 0)),
            pl.BlockSpec((None, 1, 6 * d), lambda i, j: (i, 0, 0)),
            pl.BlockSpec(w_in.shape, lambda i, j: (0, 0)),
        ],
        out_specs=[
            pl.BlockSpec((None, tm, r), lambda i, j: (i, j, 0)),
            pl.BlockSpec((tm, r), lambda i, j: (j, i)),
        ],
        out_shape=[jax.ShapeDtypeStruct((b, l, r), F32), jax.ShapeDtypeStruct((l, b * r), F32)],
        compiler_params=_cparams(("arbitrary", "arbitrary"), VMEM_LIMIT_BYTES),
        name="lru_in_proj",
    )(x, mod, w_in)


def _lru_coeffs(main_ref, prev_ref, next_ref, first, last, cw_ref, cb_ref, wd_ref, br_ref, bi_ref,
                lam_ref, a_scr, b_scr):
    tl, nb, r = main_ref.shape
    prev = jnp.where(first, 0.0, prev_ref[...])
    nxt = jnp.where(last, 0.0, next_ref[...])
    ext = jnp.concatenate([prev, main_ref[...], nxt], axis=0)
    u = cb_ref[...][None]
    for k in range(4):
        u = u + cw_ref[k:k + 1, :][None] * ext[k:k + tl]
    u = u.reshape(tl * nb, r)
    ub = u.astype(BF16)
    lam = lam_ref[...]
    softplus_neg = jnp.maximum(-lam, 0.0) + jnp.log1p(jnp.exp(-jnp.abs(lam)))
    pw = 2 * LRU_BLOCK_DIM
    for p in range(LRU_BLOCKS // 2):
        cs = slice(p * pw, (p + 1) * pw)
        z = jnp.dot(ub[:, cs], wd_ref[p], preferred_element_type=F32)
        rg = 0.5 + 0.5 * jnp.tanh(0.5 * (z[:, :pw] + br_ref[:, cs]))
        ig = 0.5 + 0.5 * jnp.tanh(0.5 * (z[:, pw:] + bi_ref[:, cs]))
        log_a = -LRU_C * rg * softplus_neg[:, cs]
        a = jnp.exp(log_a)
        gain = jnp.sqrt(-jnp.tanh(log_a) * (a * a + 1.0))
        a_scr[:, :, cs] = a.reshape(tl, nb, pw)
        b_scr[:, :, cs] = (gain * (ig * u[:, cs])).reshape(tl, nb, pw)


def _scan_kernel(fm_ref, fp_ref, fn_ref, bm_ref, bp_ref, bn_ref, h0f_ref, h0b_ref,
                 cw_ref, cb_ref, wdf_ref, wdb_ref, br_ref, bi_ref, lam_ref,
                 hf_ref, hb_ref, lastf_ref, lastb_ref,
                 af_scr, bf_scr, ab_scr, bb_scr, sf_scr, sb_scr):
    c, nc = pl.program_id(0), pl.num_programs(0)
    tl = fm_ref.shape[0]

    @pl.when(c == 0)
    def _():
        sf_scr[...] = h0f_ref[...]
        sb_scr[...] = h0b_ref[...]

    _lru_coeffs(fm_ref, fp_ref, fn_ref, c == 0, c == nc - 1, cw_ref, cb_ref, wdf_ref,
                br_ref.at[0:1], bi_ref.at[0:1], lam_ref.at[0:1], af_scr, bf_scr)
    _lru_coeffs(bm_ref, bp_ref, bn_ref, c == nc - 1, c == 0, cw_ref, cb_ref, wdb_ref,
                br_ref.at[1:2], bi_ref.at[1:2], lam_ref.at[1:2], ab_scr, bb_scr)

    def step(t, carry):
        hf, hb = carry
        hf = af_scr[t] * hf + bf_scr[t]
        hf_ref[t] = hf
        tb = tl - 1 - t
        hb = ab_scr[tb] * hb + bb_scr[tb]
        hb_ref[tb] = hb
        return hf, hb

    hf, hb = lax.fori_loop(0, tl, step, (sf_scr[...], sb_scr[...]))
    sf_scr[...] = hf
    sb_scr[...] = hb

    @pl.when(c == nc - 1)
    def _():
        lastf_ref[...] = hf
        lastb_ref[...] = hb


def _scan_call(xb, h0f, h0b, conv_w, conv_b, wd, b_r, b_i, lam, steps):
    ls, nb, r = xb.shape
    tl = min(steps, ls)
    nc = ls // tl
    fwd, bwd = (lambda c: c), (lambda c: nc - 1 - c)
    main = lambda ch: pl.BlockSpec((tl, nb, r), lambda c: (ch(c), 0, 0))
    prev2 = lambda ch: pl.BlockSpec((2, nb, r), lambda c: (jnp.maximum(ch(c) * (tl // 2) - 1, 0), 0, 0))
    next1 = lambda ch: pl.BlockSpec((1, nb, r), lambda c: (jnp.minimum((ch(c) + 1) * tl, ls - 1), 0, 0))
    full = lambda a: pl.BlockSpec(a.shape, lambda c: (0,) * a.ndim)
    state = pl.BlockSpec((nb, r), lambda c: (0, 0))
    wd_spec = lambda dirn: pl.BlockSpec((None,) + wd.shape[1:], lambda c: (dirn, 0, 0, 0))
    return pl.pallas_call(
        _scan_kernel,
        grid=(nc,),
        in_specs=[main(fwd), prev2(fwd), next1(fwd), main(bwd), prev2(bwd), next1(bwd),
                  state, state, full(conv_w), full(conv_b), wd_spec(0), wd_spec(1),
                  full(b_r), full(b_i), full(lam)],
        out_specs=[main(fwd), main(bwd), state, state],
        out_shape=[jax.ShapeDtypeStruct((ls, nb, r), F32), jax.ShapeDtypeStruct((ls, nb, r), F32),
                   jax.ShapeDtypeStruct((nb, r), F32), jax.ShapeDtypeStruct((nb, r), F32)],
        scratch_shapes=[pltpu.VMEM((tl, nb, r), F32)] * 4 + [pltpu.VMEM((nb, r), F32)] * 2,
        compiler_params=_cparams(("arbitrary",), VMEM_LIMIT_BYTES),
        name="rglru_scan",
    )(xb, xb, xb, xb, xb, xb, h0f, h0b, conv_w, conv_b, wd, wd, b_r, b_i, lam)


def _paired_block_diag(w_r, w_i):
    def pair(w):
        ndir = w.shape[0]
        w = w.reshape(ndir, LRU_BLOCKS // 2, 2, LRU_BLOCK_DIM, LRU_BLOCK_DIM)
        z = jnp.zeros_like(w[:, :, 0])
        top = jnp.concatenate([w[:, :, 0], z], axis=-1)
        bot = jnp.concatenate([z, w[:, :, 1]], axis=-1)
        return jnp.concatenate([top, bot], axis=-2)
    return jnp.concatenate([pair(w_r), pair(w_i)], axis=-1).astype(BF16)


ROUTE_IDX_LANE = N_EXPERTS
ROUTE_W_LANE = N_EXPERTS + TOP_K
ROUTE_RANK_LANE = N_EXPERTS + 2 * TOP_K


def _split_bf16(a):
    hi = a.astype(BF16)
    return hi, (a - hi.astype(F32)).astype(BF16)


def _route(h, wr_ref, info_ref, cnt_ref):
    h_hi, h_lo = _split_bf16(h)
    both = jnp.dot(h_hi, wr_ref[...], preferred_element_type=F32)
    logits = (both[:, :LANES] + both[:, LANES:]
              + jnp.dot(h_lo, wr_ref[:, :LANES], preferred_element_type=F32))
    lane = lax.broadcasted_iota(jnp.int32, logits.shape, 1)
    logits = jnp.where(lane < N_EXPERTS, logits, -jnp.inf)
    m1 = jnp.max(logits, axis=-1, keepdims=True)
    i1 = jnp.min(jnp.where(logits == m1, lane, LANES), axis=-1, keepdims=True)
    rest = jnp.where(lane == i1, -jnp.inf, logits)
    m2 = jnp.max(rest, axis=-1, keepdims=True)
    i2 = jnp.min(jnp.where(rest == m2, lane, LANES), axis=-1, keepdims=True)
    e2 = jnp.exp(m2 - m1)
    w1 = 1.0 / (1.0 + e2)
    w2 = e2 / (1.0 + e2)
    chosen = jnp.where((lane == i1) | (lane == i2), 1.0, 0.0)
    tm = h.shape[0]
    earlier = (lax.broadcasted_iota(jnp.int32, (tm, tm), 0)
               > lax.broadcasted_iota(jnp.int32, (tm, tm), 1))
    rank = jnp.dot(jnp.where(earlier, 1.0, 0.0).astype(BF16), chosen.astype(BF16),
                   preferred_element_type=F32)
    r1 = jnp.sum(jnp.where(lane == i1, rank, 0.0), axis=-1, keepdims=True)
    r2 = jnp.sum(jnp.where(lane == i2, rank, 0.0), axis=-1, keepdims=True)
    info = jnp.where(lane == i1, w1, jnp.where(lane == i2, w2, 0.0))
    for k, val in enumerate((i1.astype(F32), i2.astype(F32), w1, w2, r1, r2)):
        info = jnp.where(lane == ROUTE_IDX_LANE + k, val, info)
    info_ref[...] = info
    cnt_ref[...] = jnp.sum(chosen, axis=0, keepdims=True)


def _outproj_kernel(gate_ref, hf_ref, hb_ref, w_ref, x_ref, mod_ref, g_ref, b_ref, wr_ref,
                    out_ref, h_ref, info_ref, cnt_ref):
    d = x_ref.shape[-1]
    gt = gate_ref[...]
    gelu = 0.5 * gt * (1.0 + jnp.tanh(0.7978845608028654 * (gt + 0.044715 * (gt * gt * gt))))
    y = gelu * (hf_ref[...] + hb_ref[...])
    proj = jnp.dot(y.astype(BF16), w_ref[...], preferred_element_type=F32)
    xn = _layer_norm(ALPHA * x_ref[...] + _mod_slice(mod_ref, 2, d) * proj, g_ref[...], b_ref[...])
    out_ref[...] = xn
    h = xn * (1.0 + _mod_slice(mod_ref, 4, d)) + _mod_slice(mod_ref, 3, d)
    h_ref[...] = h.astype(BF16)
    _route(h, wr_ref, info_ref, cnt_ref)


def _outproj_call(gate, hf, hb, w_out, x, mod, ln_g, ln_b, router_padded, rows):
    b, l, d = x.shape
    r = gate.shape[-1]
    tm = rows
    nt = l // tm
    tmajor = pl.BlockSpec((tm, r), lambda i, j: (j, i))
    return pl.pallas_call(
        _outproj_kernel,
        grid=(b, nt),
        in_specs=[
            pl.BlockSpec((None, tm, r), lambda i, j: (i, j, 0)),
            tmajor, tmajor,
            pl.BlockSpec(w_out.shape, lambda i, j: (0, 0)),
            pl.BlockSpec((None, tm, d), lambda i, j: (i, j, 0)),
            pl.BlockSpec((None, 1, 6 * d), lambda i, j: (i, 0, 0)),
            pl.BlockSpec((1, d), lambda i, j: (0, 0)),
            pl.BlockSpec((1, d), lambda i, j: (0, 0)),
            pl.BlockSpec(router_padded.shape, lambda i, j: (0, 0)),
        ],
        out_specs=[
            pl.BlockSpec((None, tm, d), lambda i, j: (i, j, 0)),
            pl.BlockSpec((tm, d), lambda i, j: (i * nt + j, 0)),
            pl.BlockSpec((tm, LANES), lambda i, j: (i * nt + j, 0)),
            pl.BlockSpec((None, 1, LANES), lambda i, j: (i * nt + j, 0, 0)),
        ],
        out_shape=[jax.ShapeDtypeStruct((b, l, d), F32), jax.ShapeDtypeStruct((b * l, d), BF16),
                   jax.ShapeDtypeStruct((b * l, LANES), F32),
                   jax.ShapeDtypeStruct((b * nt, 1, LANES), F32)],
        compiler_params=_cparams(("arbitrary", "arbitrary"), VMEM_LIMIT_BYTES),
        name="lru_out_proj_route",
    )(gate, hf, hb, w_out, x, mod, ln_g, ln_b, router_padded)


SEG_ALIGN = SUBLANES
MAX_SEG_PAD = N_EXPERTS * (SEG_ALIGN - 1)
BF16_ROWS = 2 * SUBLANES


def _window_rows(window):
    return -(-(TOP_K * window + MAX_SEG_PAD) // BF16_ROWS) * BF16_ROWS


def _for_each_piece(seg_refs, w, window, fn):
    loff_ref, goff_ref, cpad_ref = seg_refs
    for e in range(N_EXPERTS):
        lo, go, c = (ref[w * N_EXPERTS + e] for ref in (loff_ref, goff_ref, cpad_ref))
        size = window
        while size >= SEG_ALIGN:
            done = c & ~(2 * size - 1)

            @pl.when((c & size) != 0)
            def _(lo=lo, go=go, done=done, size=size):
                fn(pl.multiple_of(lo + done, SEG_ALIGN), pl.multiple_of(go + done, SEG_ALIGN), size)

            size //= 2


def _for_each_tail_piece(tail_ref, len_ref, max_size, fn):
    for e in range(N_EXPERTS):
        go, c = tail_ref[e], len_ref[e]
        size = max_size
        while size >= SEG_ALIGN:
            done = c & ~(2 * size - 1)

            @pl.when((c & size) != 0)
            def _(go=go, done=done, size=size):
                fn(pl.multiple_of(go + done, SEG_ALIGN), size)

            size //= 2


def _dispatch_kernel(loff_ref, goff_ref, cpad_ref, tail_ref, tlen_ref, dloc_ref, wts_ref, h_ref,
                     xs_hbm, buf, zbuf, sems):
    w, nw = pl.program_id(0), pl.num_programs(0)
    slot = w % 2
    window, d = h_ref.shape
    r = buf.shape[1]
    segs = (loff_ref, goff_ref, cpad_ref)

    def copy(buf_slot, lrow, grow, size):
        return pltpu.make_async_copy(buf.at[buf_slot, pl.ds(lrow, size)],
                                     xs_hbm.at[pl.ds(grow, size)], sems.at[buf_slot])

    row = lax.broadcasted_iota(jnp.int32, (r, window), 0)
    hit0 = row == dloc_ref[0:1, :]
    hit1 = row == dloc_ref[1:2, :]
    perm = jnp.where(hit0 | hit1, 1.0, 0.0).astype(BF16)
    buf[slot, :, :d] = jnp.dot(perm, h_ref[...], preferred_element_type=F32)
    roww = jnp.sum(jnp.where(hit0, wts_ref[0:1, :], 0.0) + jnp.where(hit1, wts_ref[1:2, :], 0.0),
                   axis=1, keepdims=True)
    buf[slot, :, d:] = jnp.broadcast_to(roww, (r, LANES))

    _for_each_piece(segs, w, window, lambda l, g, s: copy(slot, l, g, s).start())

    @pl.when(w > 0)
    def _():
        _for_each_piece(segs, w - 1, window, lambda l, g, s: copy(1 - slot, l, g, s).wait())

    @pl.when(w == nw - 1)
    def _():
        _for_each_piece(segs, w, window, lambda l, g, s: copy(slot, l, g, s).wait())
        zbuf[...] = jnp.zeros_like(zbuf)
        fill = lambda g, s: pltpu.make_async_copy(zbuf.at[pl.ds(0, s)], xs_hbm.at[pl.ds(g, s)],
                                                  sems.at[2])
        _for_each_tail_piece(tail_ref, tlen_ref, zbuf.shape[0], lambda g, s: fill(g, s).start())
        _for_each_tail_piece(tail_ref, tlen_ref, zbuf.shape[0], lambda g, s: fill(g, s).wait())


def _dispatch_call(plan, h, window, rows, n_rows):
    n, d = h.shape
    nw = n // window
    r = _window_rows(window)
    width = d + LANES
    grid_spec = pltpu.PrefetchScalarGridSpec(
        num_scalar_prefetch=5,
        grid=(nw,),
        in_specs=[
            pl.BlockSpec((None, TOP_K, window), lambda w, *_: (w, 0, 0)),
            pl.BlockSpec((None, TOP_K, window), lambda w, *_: (w, 0, 0)),
            pl.BlockSpec((window, d), lambda w, *_: (w, 0)),
        ],
        out_specs=pl.BlockSpec(memory_space=pl.ANY),
        scratch_shapes=[pltpu.VMEM((2, r, width), F32), pltpu.VMEM((rows // 2, width), F32),
                        pltpu.SemaphoreType.DMA((3,))],
    )
    return pl.pallas_call(
        _dispatch_kernel,
        grid_spec=grid_spec,
        out_shape=jax.ShapeDtypeStruct((n_rows, width), F32),
        compiler_params=_cparams(("arbitrary",), VMEM_LIMIT_BYTES),
        name="expert_dispatch",
    )(plan["loff"], plan["goff"], plan["cpad"], plan["tail"], plan["tail_len"], plan["dloc_rows"],
      plan["wts_rows"], h)


def _expert_kernel(te_ref, nv_ref, xs_ref, wg_ref, wu_ref, wd_ref, y_ref, hb_scr, acc_scr):
    i, f, nf = pl.program_id(0), pl.program_id(1), pl.num_programs(1)
    d = y_ref.shape[-1]
    valid = i < nv_ref[0]

    @pl.when(valid & (f == 0))
    def _():
        hb_scr[...] = xs_ref[:, :d].astype(BF16)
        acc_scr[...] = jnp.zeros_like(acc_scr)

    @pl.when(valid)
    def _():
        acc_scr[...] += _swiglu_chunk(hb_scr[...], wg_ref[...].astype(BF16),
                                      wu_ref[...].astype(BF16), wd_ref[...].astype(BF16))

    @pl.when(valid & (f == nf - 1))
    def _():
        y_ref[...] = acc_scr[...] * xs_ref[:, d:d + 1]

    @pl.when(jnp.logical_not(valid) & (f == nf - 1))
    def _():
        y_ref[...] = jnp.zeros_like(y_ref)


def _expert_call(plan, xs, w_gu, w_dn, rows, cols):
    n_rows, width = xs.shape
    d = width - LANES
    nt = n_rows // rows
    ff = w_dn.shape[1]
    fc = cols
    nf = ff // fc
    grid_spec = pltpu.PrefetchScalarGridSpec(
        num_scalar_prefetch=2,
        grid=(nt, nf),
        in_specs=[
            pl.BlockSpec((rows, width), lambda i, f, te, nv: (jnp.minimum(i, nv[0] - 1), 0)),
            pl.BlockSpec((None, d, fc), lambda i, f, te, nv: (te[i], 0, f)),
            pl.BlockSpec((None, d, fc), lambda i, f, te, nv: (te[i], 0, nf + f)),
            pl.BlockSpec((None, fc, d), lambda i, f, te, nv: (te[i], f, 0)),
        ],
        out_specs=pl.BlockSpec((rows, d), lambda i, f, te, nv: (i, 0)),
        scratch_shapes=[pltpu.VMEM((rows, d), BF16), pltpu.VMEM((rows, d), F32)],
    )
    return pl.pallas_call(
        _expert_kernel,
        grid_spec=grid_spec,
        out_shape=jax.ShapeDtypeStruct((n_rows, d), F32),
        compiler_params=_cparams(("arbitrary", "arbitrary"), VMEM_LIMIT_BYTES),
        name="expert_swiglu",
    )(plan["tile_expert"], plan["n_valid"], xs, w_gu, w_gu, w_dn)


def _combine_kernel(loff_ref, goff_ref, cpad_ref, dcol_ref, y_hbm, x_ref, mod_ref, g_ref, b_ref,
                    out_ref, ybuf, sems):
    w, nw = pl.program_id(0), pl.num_programs(0)
    slot = w % 2
    window, d = x_ref.shape
    r = ybuf.shape[1]
    segs = (loff_ref, goff_ref, cpad_ref)

    def copy(buf_slot, lrow, grow, size):
        return pltpu.make_async_copy(y_hbm.at[pl.ds(grow, size)],
                                     ybuf.at[buf_slot, pl.ds(lrow, size)], sems.at[buf_slot])

    @pl.when(w == 0)
    def _():
        ybuf[...] = jnp.zeros_like(ybuf)
        _for_each_piece(segs, 0, window, lambda l, g, s: copy(0, l, g, s).start())

    @pl.when(w + 1 < nw)
    def _():
        _for_each_piece(segs, w + 1, window, lambda l, g, s: copy(1 - slot, l, g, s).start())

    _for_each_piece(segs, w, window, lambda l, g, s: copy(slot, l, g, s).wait())

    lane = lax.broadcasted_iota(jnp.int32, (window, r), 1)
    hit = (lane == dcol_ref[:, 0:1]) | (lane == dcol_ref[:, 1:2])
    perm = jnp.where(hit, 1.0, 0.0).astype(BF16)
    hi, lo = _split_bf16(ybuf[slot])
    f = (jnp.dot(perm, hi, preferred_element_type=F32)
         + jnp.dot(perm, lo, preferred_element_type=F32))
    out_ref[...] = _layer_norm(ALPHA * x_ref[...] + _mod_slice(mod_ref, 5, d) * f,
                               g_ref[...], b_ref[...])


def _combine_call(plan, y, x, mod, ln_g, ln_b, window):
    b, l, d = x.shape
    wpb = l // window
    r = _window_rows(window)
    grid_spec = pltpu.PrefetchScalarGridSpec(
        num_scalar_prefetch=3,
        grid=(b * wpb,),
        in_specs=[
            pl.BlockSpec((window, TOP_K), lambda w, *_: (w, 0)),
            pl.BlockSpec(memory_space=pl.ANY),
            pl.BlockSpec((None, window, d), lambda w, *_: (w // wpb, w % wpb, 0)),
            pl.BlockSpec((None, 1, 6 * d), lambda w, *_: (w // wpb, 0, 0)),
            pl.BlockSpec((1, d), lambda w, *_: (0, 0)),
            pl.BlockSpec((1, d), lambda w, *_: (0, 0)),
        ],
        out_specs=pl.BlockSpec((None, window, d), lambda w, *_: (w // wpb, w % wpb, 0)),
        scratch_shapes=[pltpu.VMEM((2, r, d), F32), pltpu.SemaphoreType.DMA((2,))],
    )
    return pl.pallas_call(
        _combine_kernel,
        grid_spec=grid_spec,
        out_shape=jax.ShapeDtypeStruct((b, l, d), F32),
        compiler_params=_cparams(("arbitrary",), VMEM_LIMIT_BYTES),
        name="expert_combine",
    )(plan["loff"], plan["goff"], plan["cpad"], plan["dloc_cols"], y, x, mod, ln_g, ln_b)


def _dispatch_plan(info, counts, window, rows):
    n = info.shape[0]
    nw = n // window
    as_int = lambda lane0: info[:, lane0:lane0 + TOP_K].astype(jnp.int32).reshape(nw, window, TOP_K)
    experts, rank = as_int(ROUTE_IDX_LANE), as_int(ROUTE_RANK_LANE)
    count = counts[:, 0, :N_EXPERTS].astype(jnp.int32)
    cpad = -(-count // SEG_ALIGN) * SEG_ALIGN
    loff = jnp.cumsum(cpad, axis=1) - cpad
    group = -(-jnp.sum(cpad, axis=0) // rows) * rows
    gend = jnp.cumsum(group)
    goff = (gend - group)[None, :] + jnp.cumsum(cpad, axis=0) - cpad
    is_e = experts[..., None] == jnp.arange(N_EXPERTS)
    dloc = jnp.sum(jnp.where(is_e, loff[:, None, None, :], 0), axis=-1) + rank
    n_tiles = -(-(TOP_K * n + nw * MAX_SEG_PAD) // rows) + N_EXPERTS
    tile_start = jnp.arange(n_tiles) * rows
    tile_expert = jnp.minimum(jnp.sum(tile_start[:, None] >= gend[None, :], axis=1), N_EXPERTS - 1)
    wts = info[:, ROUTE_W_LANE:ROUTE_W_LANE + TOP_K].reshape(nw, window, TOP_K)
    flat = lambda a: a.reshape(-1).astype(jnp.int32)
    return dict(
        loff=flat(loff), goff=flat(goff), cpad=flat(cpad),
        dloc_rows=jnp.transpose(dloc, (0, 2, 1)).astype(jnp.int32),
        dloc_cols=dloc.reshape(n, TOP_K).astype(jnp.int32),
        wts_rows=jnp.transpose(wts, (0, 2, 1)),
        tail=flat(goff[-1] + cpad[-1]), tail_len=flat(gend - goff[-1] - cpad[-1]),
        tile_expert=tile_expert.astype(jnp.int32),
        n_valid=(gend[-1] // rows).reshape(1).astype(jnp.int32),
        n_rows=n_tiles * rows,
    )


def _attention_layer(x, ctx, mod_x, mod_c, ln_g, ln_b, w_qkv, w_o, sink, w_gu, w_dn, grid_w, ts):
    b, l, d = x.shape
    lc = ctx.shape[1]
    g0, b0, g1, b1 = ln_g[0:1], ln_b[0:1], ln_g[1:2], ln_b[1:2]
    w_qkv, w_o, w_gu, w_dn = (w.astype(BF16) for w in (w_qkv, w_o, w_gu, w_dn))
    ctx_flat = ctx.reshape(1, b * lc, d)
    n_kv = N_KV_HEADS * HEAD_DIM

    q, k, v = _qkv_call(x, mod_x, w_qkv, _rope_tables(l, grid_w), ts["qkv_rows"])
    qc, kc, vc = _qkv_call(ctx_flat, mod_c, w_qkv, _no_rope_tables(b * lc), ts["qkv_rows"])
    qc = qc.reshape(b, lc, -1)
    kc, vc = kc.reshape(b, lc, n_kv), vc.reshape(b, lc, n_kv)
    mod_cb = jnp.broadcast_to(mod_c, (b,) + mod_c.shape[1:])

    x = _attn_call(q, k, v, kc, vc, sink, w_o, x, mod_x, g0, b0, band=True)
    ctx = _attn_call(qc, None, None, kc, vc, sink, w_o, ctx, mod_cb, g0, b0, band=False)

    x = _ffn_call(x, mod_x, w_gu, w_dn, g1, b1, ts["ffn_rows"], ts["ffn_cols"])
    ctx = _ffn_call(ctx.reshape(1, b * lc, d), mod_c, w_gu, w_dn, g1, b1,
                    ts["ffn_rows"], ts["ffn_cols"]).reshape(b, lc, d)
    return x, ctx


def _recurrent_layer(x, ctx, mod_x, mod_c, ln_g, ln_b, w_in, conv_w, conv_b, lam, w_r, b_r, w_i,
                     b_i, w_out, router, w_gu, w_dn, ts):
    b, l, d = x.shape
    g0, b0, g1, b1 = ln_g[0:1], ln_b[0:1], ln_g[1:2], ln_b[1:2]
    w_in, w_out = w_in.astype(BF16), w_out.astype(BF16)
    wd = _paired_block_diag(w_r, w_i)
    conv_b = conv_b.reshape(1, -1)
    mod_cb = jnp.broadcast_to(mod_c, (b,) + mod_c.shape[1:])

    r = w_out.shape[0]
    gate, xb = _inproj_call(x, mod_x, w_in, ts["proj_rows"])
    _, xb_c = _inproj_call(ctx, mod_cb, w_in, ts["proj_rows"])
    zero = jnp.zeros((b, r), F32)
    scan = functools.partial(_scan_call, conv_w=conv_w, conv_b=conv_b, wd=wd, b_r=b_r, b_i=b_i,
                             lam=lam, steps=ts["scan_steps"])
    _, _, cf, cb = scan(xb_c.reshape(-1, b, r), zero, zero)
    hf, hb, _, _ = scan(xb.reshape(l, b, r), cf, cb)
    router_padded = jnp.concatenate(
        _split_bf16(jnp.pad(router, ((0, 0), (0, LANES - N_EXPERTS)))), axis=1)
    x, h, info, counts = _outproj_call(gate, hf.reshape(l, b * r), hb.reshape(l, b * r), w_out, x,
                                       mod_x, g0, b0, router_padded, ts["moe_window"])
    plan = _dispatch_plan(info, counts, ts["moe_window"], ts["moe_rows"])
    xs = _dispatch_call(plan, h, ts["moe_window"], ts["moe_rows"], plan["n_rows"])
    y = _expert_call(plan, xs, w_gu, w_dn, ts["moe_rows"], ts["ffn_cols"])
    return _combine_call(plan, y, x, mod_x, g1, b1, ts["moe_window"])


def kernel(x, c, ctx, c_ctx, w_mod, b_mod, ln_g, ln_b, attn_w_qkv, attn_w_o, attn_sink, ffn_w_gu,
           ffn_w_dn, lru_w_in, lru_conv_w, lru_conv_b, lru_lambda, lru_w_r, lru_b_r, lru_w_i,
           lru_b_i, lru_w_out, moe_router, moe_w_gu, moe_w_dn):
    b, l, d = x.shape
    assert w_mod.shape[0] == DEPTH and l % BLOCK == 0 and ctx.shape[1] % BLOCK == 0
    grid_w = 64
    ts = _tile_sizes()

    rows = -(-(b + 1) // SUBLANES) * SUBLANES
    cond = jnp.zeros((rows, d), F32).at[:b].set(c).at[b].set(c_ctx)
    mod = _mod_call(cond, w_mod, b_mod[:, None, :])
    mod_x = lambda i: mod[i, :b, None, :]
    mod_c = lambda i: mod[i, b:b + 1, None, :]

    x, ctx = _attention_layer(x, ctx, mod_x(0), mod_c(0), ln_g[0], ln_b[0], attn_w_qkv[0],
                              attn_w_o[0], attn_sink[0], ffn_w_gu[0], ffn_w_dn[0], grid_w, ts)
    return _recurrent_layer(x, ctx, mod_x(1), mod_c(1), ln_g[1], ln_b[1], lru_w_in[0],
                            lru_conv_w[0], lru_conv_b[0], lru_lambda[0], lru_w_r[0], lru_b_r[0],
                            lru_w_i[0], lru_b_i[0], lru_w_out[0], moe_router[0], moe_w_gu[0],
                            moe_w_dn[0], ts)
```

```python
import functools

import jax
import jax.numpy as jnp
import numpy as np
from jax import lax
from jax.experimental import pallas as pl
from jax.experimental.pallas import tpu as pltpu

N_HEADS = 8
N_KV_HEADS = 2
GROUP = N_HEADS // N_KV_HEADS
HEAD_DIM = 128
BLOCK = 128
AXIS_DIM = HEAD_DIM // 2
ROPE_BASE = 10000.0
LRU_BLOCKS = 8
LRU_BLOCK_DIM = 128
LRU_C = 8.0
N_EXPERTS = 8
TOP_K = 2
DEPTH = 2
ALPHA = (2 * DEPTH) ** 0.25
LN_EPS = 1e-5
NEG_INF = -1e30

LANES = 128
SUBLANES = 8
VMEM_LIMIT_BYTES = 56 * 1024 * 1024

F32 = jnp.float32
BF16 = jnp.bfloat16


def _tile_sizes():
    return dict(
        qkv_rows=512,
        ffn_rows=1024,
        ffn_cols=512,
        proj_rows=512,
        scan_steps=64,
        moe_window=512,
        moe_rows=1024,
    )


def _cparams(sem, vmem=None):
    return pltpu.CompilerParams(dimension_semantics=sem, vmem_limit_bytes=vmem)


def _layer_norm(h, g, b):
    mu = jnp.mean(h, axis=-1, keepdims=True)
    d = h - mu
    var = jnp.mean(d * d, axis=-1, keepdims=True)
    return d * lax.rsqrt(var + LN_EPS) * g + b


def _mod_slice(mod_ref, k, d):
    return mod_ref[:, k * d:(k + 1) * d]


def _mod_kernel(c_ref, w_ref, b_ref, o_ref):
    c = c_ref[...]
    s = c * jax.nn.sigmoid(c)
    o_ref[...] = jnp.dot(s, w_ref[...], preferred_element_type=F32,
                         precision=lax.Precision.HIGHEST) + b_ref[...]


def _mod_call(cond, w_mod, b_mod):
    rows, d = cond.shape
    depth, _, n = w_mod.shape
    tn = 1536
    return pl.pallas_call(
        _mod_kernel,
        grid=(depth, n // tn),
        in_specs=[
            pl.BlockSpec((rows, d), lambda i, j: (0, 0)),
            pl.BlockSpec((None, d, tn), lambda i, j: (i, 0, j)),
            pl.BlockSpec((None, 1, tn), lambda i, j: (i, 0, j)),
        ],
        out_specs=pl.BlockSpec((None, rows, tn), lambda i, j: (i, 0, j)),
        out_shape=jax.ShapeDtypeStruct((depth, rows, n), F32),
        compiler_params=_cparams(("arbitrary", "arbitrary")),
        name="mod_vectors",
    )(cond, w_mod, b_mod)


def _qkv_kernel(x_ref, mod_ref, w_ref, cos_ref, sa_ref, sb_ref, q_ref, k_ref, v_ref):
    d = x_ref.shape[-1]
    h = x_ref[...] * (1.0 + _mod_slice(mod_ref, 1, d)) + _mod_slice(mod_ref, 0, d)
    qkv = jnp.dot(h.astype(BF16), w_ref[...], preferred_element_type=F32)
    cos, sa, sb = cos_ref[...], sa_ref[...], sb_ref[...]
    scale = HEAD_DIM ** -0.5
    for head in range(N_HEADS + N_KV_HEADS):
        u = qkv[:, head * HEAD_DIM:(head + 1) * HEAD_DIM]
        r = (u * cos + pltpu.roll(u, HEAD_DIM - AXIS_DIM // 2, axis=1) * sa
             + pltpu.roll(u, AXIS_DIM // 2, axis=1) * sb)
        if head < N_HEADS:
            q_ref[:, head * HEAD_DIM:(head + 1) * HEAD_DIM] = (r * scale).astype(BF16)
        else:
            kh = head - N_HEADS
            k_ref[:, kh * HEAD_DIM:(kh + 1) * HEAD_DIM] = r.astype(BF16)
    kv = N_KV_HEADS * HEAD_DIM
    v_ref[...] = qkv[:, N_HEADS * HEAD_DIM + kv:].astype(BF16)


def _qkv_call(x, mod, w_qkv, tabs, rows):
    b, l, d = x.shape
    n_q, n_kv = N_HEADS * HEAD_DIM, N_KV_HEADS * HEAD_DIM
    tm = min(rows, l)
    tab_spec = pl.BlockSpec((tm, HEAD_DIM), lambda i, j: (j, 0))
    return pl.pallas_call(
        _qkv_kernel,
        grid=(b, l // tm),
        in_specs=[
            pl.BlockSpec((None, tm, d), lambda i, j: (i, j, 0)),
            pl.BlockSpec((None, 1, 6 * d), lambda i, j: (i, 0, 0)),
            pl.BlockSpec(w_qkv.shape, lambda i, j: (0, 0)),
            tab_spec, tab_spec, tab_spec,
        ],
        out_specs=[
            pl.BlockSpec((None, tm, n_q), lambda i, j: (i, j, 0)),
            pl.BlockSpec((None, tm, n_kv), lambda i, j: (i, j, 0)),
            pl.BlockSpec((None, tm, n_kv), lambda i, j: (i, j, 0)),
        ],
        out_shape=[
            jax.ShapeDtypeStruct((b, l, n_q), BF16),
            jax.ShapeDtypeStruct((b, l, n_kv), BF16),
            jax.ShapeDtypeStruct((b, l, n_kv), BF16),
        ],
        compiler_params=_cparams(("arbitrary", "arbitrary"), VMEM_LIMIT_BYTES),
        name="qkv_rope",
    )(x, mod, w_qkv, *tabs)


def _rope_tables(l, grid_w):
    f32 = np.float32
    t = np.arange(l)
    pos = np.stack([(t // grid_w).astype(f32), (t % grid_w).astype(f32)], axis=1)
    freqs = f32(ROPE_BASE) ** (-np.arange(0, AXIS_DIM, 2, dtype=f32) / f32(AXIS_DIM))
    lane = np.arange(HEAD_DIM)
    ang = (pos[:, lane // AXIS_DIM] * freqs[lane % (AXIS_DIM // 2)][None, :]).astype(f32)
    first = (lane % AXIS_DIM) < AXIS_DIM // 2
    cos, sin = np.cos(ang).astype(f32), np.sin(ang).astype(f32)
    zero = f32(0.0)
    return cos, np.where(first, -sin, zero), np.where(first, zero, sin)


def _no_rope_tables(l):
    z = np.zeros((l, HEAD_DIM), np.float32)
    return np.ones((l, HEAD_DIM), np.float32), z, z


def _nt_dot(a, b):
    return lax.dot_general(a, b, (((1,), (1,)), ((), ())), preferred_element_type=F32)


def _attend(qs, segs, sink_col):
    k_all = jnp.concatenate([k for k, _, _ in segs], axis=0)
    v_all = jnp.concatenate([v for _, v, _ in segs], axis=0)
    v_ext = jnp.concatenate([v_all, jnp.ones_like(v_all)], axis=1)
    s_all = _nt_dot(qs, k_all)
    tiles, col = [], 0
    for k, _, mask in segs:
        for c in range(col, col + k.shape[0], LANES):
            s = s_all[:, c:c + LANES]
            tiles.append(s if mask is None else jnp.where(mask, s, NEG_INF))
        col += k.shape[0]
    m_tile = tiles[0]
    for s in tiles[1:]:
        m_tile = jnp.maximum(m_tile, s)
    m = jnp.maximum(sink_col, jnp.max(m_tile, axis=-1, keepdims=True))
    p_all = jnp.concatenate([jnp.exp(s - m).astype(BF16) for s in tiles], axis=1)
    res = jnp.dot(p_all, v_ext, preferred_element_type=F32)
    hd = v_all.shape[1]
    return res[:, :hd] / (res[:, hd:hd + 1] + jnp.exp(sink_col - m))


def _attn_epilogue(o_scr, wo_ref, x_ref, mod_ref, g_ref, b_ref, out_ref):
    d = x_ref.shape[-1]
    proj = jnp.dot(o_scr[...], wo_ref[...], preferred_element_type=F32)
    out_ref[...] = _layer_norm(ALPHA * x_ref[...] + _mod_slice(mod_ref, 2, d) * proj,
                               g_ref[...], b_ref[...])


def _sink_column(sink_ref, kvh):
    return jnp.concatenate(
        [jnp.full((BLOCK, 1), sink_ref[kvh * GROUP + g], F32) for g in range(GROUP)], axis=0)


def _stack_group(q_ref, kvh):
    return jnp.concatenate(
        [q_ref[:, (kvh * GROUP + g) * HEAD_DIM:(kvh * GROUP + g + 1) * HEAD_DIM]
         for g in range(GROUP)], axis=0)


def _unstack_group(o, o_scr, kvh):
    for g in range(GROUP):
        col = (kvh * GROUP + g) * HEAD_DIM
        o_scr[:, col:col + HEAD_DIM] = o[g * BLOCK:(g + 1) * BLOCK].astype(BF16)


def _band_attn_kernel(sink_ref, q_ref, kp_ref, kc_ref, kn_ref, vp_ref, vc_ref, vn_ref,
                      kx_ref, vx_ref, wo_ref, x_ref, mod_ref, g_ref, b_ref, out_ref, o_scr):
    n, nb = pl.program_id(1), pl.num_programs(1)
    row = lax.broadcasted_iota(jnp.int32, (GROUP * BLOCK, BLOCK), 0) % BLOCK
    col = lax.broadcasted_iota(jnp.int32, (GROUP * BLOCK, BLOCK), 1)
    mask_prev = (col >= row) & (n > 0)
    mask_next = (col <= row) & (n < nb - 1)
    for kvh in range(N_KV_HEADS):
        hs = slice(kvh * HEAD_DIM, (kvh + 1) * HEAD_DIM)
        segs = [(kp_ref[:, hs], vp_ref[:, hs], mask_prev),
                (kc_ref[:, hs], vc_ref[:, hs], None),
                (kn_ref[:, hs], vn_ref[:, hs], mask_next),
                (kx_ref[:, hs], vx_ref[:, hs], None)]
        o = _attend(_stack_group(q_ref, kvh), segs, _sink_column(sink_ref, kvh))
        _unstack_group(o, o_scr, kvh)
    _attn_epilogue(o_scr, wo_ref, x_ref, mod_ref, g_ref, b_ref, out_ref)


def _ctx_attn_kernel(sink_ref, q_ref, kx_ref, vx_ref, wo_ref, x_ref, mod_ref, g_ref, b_ref,
                     out_ref, o_scr):
    for kvh in range(N_KV_HEADS):
        hs = slice(kvh * HEAD_DIM, (kvh + 1) * HEAD_DIM)
        o = _attend(_stack_group(q_ref, kvh), [(kx_ref[:, hs], vx_ref[:, hs], None)],
                    _sink_column(sink_ref, kvh))
        _unstack_group(o, o_scr, kvh)
    _attn_epilogue(o_scr, wo_ref, x_ref, mod_ref, g_ref, b_ref, out_ref)


def _attn_call(q, k, v, kx, vx, sink, w_o, x, mod, ln_g, ln_b, band):
    b, l, d = x.shape
    lc = kx.shape[1]
    nb = l // BLOCK
    n_q, n_kv = N_HEADS * HEAD_DIM, N_KV_HEADS * HEAD_DIM
    kv_blk = lambda f: pl.BlockSpec((None, BLOCK, n_kv), f)
    common_in = [
        pl.BlockSpec((None, lc, n_kv), lambda i, j: (i, 0, 0)),
        pl.BlockSpec((None, lc, n_kv), lambda i, j: (i, 0, 0)),
        pl.BlockSpec(w_o.shape, lambda i, j: (0, 0)),
        pl.BlockSpec((None, BLOCK, d), lambda i, j: (i, j, 0)),
        pl.BlockSpec((None, 1, 6 * d), lambda i, j: (i, 0, 0)),
        pl.BlockSpec((1, d), lambda i, j: (0, 0)),
        pl.BlockSpec((1, d), lambda i, j: (0, 0)),
    ]
    head_in = [pl.BlockSpec(memory_space=pltpu.SMEM),
               pl.BlockSpec((None, BLOCK, n_q), lambda i, j: (i, j, 0))]
    if band:
        prev = lambda i, j: (i, jnp.maximum(j - 1, 0), 0)
        cur = lambda i, j: (i, j, 0)
        nxt = lambda i, j: (i, jnp.minimum(j + 1, nb - 1), 0)
        in_specs = head_in + [kv_blk(prev), kv_blk(cur), kv_blk(nxt)] * 2 + common_in
        args = (sink, q, k, k, k, v, v, v, kx, vx, w_o, x, mod, ln_g, ln_b)
        body, name = _band_attn_kernel, "band_attention"
    else:
        in_specs = head_in + common_in
        args = (sink, q, kx, vx, w_o, x, mod, ln_g, ln_b)
        body, name = _ctx_attn_kernel, "context_attention"
    return pl.pallas_call(
        body,
        grid=(b, nb),
        in_specs=in_specs,
        out_specs=pl.BlockSpec((None, BLOCK, d), lambda i, j: (i, j, 0)),
        out_shape=jax.ShapeDtypeStruct((b, l, d), F32),
        scratch_shapes=[pltpu.VMEM((BLOCK, n_q), BF16)],
        compiler_params=_cparams(("arbitrary", "arbitrary"), VMEM_LIMIT_BYTES),
        name=name,
    )(*args)


def _swiglu_chunk(hb, wg, wu, wd):
    g = jnp.dot(hb, wg, preferred_element_type=F32)
    u = jnp.dot(hb, wu, preferred_element_type=F32)
    a = (g * jax.nn.sigmoid(g)) * u
    return jnp.dot(a.astype(BF16), wd, preferred_element_type=F32)


def _ffn_kernel(x_ref, mod_ref, wg_ref, wu_ref, wd_ref, g_ref, b_ref, out_ref, hb_scr, acc_scr):
    f, nf = pl.program_id(2), pl.num_programs(2)
    d = x_ref.shape[-1]

    @pl.when(f == 0)
    def _():
        h = x_ref[...] * (1.0 + _mod_slice(mod_ref, 4, d)) + _mod_slice(mod_ref, 3, d)
        hb_scr[...] = h.astype(BF16)
        acc_scr[...] = jnp.zeros_like(acc_scr)

    acc_scr[...] += _swiglu_chunk(hb_scr[...], wg_ref[...], wu_ref[...], wd_ref[...])

    @pl.when(f == nf - 1)
    def _():
        out_ref[...] = _layer_norm(ALPHA * x_ref[...] + _mod_slice(mod_ref, 5, d) * acc_scr[...],
                                   g_ref[...], b_ref[...])


def _ffn_call(x, mod, w_gu, w_dn, ln_g, ln_b, rows, cols):
    b, l, d = x.shape
    ff = w_dn.shape[0]
    tm, fc = min(rows, l), cols
    nf = ff // fc
    return pl.pallas_call(
        _ffn_kernel,
        grid=(b, l // tm, nf),
        in_specs=[
            pl.BlockSpec((None, tm, d), lambda i, j, f: (i, j, 0)),
            pl.BlockSpec((None, 1, 6 * d), lambda i, j, f: (i, 0, 0)),
            pl.BlockSpec((d, fc), lambda i, j, f: (0, f)),
            pl.BlockSpec((d, fc), lambda i, j, f: (0, nf + f)),
            pl.BlockSpec((fc, d), lambda i, j, f: (f, 0)),
            pl.BlockSpec((1, d), lambda i, j, f: (0, 0)),
            pl.BlockSpec((1, d), lambda i, j, f: (0, 0)),
        ],
        out_specs=pl.BlockSpec((None, tm, d), lambda i, j, f: (i, j, 0)),
        out_shape=jax.ShapeDtypeStruct((b, l, d), F32),
        scratch_shapes=[pltpu.VMEM((tm, d), BF16), pltpu.VMEM((tm, d), F32)],
        compiler_params=_cparams(("arbitrary", "arbitrary", "arbitrary"), VMEM_LIMIT_BYTES),
        name="dense_swiglu",
    )(x, mod, w_gu, w_gu, w_dn, ln_g, ln_b)


def _inproj_kernel(x_ref, mod_ref, w_ref, gate_ref, xb_ref):
    d = x_ref.shape[-1]
    r = xb_ref.shape[-1]
    h = x_ref[...] * (1.0 + _mod_slice(mod_ref, 1, d)) + _mod_slice(mod_ref, 0, d)
    y = jnp.dot(h.astype(BF16), w_ref[...], preferred_element_type=F32)
    gate_ref[...] = y[:, :r]
    xb_ref[...] = y[:, r:]


def _inproj_call(x, mod, w_in, rows):
    b, l, d = x.shape
    r = w_in.shape[1] // 2
    tm = min(rows, l)
    return pl.pallas_call(
        _inproj_kernel,
        grid=(b, l // tm),
        in_specs=[
            pl.BlockSpec((None, tm, d), lambda i, j: (i, j, 0)),
            pl.BlockSpec((None, 1, 6 * d), lambda i, j: (i, 0, 0)),
            pl.BlockSpec(w_in.shape, lambda i, j: (0, 0)),
        ],
        out_specs=[
            pl.BlockSpec((None, tm, r), lambda i, j: (i, j, 0)),
            pl.BlockSpec((tm, r), lambda i, j: (j, i)),
        ],
        out_shape=[jax.ShapeDtypeStruct((b, l, r), F32), jax.ShapeDtypeStruct((l, b * r), F32)],
        compiler_params=_cparams(("arbitrary", "arbitrary"), VMEM_LIMIT_BYTES),
        name="lru_in_proj",
    )(x, mod, w_in)


def _lru_coeffs(main_ref, prev_ref, next_ref, first, last, cw_ref, cb_ref, wd_ref, br_ref, bi_ref,
                lam_ref, a_scr, b_scr):
    tl, nb, r = main_ref.shape
    prev = jnp.where(first, 0.0, prev_ref[...])
    nxt = jnp.where(last, 0.0, next_ref[...])
    ext = jnp.concatenate([prev, main_ref[...], nxt], axis=0)
    u = cb_ref[...][None]
    for k in range(4):
        u = u + cw_ref[k:k + 1, :][None] * ext[k:k + tl]
    u = u.reshape(tl * nb, r)
    ub = u.astype(BF16)
    lam = lam_ref[...]
    softplus_neg = jnp.maximum(-lam, 0.0) + jnp.log1p(jnp.exp(-jnp.abs(lam)))
    half_rate = (-0.5 * LRU_C) * softplus_neg
    pw = 2 * LRU_BLOCK_DIM
    for p in range(LRU_BLOCKS // 2):
        cs = slice(p * pw, (p + 1) * pw)
        z = jnp.dot(ub[:, cs], wd_ref[p], preferred_element_type=F32)
        tr = jnp.tanh(z[:, :pw] + br_ref[:, cs])
        ig = 0.5 + 0.5 * jnp.tanh(z[:, pw:] + bi_ref[:, cs])
        log_a = half_rate[:, cs] * tr + half_rate[:, cs]
        a = jnp.exp(log_a)
        gain = jnp.sqrt(-jnp.tanh(log_a) * (a * a + 1.0))
        a_scr[:, :, cs] = a.reshape(tl, nb, pw)
        b_scr[:, :, cs] = (gain * (ig * u[:, cs])).reshape(tl, nb, pw)


def _scan_kernel(fm_ref, fp_ref, fn_ref, bm_ref, bp_ref, bn_ref, h0f_ref, h0b_ref,
                 cw_ref, cb_ref, wdf_ref, wdb_ref, br_ref, bi_ref, lam_ref,
                 hf_ref, hb_ref, lastf_ref, lastb_ref,
                 af_scr, bf_scr, ab_scr, bb_scr, sf_scr, sb_scr):
    c, nc = pl.program_id(0), pl.num_programs(0)
    tl = fm_ref.shape[0]

    @pl.when(c == 0)
    def _():
        sf_scr[...] = h0f_ref[...]
        sb_scr[...] = h0b_ref[...]

    _lru_coeffs(fm_ref, fp_ref, fn_ref, c == 0, c == nc - 1, cw_ref, cb_ref, wdf_ref,
                br_ref.at[0:1], bi_ref.at[0:1], lam_ref.at[0:1], af_scr, bf_scr)
    _lru_coeffs(bm_ref, bp_ref, bn_ref, c == nc - 1, c == 0, cw_ref, cb_ref, wdb_ref,
                br_ref.at[1:2], bi_ref.at[1:2], lam_ref.at[1:2], ab_scr, bb_scr)

    def step(t, carry):
        hf, hb = carry
        hf = af_scr[t] * hf + bf_scr[t]
        hf_ref[t] = hf
        tb = tl - 1 - t
        hb = ab_scr[tb] * hb + bb_scr[tb]
        hb_ref[tb] = hb
        return hf, hb

    hf, hb = lax.fori_loop(0, tl, step, (sf_scr[...], sb_scr[...]))
    sf_scr[...] = hf
    sb_scr[...] = hb

    @pl.when(c == nc - 1)
    def _():
        lastf_ref[...] = hf
        lastb_ref[...] = hb


def _scan_call(xb, h0f, h0b, conv_w, conv_b, wd, b_r, b_i, lam, steps):
    ls, nb, r = xb.shape
    tl = min(steps, ls)
    nc = ls // tl
    fwd, bwd = (lambda c: c), (lambda c: nc - 1 - c)
    main = lambda ch: pl.BlockSpec((tl, nb, r), lambda c: (ch(c), 0, 0))
    prev2 = lambda ch: pl.BlockSpec((2, nb, r), lambda c: (jnp.maximum(ch(c) * (tl // 2) - 1, 0), 0, 0))
    next1 = lambda ch: pl.BlockSpec((1, nb, r), lambda c: (jnp.minimum((ch(c) + 1) * tl, ls - 1), 0, 0))
    full = lambda a: pl.BlockSpec(a.shape, lambda c: (0,) * a.ndim)
    state = pl.BlockSpec((nb, r), lambda c: (0, 0))
    wd_spec = lambda dirn: pl.BlockSpec((None,) + wd.shape[1:], lambda c: (dirn, 0, 0, 0))
    return pl.pallas_call(
        _scan_kernel,
        grid=(nc,),
        in_specs=[main(fwd), prev2(fwd), next1(fwd), main(bwd), prev2(bwd), next1(bwd),
                  state, state, full(conv_w), full(conv_b), wd_spec(0), wd_spec(1),
                  full(b_r), full(b_i), full(lam)],
        out_specs=[main(fwd), main(bwd), state, state],
        out_shape=[jax.ShapeDtypeStruct((ls, nb, r), F32), jax.ShapeDtypeStruct((ls, nb, r), F32),
                   jax.ShapeDtypeStruct((nb, r), F32), jax.ShapeDtypeStruct((nb, r), F32)],
        scratch_shapes=[pltpu.VMEM((tl, nb, r), F32)] * 4 + [pltpu.VMEM((nb, r), F32)] * 2,
        compiler_params=_cparams(("arbitrary",), VMEM_LIMIT_BYTES),
        name="rglru_scan",
    )(xb, xb, xb, xb, xb, xb, h0f, h0b, conv_w, conv_b, wd, wd, b_r, b_i, lam)


def _paired_block_diag(w_r, w_i):
    def pair(w):
        ndir = w.shape[0]
        w = w.reshape(ndir, LRU_BLOCKS // 2, 2, LRU_BLOCK_DIM, LRU_BLOCK_DIM)
        z = jnp.zeros_like(w[:, :, 0])
        top = jnp.concatenate([w[:, :, 0], z], axis=-1)
        bot = jnp.concatenate([z, w[:, :, 1]], axis=-1)
        return jnp.concatenate([top, bot], axis=-2)
    return (0.5 * jnp.concatenate([pair(w_r), pair(w_i)], axis=-1)).astype(BF16)


ROUTE_IDX_LANE = N_EXPERTS
ROUTE_W_LANE = N_EXPERTS + TOP_K
ROUTE_RANK_LANE = N_EXPERTS + 2 * TOP_K


def _split_bf16(a):
    hi = a.astype(BF16)
    return hi, (a - hi.astype(F32)).astype(BF16)


def _route(h, wr_ref, info_ref, cnt_ref):
    h_hi, h_lo = _split_bf16(h)
    both = jnp.dot(h_hi, wr_ref[...], preferred_element_type=F32)
    logits = (both[:, :LANES] + both[:, LANES:]
              + jnp.dot(h_lo, wr_ref[:, :LANES], preferred_element_type=F32))
    lane = lax.broadcasted_iota(jnp.int32, logits.shape, 1)
    logits = jnp.where(lane < N_EXPERTS, logits, -jnp.inf)
    m1 = jnp.max(logits, axis=-1, keepdims=True)
    i1 = jnp.min(jnp.where(logits == m1, lane, LANES), axis=-1, keepdims=True)
    rest = jnp.where(lane == i1, -jnp.inf, logits)
    m2 = jnp.max(rest, axis=-1, keepdims=True)
    i2 = jnp.min(jnp.where(rest == m2, lane, LANES), axis=-1, keepdims=True)
    e2 = jnp.exp(m2 - m1)
    w1 = 1.0 / (1.0 + e2)
    w2 = e2 / (1.0 + e2)
    chosen = jnp.where((lane == i1) | (lane == i2), 1.0, 0.0)
    tm = h.shape[0]
    earlier = (lax.broadcasted_iota(jnp.int32, (tm, tm), 0)
               > lax.broadcasted_iota(jnp.int32, (tm, tm), 1))
    rank = jnp.dot(jnp.where(earlier, 1.0, 0.0).astype(BF16), chosen.astype(BF16),
                   preferred_element_type=F32)
    r1 = jnp.sum(jnp.where(lane == i1, rank, 0.0), axis=-1, keepdims=True)
    r2 = jnp.sum(jnp.where(lane == i2, rank, 0.0), axis=-1, keepdims=True)
    info = jnp.where(lane == i1, w1, jnp.where(lane == i2, w2, 0.0))
    for k, val in enumerate((i1.astype(F32), i2.astype(F32), w1, w2, r1, r2)):
        info = jnp.where(lane == ROUTE_IDX_LANE + k, val, info)
    info_ref[...] = info
    cnt_ref[...] = jnp.sum(chosen, axis=0, keepdims=True)


def _outproj_kernel(gate_ref, hf_ref, hb_ref, w_ref, x_ref, mod_ref, g_ref, b_ref, wr_ref,
                    out_ref, h_ref, info_ref, cnt_ref):
    d = x_ref.shape[-1]
    gt = gate_ref[...]
    gelu = 0.5 * gt * (1.0 + jnp.tanh(0.7978845608028654 * (gt + 0.044715 * (gt * gt * gt))))
    y = gelu * (hf_ref[...] + hb_ref[...])
    proj = jnp.dot(y.astype(BF16), w_ref[...], preferred_element_type=F32)
    xn = _layer_norm(ALPHA * x_ref[...] + _mod_slice(mod_ref, 2, d) * proj, g_ref[...], b_ref[...])
    out_ref[...] = xn
    h = xn * (1.0 + _mod_slice(mod_ref, 4, d)) + _mod_slice(mod_ref, 3, d)
    h_ref[...] = h.astype(BF16)
    _route(h, wr_ref, info_ref, cnt_ref)


def _outproj_call(gate, hf, hb, w_out, x, mod, ln_g, ln_b, router_padded, rows):
    b, l, d = x.shape
    r = gate.shape[-1]
    tm = rows
    nt = l // tm
    tmajor = pl.BlockSpec((tm, r), lambda i, j: (j, i))
    return pl.pallas_call(
        _outproj_kernel,
        grid=(b, nt),
        in_specs=[
            pl.BlockSpec((None, tm, r), lambda i, j: (i, j, 0)),
            tmajor, tmajor,
            pl.BlockSpec(w_out.shape, lambda i, j: (0, 0)),
            pl.BlockSpec((None, tm, d), lambda i, j: (i, j, 0)),
            pl.BlockSpec((None, 1, 6 * d), lambda i, j: (i, 0, 0)),
            pl.BlockSpec((1, d), lambda i, j: (0, 0)),
            pl.BlockSpec((1, d), lambda i, j: (0, 0)),
            pl.BlockSpec(router_padded.shape, lambda i, j: (0, 0)),
        ],
        out_specs=[
            pl.BlockSpec((None, tm, d), lambda i, j: (i, j, 0)),
            pl.BlockSpec((tm, d), lambda i, j: (i * nt + j, 0)),
            pl.BlockSpec((tm, LANES), lambda i, j: (i * nt + j, 0)),
            pl.BlockSpec((None, 1, LANES), lambda i, j: (i * nt + j, 0, 0)),
        ],
        out_shape=[jax.ShapeDtypeStruct((b, l, d), F32), jax.ShapeDtypeStruct((b * l, d), BF16),
                   jax.ShapeDtypeStruct((b * l, LANES), F32),
                   jax.ShapeDtypeStruct((b * nt, 1, LANES), F32)],
        compiler_params=_cparams(("arbitrary", "arbitrary"), VMEM_LIMIT_BYTES),
        name="lru_out_proj_route",
    )(gate, hf, hb, w_out, x, mod, ln_g, ln_b, router_padded)


SEG_ALIGN = SUBLANES
MAX_SEG_PAD = N_EXPERTS * (SEG_ALIGN - 1)
BF16_ROWS = 2 * SUBLANES


def _window_rows(window):
    return -(-(TOP_K * window + MAX_SEG_PAD) // BF16_ROWS) * BF16_ROWS


def _for_each_piece(seg_refs, w, window, fn):
    loff_ref, goff_ref, cpad_ref = seg_refs
    for e in range(N_EXPERTS):
        lo, go, c = (ref[w * N_EXPERTS + e] for ref in (loff_ref, goff_ref, cpad_ref))
        size = window
        while size >= SEG_ALIGN:
            done = c & ~(2 * size - 1)

            @pl.when((c & size) != 0)
            def _(lo=lo, go=go, done=done, size=size):
                fn(pl.multiple_of(lo + done, SEG_ALIGN), pl.multiple_of(go + done, SEG_ALIGN), size)

            size //= 2


def _for_each_tail_piece(tail_ref, len_ref, max_size, fn):
    for e in range(N_EXPERTS):
        go, c = tail_ref[e], len_ref[e]
        size = max_size
        while size >= SEG_ALIGN:
            done = c & ~(2 * size - 1)

            @pl.when((c & size) != 0)
            def _(go=go, done=done, size=size):
                fn(pl.multiple_of(go + done, SEG_ALIGN), size)

            size //= 2


def _dispatch_kernel(loff_ref, goff_ref, cpad_ref, tail_ref, tlen_ref, rest_ref, dloc_ref, wts_ref,
                     h_ref, xs_hbm, buf, zbuf, sems):
    w, nw = pl.program_id(0), pl.num_programs(0)
    slot = w % 2
    window, d = h_ref.shape
    r = buf.shape[1]
    zrows = zbuf.shape[0]
    segs = (loff_ref, goff_ref, cpad_ref)

    def copy(buf_slot, lrow, grow, size):
        return pltpu.make_async_copy(buf.at[buf_slot, pl.ds(lrow, size)],
                                     xs_hbm.at[pl.ds(grow, size)], sems.at[buf_slot])

    def fill(grow, size):
        return pltpu.make_async_copy(zbuf.at[pl.ds(0, size)], xs_hbm.at[pl.ds(grow, size)],
                                     sems.at[2])

    def for_each_fill(act):
        _for_each_tail_piece(tail_ref, tlen_ref, zrows, lambda g, s: act(fill(g, s)))

        def body(k, _):
            act(fill(pl.multiple_of(rest_ref[0] + k * zrows, SEG_ALIGN), zrows))
            return 0
        lax.fori_loop(0, (xs_hbm.shape[0] - rest_ref[0]) // zrows, body, 0)

    @pl.when(w == 0)
    def _():
        zbuf[...] = jnp.zeros_like(zbuf)
        for_each_fill(lambda cp: cp.start())

    row = lax.broadcasted_iota(jnp.int32, (r, window), 0)
    hit0 = row == dloc_ref[0:1, :]
    hit1 = row == dloc_ref[1:2, :]
    perm = jnp.where(hit0 | hit1, 1.0, 0.0).astype(BF16)
    buf[slot, :, :d] = jnp.dot(perm, h_ref[...], preferred_element_type=F32)
    roww = jnp.sum(jnp.where(hit0, wts_ref[0:1, :], 0.0) + jnp.where(hit1, wts_ref[1:2, :], 0.0),
                   axis=1, keepdims=True)
    buf[slot, :, d:] = jnp.broadcast_to(roww, (r, LANES))

    _for_each_piece(segs, w, window, lambda l, g, s: copy(slot, l, g, s).start())

    @pl.when(w > 0)
    def _():
        _for_each_piece(segs, w - 1, window, lambda l, g, s: copy(1 - slot, l, g, s).wait())

    @pl.when(w == nw - 1)
    def _():
        _for_each_piece(segs, w, window, lambda l, g, s: copy(slot, l, g, s).wait())
        for_each_fill(lambda cp: cp.wait())


def _dispatch_call(plan, h, window, rows, n_rows):
    n, d = h.shape
    nw = n // window
    r = _window_rows(window)
    width = d + LANES
    grid_spec = pltpu.PrefetchScalarGridSpec(
        num_scalar_prefetch=6,
        grid=(nw,),
        in_specs=[
            pl.BlockSpec((None, TOP_K, window), lambda w, *_: (w, 0, 0)),
            pl.BlockSpec((None, TOP_K, window), lambda w, *_: (w, 0, 0)),
            pl.BlockSpec((window, d), lambda w, *_: (w, 0)),
        ],
        out_specs=pl.BlockSpec(memory_space=pl.ANY),
        scratch_shapes=[pltpu.VMEM((2, r, width), F32), pltpu.VMEM((rows // 2, width), F32),
                        pltpu.SemaphoreType.DMA((3,))],
    )
    return pl.pallas_call(
        _dispatch_kernel,
        grid_spec=grid_spec,
        out_shape=jax.ShapeDtypeStruct((n_rows, width), F32),
        compiler_params=_cparams(("arbitrary",), VMEM_LIMIT_BYTES),
        name="expert_dispatch",
    )(plan["loff"], plan["goff"], plan["cpad"], plan["tail"], plan["tail_len"], plan["rest"],
      plan["dloc_rows"], plan["wts_rows"], h)


def _expert_kernel(te_ref, nv_ref, xs_ref, wg_ref, wu_ref, wd_ref, y_ref, hb_scr, acc_scr):
    i, f, nf = pl.program_id(0), pl.program_id(1), pl.num_programs(1)
    d = y_ref.shape[-1]
    valid = i < nv_ref[0]

    @pl.when(valid & (f == 0))
    def _():
        hb_scr[...] = xs_ref[:, :d].astype(BF16)
        acc_scr[...] = jnp.zeros_like(acc_scr)

    @pl.when(valid)
    def _():
        acc_scr[...] += _swiglu_chunk(hb_scr[...], wg_ref[...].astype(BF16),
                                      wu_ref[...].astype(BF16), wd_ref[...].astype(BF16))

    @pl.when(valid & (f == nf - 1))
    def _():
        y_ref[...] = acc_scr[...] * xs_ref[:, d:d + 1]

    @pl.when(jnp.logical_not(valid) & (f == nf - 1))
    def _():
        y_ref[...] = jnp.zeros_like(y_ref)


def _expert_call(plan, xs, w_gu, w_dn, rows, cols):
    n_rows, width = xs.shape
    d = width - LANES
    nt = n_rows // rows
    ff = w_dn.shape[1]
    fc = cols
    nf = ff // fc
    tile = lambda i, nv: jnp.minimum(i, nv[0] - 1)
    chunk = lambda i, f, nv: jnp.where(i < nv[0], f, nf - 1)
    grid_spec = pltpu.PrefetchScalarGridSpec(
        num_scalar_prefetch=2,
        grid=(nt, nf),
        in_specs=[
            pl.BlockSpec((rows, width), lambda i, f, te, nv: (tile(i, nv), 0)),
            pl.BlockSpec((None, d, fc), lambda i, f, te, nv: (te[tile(i, nv)], 0, chunk(i, f, nv))),
            pl.BlockSpec((None, d, fc),
                         lambda i, f, te, nv: (te[tile(i, nv)], 0, nf + chunk(i, f, nv))),
            pl.BlockSpec((None, fc, d), lambda i, f, te, nv: (te[tile(i, nv)], chunk(i, f, nv), 0)),
        ],
        out_specs=pl.BlockSpec((rows, d), lambda i, f, te, nv: (i, 0)),
        scratch_shapes=[pltpu.VMEM((rows, d), BF16), pltpu.VMEM((rows, d), F32)],
    )
    return pl.pallas_call(
        _expert_kernel,
        grid_spec=grid_spec,
        out_shape=jax.ShapeDtypeStruct((n_rows, d), F32),
        compiler_params=_cparams(("arbitrary", "arbitrary"), VMEM_LIMIT_BYTES),
        name="expert_swiglu",
    )(plan["tile_expert"], plan["n_valid"], xs, w_gu, w_gu, w_dn)


def _combine_kernel(loff_ref, goff_ref, cpad_ref, dcol_ref, y_hbm, x_ref, mod_ref, g_ref, b_ref,
                    out_ref, ybuf, sems):
    w, nw = pl.program_id(0), pl.num_programs(0)
    slot = w % 2
    window, d = x_ref.shape
    r = ybuf.shape[1]
    segs = (loff_ref, goff_ref, cpad_ref)

    def copy(buf_slot, lrow, grow, size):
        return pltpu.make_async_copy(y_hbm.at[pl.ds(grow, size)],
                                     ybuf.at[buf_slot, pl.ds(lrow, size)], sems.at[buf_slot])

    @pl.when(w == 0)
    def _():
        ybuf[...] = jnp.zeros_like(ybuf)
        _for_each_piece(segs, 0, window, lambda l, g, s: copy(0, l, g, s).start())

    @pl.when(w + 1 < nw)
    def _():
        _for_each_piece(segs, w + 1, window, lambda l, g, s: copy(1 - slot, l, g, s).start())

    _for_each_piece(segs, w, window, lambda l, g, s: copy(slot, l, g, s).wait())

    lane = lax.broadcasted_iota(jnp.int32, (window, r), 1)
    hit = (lane == dcol_ref[:, 0:1]) | (lane == dcol_ref[:, 1:2])
    perm = jnp.where(hit, 1.0, 0.0).astype(BF16)
    hi, lo = _split_bf16(ybuf[slot])
    f = (jnp.dot(perm, hi, preferred_element_type=F32)
         + jnp.dot(perm, lo, preferred_element_type=F32))
    out_ref[...] = _layer_norm(ALPHA * x_ref[...] + _mod_slice(mod_ref, 5, d) * f,
                               g_ref[...], b_ref[...])


def _combine_call(plan, y, x, mod, ln_g, ln_b, window):
    b, l, d = x.shape
    wpb = l // window
    r = _window_rows(window)
    grid_spec = pltpu.PrefetchScalarGridSpec(
        num_scalar_prefetch=3,
        grid=(b * wpb,),
        in_specs=[
            pl.BlockSpec((window, TOP_K), lambda w, *_: (w, 0)),
            pl.BlockSpec(memory_space=pl.ANY),
            pl.BlockSpec((None, window, d), lambda w, *_: (w // wpb, w % wpb, 0)),
            pl.BlockSpec((None, 1, 6 * d), lambda w, *_: (w // wpb, 0, 0)),
            pl.BlockSpec((1, d), lambda w, *_: (0, 0)),
            pl.BlockSpec((1, d), lambda w, *_: (0, 0)),
        ],
        out_specs=pl.BlockSpec((None, window, d), lambda w, *_: (w // wpb, w % wpb, 0)),
        scratch_shapes=[pltpu.VMEM((2, r, d), F32), pltpu.SemaphoreType.DMA((2,))],
    )
    return pl.pallas_call(
        _combine_kernel,
        grid_spec=grid_spec,
        out_shape=jax.ShapeDtypeStruct((b, l, d), F32),
        compiler_params=_cparams(("arbitrary",), VMEM_LIMIT_BYTES),
        name="expert_combine",
    )(plan["loff"], plan["goff"], plan["cpad"], plan["dloc_cols"], y, x, mod, ln_g, ln_b)


def _dispatch_plan(info, counts, window, rows):
    n = info.shape[0]
    nw = n // window
    as_int = lambda lane0: info[:, lane0:lane0 + TOP_K].astype(jnp.int32).reshape(nw, window, TOP_K)
    experts, rank = as_int(ROUTE_IDX_LANE), as_int(ROUTE_RANK_LANE)
    count = counts[:, 0, :N_EXPERTS].astype(jnp.int32)
    cpad = -(-count // SEG_ALIGN) * SEG_ALIGN
    loff = jnp.cumsum(cpad, axis=1) - cpad
    group = -(-jnp.sum(cpad, axis=0) // rows) * rows
    gend = jnp.cumsum(group)
    goff = (gend - group)[None, :] + jnp.cumsum(cpad, axis=0) - cpad
    is_e = experts[..., None] == jnp.arange(N_EXPERTS)
    dloc = jnp.sum(jnp.where(is_e, loff[:, None, None, :], 0), axis=-1) + rank
    n_tiles = -(-(TOP_K * n + nw * MAX_SEG_PAD) // rows) + N_EXPERTS
    tile_start = jnp.arange(n_tiles) * rows
    tile_expert = jnp.minimum(jnp.sum(tile_start[:, None] >= gend[None, :], axis=1), N_EXPERTS - 1)
    wts = info[:, ROUTE_W_LANE:ROUTE_W_LANE + TOP_K].reshape(nw, window, TOP_K)
    flat = lambda a: a.reshape(-1).astype(jnp.int32)
    return dict(
        loff=flat(loff), goff=flat(goff), cpad=flat(cpad),
        dloc_rows=jnp.transpose(dloc, (0, 2, 1)).astype(jnp.int32),
        dloc_cols=dloc.reshape(n, TOP_K).astype(jnp.int32),
        wts_rows=jnp.transpose(wts, (0, 2, 1)),
        tail=flat(goff[-1] + cpad[-1]), tail_len=flat(gend - goff[-1] - cpad[-1]),
        rest=flat(gend[-1:]),
        tile_expert=tile_expert.astype(jnp.int32),
        n_valid=(gend[-1] // rows).reshape(1).astype(jnp.int32),
        n_rows=n_tiles * rows,
    )


def _attention_layer(x, ctx, mod_x, mod_c, ln_g, ln_b, w_qkv, w_o, sink, w_gu, w_dn, grid_w, ts):
    b, l, d = x.shape
    lc = ctx.shape[1]
    g0, b0, g1, b1 = ln_g[0:1], ln_b[0:1], ln_g[1:2], ln_b[1:2]
    w_qkv, w_o, w_gu, w_dn = (w.astype(BF16) for w in (w_qkv, w_o, w_gu, w_dn))
    ctx_flat = ctx.reshape(1, b * lc, d)
    n_kv = N_KV_HEADS * HEAD_DIM

    q, k, v = _qkv_call(x, mod_x, w_qkv, _rope_tables(l, grid_w), ts["qkv_rows"])
    qc, kc, vc = _qkv_call(ctx_flat, mod_c, w_qkv, _no_rope_tables(b * lc), ts["qkv_rows"])
    qc = qc.reshape(b, lc, -1)
    kc, vc = kc.reshape(b, lc, n_kv), vc.reshape(b, lc, n_kv)
    mod_cb = jnp.broadcast_to(mod_c, (b,) + mod_c.shape[1:])

    x = _attn_call(q, k, v, kc, vc, sink, w_o, x, mod_x, g0, b0, band=True)
    ctx = _attn_call(qc, None, None, kc, vc, sink, w_o, ctx, mod_cb, g0, b0, band=False)

    x = _ffn_call(x, mod_x, w_gu, w_dn, g1, b1, ts["ffn_rows"], ts["ffn_cols"])
    ctx = _ffn_call(ctx.reshape(1, b * lc, d), mod_c, w_gu, w_dn, g1, b1,
                    ts["ffn_rows"], ts["ffn_cols"]).reshape(b, lc, d)
    return x, ctx


def _recurrent_layer(x, ctx, mod_x, mod_c, ln_g, ln_b, w_in, conv_w, conv_b, lam, w_r, b_r, w_i,
                     b_i, w_out, router, w_gu, w_dn, ts):
    b, l, d = x.shape
    g0, b0, g1, b1 = ln_g[0:1], ln_b[0:1], ln_g[1:2], ln_b[1:2]
    w_in, w_out = w_in.astype(BF16), w_out.astype(BF16)
    wd = _paired_block_diag(w_r, w_i)
    conv_b = conv_b.reshape(1, -1)
    mod_cb = jnp.broadcast_to(mod_c, (b,) + mod_c.shape[1:])

    r = w_out.shape[0]
    gate, xb = _inproj_call(x, mod_x, w_in, ts["proj_rows"])
    _, xb_c = _inproj_call(ctx, mod_cb, w_in, ts["proj_rows"])
    zero = jnp.zeros((b, r), F32)
    scan = functools.partial(_scan_call, conv_w=conv_w, conv_b=conv_b, wd=wd, b_r=0.5 * b_r,
                             b_i=0.5 * b_i, lam=lam, steps=ts["scan_steps"])
    _, _, cf, cb = scan(xb_c.reshape(-1, b, r), zero, zero)
    hf, hb, _, _ = scan(xb.reshape(l, b, r), cf, cb)
    router_padded = jnp.concatenate(
        _split_bf16(jnp.pad(router, ((0, 0), (0, LANES - N_EXPERTS)))), axis=1)
    x, h, info, counts = _outproj_call(gate, hf.reshape(l, b * r), hb.reshape(l, b * r), w_out, x,
                                       mod_x, g0, b0, router_padded, ts["moe_window"])
    plan = _dispatch_plan(info, counts, ts["moe_window"], ts["moe_rows"])
    xs = _dispatch_call(plan, h, ts["moe_window"], ts["moe_rows"], plan["n_rows"])
    y = _expert_call(plan, xs, w_gu, w_dn, ts["moe_rows"], ts["ffn_cols"])
    return _combine_call(plan, y, x, mod_x, g1, b1, ts["moe_window"])


def kernel(x, c, ctx, c_ctx, w_mod, b_mod, ln_g, ln_b, attn_w_qkv, attn_w_o, attn_sink, ffn_w_gu,
           ffn_w_dn, lru_w_in, lru_conv_w, lru_conv_b, lru_lambda, lru_w_r, lru_b_r, lru_w_i,
           lru_b_i, lru_w_out, moe_router, moe_w_gu, moe_w_dn):
    b, l, d = x.shape
    assert w_mod.shape[0] == DEPTH and l % BLOCK == 0 and ctx.shape[1] % BLOCK == 0
    grid_w = 64
    ts = _tile_sizes()

    rows = -(-(b + 1) // SUBLANES) * SUBLANES
    cond = jnp.zeros((rows, d), F32).at[:b].set(c).at[b].set(c_ctx)
    mod = _mod_call(cond, w_mod, b_mod[:, None, :])
    mod_x = lambda i: mod[i, :b, None, :]
    mod_c = lambda i: mod[i, b:b + 1, None, :]

    x, ctx = _attention_layer(x, ctx, mod_x(0), mod_c(0), ln_g[0], ln_b[0], attn_w_qkv[0],
                              attn_w_o[0], attn_sink[0], ffn_w_gu[0], ffn_w_dn[0], grid_w, ts)
    return _recurrent_layer(x, ctx, mod_x(1), mod_c(1), ln_g[1], ln_b[1], lru_w_in[0],
                            lru_conv_w[0], lru_conv_b[0], lru_lambda[0], lru_w_r[0], lru_b_r[0],
                            lru_w_i[0], lru_b_i[0], lru_w_out[0], moe_router[0], moe_w_gu[0],
                            moe_w_dn[0], ts)
```

```python
import functools

import jax
import jax.numpy as jnp
import numpy as np
from jax import lax
from jax.experimental import pallas as pl
from jax.experimental.pallas import tpu as pltpu

N_HEADS = 8
N_KV_HEADS = 2
GROUP = N_HEADS // N_KV_HEADS
HEAD_DIM = 128
BLOCK = 128
AXIS_DIM = HEAD_DIM // 2
ROPE_BASE = 10000.0
LRU_BLOCKS = 8
LRU_BLOCK_DIM = 128
LRU_C = 8.0
N_EXPERTS = 8
TOP_K = 2
DEPTH = 2
ALPHA = (2 * DEPTH) ** 0.25
LN_EPS = 1e-5
NEG_INF = -1e30

LANES = 128
SUBLANES = 8
VMEM_LIMIT_BYTES = 56 * 1024 * 1024

F32 = jnp.float32
BF16 = jnp.bfloat16


def _tile_sizes():
    return dict(
        qkv_rows=512,
        ffn_rows=1024,
        ffn_cols=512,
        proj_rows=512,
        scan_steps=64,
        moe_window=512,
        moe_rows=1024,
    )


def _cparams(sem, vmem=None):
    return pltpu.CompilerParams(dimension_semantics=sem, vmem_limit_bytes=vmem)


def _layer_norm(h, g, b):
    mu = jnp.mean(h, axis=-1, keepdims=True)
    d = h - mu
    var = jnp.mean(d * d, axis=-1, keepdims=True)
    return d * lax.rsqrt(var + LN_EPS) * g + b


def _mod_slice(mod_ref, k, d):
    return mod_ref[:, k * d:(k + 1) * d]


def _mod_kernel(c_ref, w_ref, b_ref, o_ref):
    c = c_ref[...]
    s = c * jax.nn.sigmoid(c)
    o_ref[...] = jnp.dot(s, w_ref[...], preferred_element_type=F32,
                         precision=lax.Precision.HIGHEST) + b_ref[...]


def _mod_call(cond, w_mod, b_mod):
    rows, d = cond.shape
    depth, _, n = w_mod.shape
    tn = 1536
    return pl.pallas_call(
        _mod_kernel,
        grid=(depth, n // tn),
        in_specs=[
            pl.BlockSpec((rows, d), lambda i, j: (0, 0)),
            pl.BlockSpec((None, d, tn), lambda i, j: (i, 0, j)),
            pl.BlockSpec((None, 1, tn), lambda i, j: (i, 0, j)),
        ],
        out_specs=pl.BlockSpec((None, rows, tn), lambda i, j: (i, 0, j)),
        out_shape=jax.ShapeDtypeStruct((depth, rows, n), F32),
        compiler_params=_cparams(("arbitrary", "arbitrary")),
        name="mod_vectors",
    )(cond, w_mod, b_mod)


def _qkv_kernel(x_ref, mod_ref, w_ref, cos_ref, sa_ref, sb_ref, q_ref, k_ref, v_ref):
    d = x_ref.shape[-1]
    h = x_ref[...] * (1.0 + _mod_slice(mod_ref, 1, d)) + _mod_slice(mod_ref, 0, d)
    qkv = jnp.dot(h.astype(BF16), w_ref[...], preferred_element_type=F32)
    cos, sa, sb = cos_ref[...], sa_ref[...], sb_ref[...]
    scale = HEAD_DIM ** -0.5
    for head in range(N_HEADS + N_KV_HEADS):
        u = qkv[:, head * HEAD_DIM:(head + 1) * HEAD_DIM]
        r = (u * cos + pltpu.roll(u, HEAD_DIM - AXIS_DIM // 2, axis=1) * sa
             + pltpu.roll(u, AXIS_DIM // 2, axis=1) * sb)
        if head < N_HEADS:
            q_ref[:, head * HEAD_DIM:(head + 1) * HEAD_DIM] = (r * scale).astype(BF16)
        else:
            kh = head - N_HEADS
            k_ref[:, kh * HEAD_DIM:(kh + 1) * HEAD_DIM] = r.astype(BF16)
    kv = N_KV_HEADS * HEAD_DIM
    v_ref[...] = qkv[:, N_HEADS * HEAD_DIM + kv:].astype(BF16)


def _qkv_call(x, mod, w_qkv, tabs, rows):
    b, l, d = x.shape
    n_q, n_kv = N_HEADS * HEAD_DIM, N_KV_HEADS * HEAD_DIM
    tm = min(rows, l)
    tab_spec = pl.BlockSpec((tm, HEAD_DIM), lambda i, j: (j, 0))
    return pl.pallas_call(
        _qkv_kernel,
        grid=(b, l // tm),
        in_specs=[
            pl.BlockSpec((None, tm, d), lambda i, j: (i, j, 0)),
            pl.BlockSpec((None, 1, 6 * d), lambda i, j: (i, 0, 0)),
            pl.BlockSpec(w_qkv.shape, lambda i, j: (0, 0)),
            tab_spec, tab_spec, tab_spec,
        ],
        out_specs=[
            pl.BlockSpec((None, tm, n_q), lambda i, j: (i, j, 0)),
            pl.BlockSpec((None, tm, n_kv), lambda i, j: (i, j, 0)),
            pl.BlockSpec((None, tm, n_kv), lambda i, j: (i, j, 0)),
        ],
        out_shape=[
            jax.ShapeDtypeStruct((b, l, n_q), BF16),
            jax.ShapeDtypeStruct((b, l, n_kv), BF16),
            jax.ShapeDtypeStruct((b, l, n_kv), BF16),
        ],
        compiler_params=_cparams(("arbitrary", "arbitrary"), VMEM_LIMIT_BYTES),
        name="qkv_rope",
    )(x, mod, w_qkv, *tabs)


def _rope_tables(l, grid_w):
    f32 = np.float32
    t = np.arange(l)
    pos = np.stack([(t // grid_w).astype(f32), (t % grid_w).astype(f32)], axis=1)
    freqs = f32(ROPE_BASE) ** (-np.arange(0, AXIS_DIM, 2, dtype=f32) / f32(AXIS_DIM))
    lane = np.arange(HEAD_DIM)
    ang = (pos[:, lane // AXIS_DIM] * freqs[lane % (AXIS_DIM // 2)][None, :]).astype(f32)
    first = (lane % AXIS_DIM) < AXIS_DIM // 2
    cos, sin = np.cos(ang).astype(f32), np.sin(ang).astype(f32)
    zero = f32(0.0)
    return cos, np.where(first, -sin, zero), np.where(first, zero, sin)


def _no_rope_tables(l):
    z = np.zeros((l, HEAD_DIM), np.float32)
    return np.ones((l, HEAD_DIM), np.float32), z, z


def _nt_dot(a, b):
    return lax.dot_general(a, b, (((1,), (1,)), ((), ())), preferred_element_type=F32)


def _attend(qs, segs, sink_col):
    k_all = jnp.concatenate([k for k, _, _ in segs], axis=0)
    v_all = jnp.concatenate([v for _, v, _ in segs], axis=0)
    v_ext = jnp.concatenate([v_all, jnp.ones_like(v_all)], axis=1)
    s_all = _nt_dot(qs, k_all)
    tiles, col = [], 0
    for k, _, mask in segs:
        for c in range(col, col + k.shape[0], LANES):
            s = s_all[:, c:c + LANES]
            tiles.append(s if mask is None else jnp.where(mask, s, NEG_INF))
        col += k.shape[0]
    m_tile = tiles[0]
    for s in tiles[1:]:
        m_tile = jnp.maximum(m_tile, s)
    m = jnp.maximum(sink_col, jnp.max(m_tile, axis=-1, keepdims=True))
    p_all = jnp.concatenate([jnp.exp(s - m).astype(BF16) for s in tiles], axis=1)
    res = jnp.dot(p_all, v_ext, preferred_element_type=F32)
    hd = v_all.shape[1]
    return res[:, :hd] / (res[:, hd:hd + 1] + jnp.exp(sink_col - m))


def _attn_epilogue(o_scr, wo_ref, x_ref, mod_ref, g_ref, b_ref, out_ref):
    d = x_ref.shape[-1]
    proj = jnp.dot(o_scr[...], wo_ref[...], preferred_element_type=F32)
    out_ref[...] = _layer_norm(ALPHA * x_ref[...] + _mod_slice(mod_ref, 2, d) * proj,
                               g_ref[...], b_ref[...])


def _sink_column(sink_ref, kvh):
    return jnp.concatenate(
        [jnp.full((BLOCK, 1), sink_ref[kvh * GROUP + g], F32) for g in range(GROUP)], axis=0)


def _stack_group(q_ref, kvh):
    return jnp.concatenate(
        [q_ref[:, (kvh * GROUP + g) * HEAD_DIM:(kvh * GROUP + g + 1) * HEAD_DIM]
         for g in range(GROUP)], axis=0)


def _unstack_group(o, o_scr, kvh):
    for g in range(GROUP):
        col = (kvh * GROUP + g) * HEAD_DIM
        o_scr[:, col:col + HEAD_DIM] = o[g * BLOCK:(g + 1) * BLOCK].astype(BF16)


def _band_attn_kernel(sink_ref, q_ref, kp_ref, kc_ref, kn_ref, vp_ref, vc_ref, vn_ref,
                      kx_ref, vx_ref, wo_ref, x_ref, mod_ref, g_ref, b_ref, out_ref, o_scr):
    n, nb = pl.program_id(1), pl.num_programs(1)
    row = lax.broadcasted_iota(jnp.int32, (GROUP * BLOCK, BLOCK), 0) % BLOCK
    col = lax.broadcasted_iota(jnp.int32, (GROUP * BLOCK, BLOCK), 1)
    mask_prev = (col >= row) & (n > 0)
    mask_next = (col <= row) & (n < nb - 1)
    for kvh in range(N_KV_HEADS):
        hs = slice(kvh * HEAD_DIM, (kvh + 1) * HEAD_DIM)
        segs = [(kp_ref[:, hs], vp_ref[:, hs], mask_prev),
                (kc_ref[:, hs], vc_ref[:, hs], None),
                (kn_ref[:, hs], vn_ref[:, hs], mask_next),
                (kx_ref[:, hs], vx_ref[:, hs], None)]
        o = _attend(_stack_group(q_ref, kvh), segs, _sink_column(sink_ref, kvh))
        _unstack_group(o, o_scr, kvh)
    _attn_epilogue(o_scr, wo_ref, x_ref, mod_ref, g_ref, b_ref, out_ref)


def _ctx_attn_kernel(sink_ref, q_ref, kx_ref, vx_ref, wo_ref, x_ref, mod_ref, g_ref, b_ref,
                     out_ref, o_scr):
    for kvh in range(N_KV_HEADS):
        hs = slice(kvh * HEAD_DIM, (kvh + 1) * HEAD_DIM)
        o = _attend(_stack_group(q_ref, kvh), [(kx_ref[:, hs], vx_ref[:, hs], None)],
                    _sink_column(sink_ref, kvh))
        _unstack_group(o, o_scr, kvh)
    _attn_epilogue(o_scr, wo_ref, x_ref, mod_ref, g_ref, b_ref, out_ref)


def _attn_call(q, k, v, kx, vx, sink, w_o, x, mod, ln_g, ln_b, band):
    b, l, d = x.shape
    lc = kx.shape[1]
    nb = l // BLOCK
    n_q, n_kv = N_HEADS * HEAD_DIM, N_KV_HEADS * HEAD_DIM
    kv_blk = lambda f: pl.BlockSpec((None, BLOCK, n_kv), f)
    common_in = [
        pl.BlockSpec((None, lc, n_kv), lambda i, j: (i, 0, 0)),
        pl.BlockSpec((None, lc, n_kv), lambda i, j: (i, 0, 0)),
        pl.BlockSpec(w_o.shape, lambda i, j: (0, 0)),
        pl.BlockSpec((None, BLOCK, d), lambda i, j: (i, j, 0)),
        pl.BlockSpec((None, 1, 6 * d), lambda i, j: (i, 0, 0)),
        pl.BlockSpec((1, d), lambda i, j: (0, 0)),
        pl.BlockSpec((1, d), lambda i, j: (0, 0)),
    ]
    head_in = [pl.BlockSpec(memory_space=pltpu.SMEM),
               pl.BlockSpec((None, BLOCK, n_q), lambda i, j: (i, j, 0))]
    if band:
        prev = lambda i, j: (i, jnp.maximum(j - 1, 0), 0)
        cur = lambda i, j: (i, j, 0)
        nxt = lambda i, j: (i, jnp.minimum(j + 1, nb - 1), 0)
        in_specs = head_in + [kv_blk(prev), kv_blk(cur), kv_blk(nxt)] * 2 + common_in
        args = (sink, q, k, k, k, v, v, v, kx, vx, w_o, x, mod, ln_g, ln_b)
        body, name = _band_attn_kernel, "band_attention"
    else:
        in_specs = head_in + common_in
        args = (sink, q, kx, vx, w_o, x, mod, ln_g, ln_b)
        body, name = _ctx_attn_kernel, "context_attention"
    return pl.pallas_call(
        body,
        grid=(b, nb),
        in_specs=in_specs,
        out_specs=pl.BlockSpec((None, BLOCK, d), lambda i, j: (i, j, 0)),
        out_shape=jax.ShapeDtypeStruct((b, l, d), F32),
        scratch_shapes=[pltpu.VMEM((BLOCK, n_q), BF16)],
        compiler_params=_cparams(("arbitrary", "arbitrary"), VMEM_LIMIT_BYTES),
        name=name,
    )(*args)


def _swiglu_chunk(hb, wg, wu, wd):
    g = jnp.dot(hb, wg, preferred_element_type=F32)
    u = jnp.dot(hb, wu, preferred_element_type=F32)
    a = (g * jax.nn.sigmoid(g)) * u
    return jnp.dot(a.astype(BF16), wd, preferred_element_type=F32)


def _ffn_kernel(x_ref, mod_ref, wg_ref, wu_ref, wd_ref, g_ref, b_ref, out_ref, hb_scr, acc_scr):
    f, nf = pl.program_id(2), pl.num_programs(2)
    d = x_ref.shape[-1]

    @pl.when(f == 0)
    def _():
        h = x_ref[...] * (1.0 + _mod_slice(mod_ref, 4, d)) + _mod_slice(mod_ref, 3, d)
        hb_scr[...] = h.astype(BF16)
        acc_scr[...] = jnp.zeros_like(acc_scr)

    acc_scr[...] += _swiglu_chunk(hb_scr[...], wg_ref[...], wu_ref[...], wd_ref[...])

    @pl.when(f == nf - 1)
    def _():
        out_ref[...] = _layer_norm(ALPHA * x_ref[...] + _mod_slice(mod_ref, 5, d) * acc_scr[...],
                                   g_ref[...], b_ref[...])


def _ffn_call(x, mod, w_gu, w_dn, ln_g, ln_b, rows, cols):
    b, l, d = x.shape
    ff = w_dn.shape[0]
    tm, fc = min(rows, l), cols
    nf = ff // fc
    return pl.pallas_call(
        _ffn_kernel,
        grid=(b, l // tm, nf),
        in_specs=[
            pl.BlockSpec((None, tm, d), lambda i, j, f: (i, j, 0)),
            pl.BlockSpec((None, 1, 6 * d), lambda i, j, f: (i, 0, 0)),
            pl.BlockSpec((d, fc), lambda i, j, f: (0, f)),
            pl.BlockSpec((d, fc), lambda i, j, f: (0, nf + f)),
            pl.BlockSpec((fc, d), lambda i, j, f: (f, 0)),
            pl.BlockSpec((1, d), lambda i, j, f: (0, 0)),
            pl.BlockSpec((1, d), lambda i, j, f: (0, 0)),
        ],
        out_specs=pl.BlockSpec((None, tm, d), lambda i, j, f: (i, j, 0)),
        out_shape=jax.ShapeDtypeStruct((b, l, d), F32),
        scratch_shapes=[pltpu.VMEM((tm, d), BF16), pltpu.VMEM((tm, d), F32)],
        compiler_params=_cparams(("arbitrary", "arbitrary", "arbitrary"), VMEM_LIMIT_BYTES),
        name="dense_swiglu",
    )(x, mod, w_gu, w_gu, w_dn, ln_g, ln_b)


LRU_GROUPS = 2


def _inproj_kernel(x_ref, mod_ref, w_ref, gate_ref, *xb_refs):
    d = x_ref.shape[-1]
    r = gate_ref.shape[-1]
    rg = r // len(xb_refs)
    h = x_ref[...] * (1.0 + _mod_slice(mod_ref, 1, d)) + _mod_slice(mod_ref, 0, d)
    y = jnp.dot(h.astype(BF16), w_ref[...], preferred_element_type=F32)
    gate_ref[...] = y[:, :r]
    for g, xb_ref in enumerate(xb_refs):
        xb_ref[...] = y[:, r + g * rg:r + (g + 1) * rg]


def _inproj_call(x, mod, w_in, rows):
    b, l, d = x.shape
    r = w_in.shape[1] // 2
    rg = r // LRU_GROUPS
    tm = min(rows, l)
    return pl.pallas_call(
        _inproj_kernel,
        grid=(b, l // tm),
        in_specs=[
            pl.BlockSpec((None, tm, d), lambda i, j: (i, j, 0)),
            pl.BlockSpec((None, 1, 6 * d), lambda i, j: (i, 0, 0)),
            pl.BlockSpec(w_in.shape, lambda i, j: (0, 0)),
        ],
        out_specs=[pl.BlockSpec((None, tm, r), lambda i, j: (i, j, 0))]
        + [pl.BlockSpec((tm, rg), lambda i, j: (j, i))] * LRU_GROUPS,
        out_shape=[jax.ShapeDtypeStruct((b, l, r), F32)]
        + [jax.ShapeDtypeStruct((l, b * rg), F32)] * LRU_GROUPS,
        compiler_params=_cparams(("arbitrary", "arbitrary"), VMEM_LIMIT_BYTES),
        name="lru_in_proj",
    )(x, mod, w_in)


def _lru_coeffs(main_ref, prev_ref, next_ref, first, last, cw_ref, cb_ref, wd_ref, br_ref, bi_ref,
                lam_ref, a_scr, b_scr):
    tl, nb, r = main_ref.shape
    prev = jnp.where(first, 0.0, prev_ref[...])
    nxt = jnp.where(last, 0.0, next_ref[...])
    ext = jnp.concatenate([prev, main_ref[...], nxt], axis=0)
    u = cb_ref[...][None]
    for k in range(4):
        u = u + cw_ref[k:k + 1, :][None] * ext[k:k + tl]
    u = u.reshape(tl * nb, r)
    ub = u.astype(BF16)
    lam = lam_ref[...]
    softplus_neg = jnp.maximum(-lam, 0.0) + jnp.log1p(jnp.exp(-jnp.abs(lam)))
    half_rate = (-0.5 * LRU_C) * softplus_neg
    pw = 2 * LRU_BLOCK_DIM
    for p in range(r // pw):
        cs = slice(p * pw, (p + 1) * pw)
        z = jnp.dot(ub[:, cs], wd_ref[p], preferred_element_type=F32)
        tr = jnp.tanh(z[:, :pw] + br_ref[:, cs])
        ig = 0.5 + 0.5 * jnp.tanh(z[:, pw:] + bi_ref[:, cs])
        log_a = half_rate[:, cs] * tr + half_rate[:, cs]
        a = jnp.exp(log_a)
        gain = jnp.sqrt(-jnp.tanh(log_a) * (a * a + 1.0))
        a_scr[:, :, cs] = a.reshape(tl, nb, pw)
        b_scr[:, :, cs] = (gain * (ig * u[:, cs])).reshape(tl, nb, pw)


def _scan_kernel(fm_ref, fp_ref, fn_ref, bm_ref, bp_ref, bn_ref, h0f_ref, h0b_ref,
                 cw_ref, cb_ref, wdf_ref, wdb_ref, br_ref, bi_ref, lam_ref,
                 hf_ref, hb_ref, lastf_ref, lastb_ref,
                 af_scr, bf_scr, ab_scr, bb_scr, sf_scr, sb_scr):
    c, nc = pl.program_id(0), pl.num_programs(0)
    tl = fm_ref.shape[0]

    @pl.when(c == 0)
    def _():
        sf_scr[...] = h0f_ref[...]
        sb_scr[...] = h0b_ref[...]

    _lru_coeffs(fm_ref, fp_ref, fn_ref, c == 0, c == nc - 1, cw_ref, cb_ref, wdf_ref,
                br_ref.at[0:1], bi_ref.at[0:1], lam_ref.at[0:1], af_scr, bf_scr)
    _lru_coeffs(bm_ref, bp_ref, bn_ref, c == nc - 1, c == 0, cw_ref, cb_ref, wdb_ref,
                br_ref.at[1:2], bi_ref.at[1:2], lam_ref.at[1:2], ab_scr, bb_scr)

    def step(t, carry):
        hf, hb = carry
        hf = af_scr[t] * hf + bf_scr[t]
        hf_ref[t] = hf
        tb = tl - 1 - t
        hb = ab_scr[tb] * hb + bb_scr[tb]
        hb_ref[tb] = hb
        return hf, hb

    hf, hb = lax.fori_loop(0, tl, step, (sf_scr[...], sb_scr[...]))
    sf_scr[...] = hf
    sb_scr[...] = hb

    @pl.when(c == nc - 1)
    def _():
        lastf_ref[...] = hf
        lastb_ref[...] = hb


def _scan_call(xb, h0f, h0b, conv_w, conv_b, wd, b_r, b_i, lam, steps):
    ls, nb, r = xb.shape
    tl = min(steps, ls)
    nc = ls // tl
    fwd, bwd = (lambda c: c), (lambda c: nc - 1 - c)
    main = lambda ch: pl.BlockSpec((tl, nb, r), lambda c: (ch(c), 0, 0))
    prev2 = lambda ch: pl.BlockSpec((2, nb, r), lambda c: (jnp.maximum(ch(c) * (tl // 2) - 1, 0), 0, 0))
    next1 = lambda ch: pl.BlockSpec((1, nb, r), lambda c: (jnp.minimum((ch(c) + 1) * tl, ls - 1), 0, 0))
    full = lambda a: pl.BlockSpec(a.shape, lambda c: (0,) * a.ndim)
    state = pl.BlockSpec((nb, r), lambda c: (0, 0))
    wd_spec = lambda dirn: pl.BlockSpec((None,) + wd.shape[1:], lambda c: (dirn, 0, 0, 0))
    return pl.pallas_call(
        _scan_kernel,
        grid=(nc,),
        in_specs=[main(fwd), prev2(fwd), next1(fwd), main(bwd), prev2(bwd), next1(bwd),
                  state, state, full(conv_w), full(conv_b), wd_spec(0), wd_spec(1),
                  full(b_r), full(b_i), full(lam)],
        out_specs=[main(fwd), main(bwd), state, state],
        out_shape=[jax.ShapeDtypeStruct((ls, nb, r), F32), jax.ShapeDtypeStruct((ls, nb, r), F32),
                   jax.ShapeDtypeStruct((nb, r), F32), jax.ShapeDtypeStruct((nb, r), F32)],
        scratch_shapes=[pltpu.VMEM((tl, nb, r), F32)] * 4 + [pltpu.VMEM((nb, r), F32)] * 2,
        compiler_params=_cparams(("arbitrary",), VMEM_LIMIT_BYTES),
        name="rglru_scan",
    )(xb, xb, xb, xb, xb, xb, h0f, h0b, conv_w, conv_b, wd, wd, b_r, b_i, lam)


def _paired_block_diag(w_r, w_i):
    def pair(w):
        ndir = w.shape[0]
        w = w.reshape(ndir, LRU_BLOCKS // 2, 2, LRU_BLOCK_DIM, LRU_BLOCK_DIM)
        z = jnp.zeros_like(w[:, :, 0])
        top = jnp.concatenate([w[:, :, 0], z], axis=-1)
        bot = jnp.concatenate([z, w[:, :, 1]], axis=-1)
        return jnp.concatenate([top, bot], axis=-2)
    return (0.5 * jnp.concatenate([pair(w_r), pair(w_i)], axis=-1)).astype(BF16)


ROUTE_IDX_LANE = N_EXPERTS
ROUTE_W_LANE = N_EXPERTS + TOP_K
ROUTE_ROW_LANE = N_EXPERTS + 2 * TOP_K


def _split_bf16(a):
    hi = a.astype(BF16)
    return hi, (a - hi.astype(F32)).astype(BF16)


def _route(h, wr_ref, info_ref, cnt_ref):
    h_hi, h_lo = _split_bf16(h)
    both = jnp.dot(h_hi, wr_ref[...], preferred_element_type=F32)
    logits = (both[:, :LANES] + both[:, LANES:]
              + jnp.dot(h_lo, wr_ref[:, :LANES], preferred_element_type=F32))
    lane = lax.broadcasted_iota(jnp.int32, logits.shape, 1)
    logits = jnp.where(lane < N_EXPERTS, logits, -jnp.inf)
    m1 = jnp.max(logits, axis=-1, keepdims=True)
    i1 = jnp.min(jnp.where(logits == m1, lane, LANES), axis=-1, keepdims=True)
    rest = jnp.where(lane == i1, -jnp.inf, logits)
    m2 = jnp.max(rest, axis=-1, keepdims=True)
    i2 = jnp.min(jnp.where(rest == m2, lane, LANES), axis=-1, keepdims=True)
    e2 = jnp.exp(m2 - m1)
    w1 = 1.0 / (1.0 + e2)
    w2 = e2 / (1.0 + e2)
    chosen = jnp.where((lane == i1) | (lane == i2), 1.0, 0.0)
    tm = h.shape[0]
    earlier = (lax.broadcasted_iota(jnp.int32, (tm, tm), 0)
               > lax.broadcasted_iota(jnp.int32, (tm, tm), 1))
    rank = jnp.dot(jnp.where(earlier, 1.0, 0.0).astype(BF16), chosen.astype(BF16),
                   preferred_element_type=F32)
    counts = jnp.sum(chosen, axis=0, keepdims=True)
    padded = jnp.ceil(counts * (1.0 / SEG_ALIGN)) * SEG_ALIGN
    before = (lax.broadcasted_iota(jnp.int32, (LANES, LANES), 0)
              < lax.broadcasted_iota(jnp.int32, (LANES, LANES), 1))
    start = jnp.dot(jnp.broadcast_to(padded, (SUBLANES, LANES)).astype(BF16),
                    jnp.where(before, 1.0, 0.0).astype(BF16), preferred_element_type=F32)[0:1]
    row = rank + start
    d1 = jnp.sum(jnp.where(lane == i1, row, 0.0), axis=-1, keepdims=True)
    d2 = jnp.sum(jnp.where(lane == i2, row, 0.0), axis=-1, keepdims=True)
    info = jnp.where(lane == i1, w1, jnp.where(lane == i2, w2, 0.0))
    for k, val in enumerate((i1.astype(F32), i2.astype(F32), w1, w2, d1, d2)):
        info = jnp.where(lane == ROUTE_IDX_LANE + k, val, info)
    info_ref[...] = info
    cnt_ref[...] = counts


def _outproj_kernel(gate_ref, *refs):
    state_refs, refs = refs[:2 * LRU_GROUPS], refs[2 * LRU_GROUPS:]
    w_ref, x_ref, mod_ref, g_ref, b_ref, wr_ref, out_ref, h_ref, info_ref, cnt_ref = refs
    d = x_ref.shape[-1]
    gt = gate_ref[...]
    gelu = 0.5 * gt * (1.0 + jnp.tanh(0.7978845608028654 * (gt + 0.044715 * (gt * gt * gt))))
    states = jnp.concatenate([state_refs[g][...] + state_refs[LRU_GROUPS + g][...]
                              for g in range(LRU_GROUPS)], axis=1)
    y = gelu * states
    proj = jnp.dot(y.astype(BF16), w_ref[...], preferred_element_type=F32)
    xn = _layer_norm(ALPHA * x_ref[...] + _mod_slice(mod_ref, 2, d) * proj, g_ref[...], b_ref[...])
    out_ref[...] = xn
    h = xn * (1.0 + _mod_slice(mod_ref, 4, d)) + _mod_slice(mod_ref, 3, d)
    h_ref[...] = h.astype(BF16)
    _route(h, wr_ref, info_ref, cnt_ref)


def _outproj_call(gate, hf, hb, w_out, x, mod, ln_g, ln_b, router_padded, rows):
    b, l, d = x.shape
    r = gate.shape[-1]
    tm = rows
    nt = l // tm
    tmajor = pl.BlockSpec((tm, r // LRU_GROUPS), lambda i, j: (j, i))
    return pl.pallas_call(
        _outproj_kernel,
        grid=(b, nt),
        in_specs=[
            pl.BlockSpec((None, tm, r), lambda i, j: (i, j, 0)),
            *([tmajor] * (2 * LRU_GROUPS)),
            pl.BlockSpec(w_out.shape, lambda i, j: (0, 0)),
            pl.BlockSpec((None, tm, d), lambda i, j: (i, j, 0)),
            pl.BlockSpec((None, 1, 6 * d), lambda i, j: (i, 0, 0)),
            pl.BlockSpec((1, d), lambda i, j: (0, 0)),
            pl.BlockSpec((1, d), lambda i, j: (0, 0)),
            pl.BlockSpec(router_padded.shape, lambda i, j: (0, 0)),
        ],
        out_specs=[
            pl.BlockSpec((None, tm, d), lambda i, j: (i, j, 0)),
            pl.BlockSpec((tm, d), lambda i, j: (i * nt + j, 0)),
            pl.BlockSpec((tm, LANES), lambda i, j: (i * nt + j, 0)),
            pl.BlockSpec((None, 1, LANES), lambda i, j: (i * nt + j, 0, 0)),
        ],
        out_shape=[jax.ShapeDtypeStruct((b, l, d), F32), jax.ShapeDtypeStruct((b * l, d), BF16),
                   jax.ShapeDtypeStruct((b * l, LANES), F32),
                   jax.ShapeDtypeStruct((b * nt, 1, LANES), F32)],
        compiler_params=_cparams(("arbitrary", "arbitrary"), VMEM_LIMIT_BYTES),
        name="lru_out_proj_route",
    )(gate, *hf, *hb, w_out, x, mod, ln_g, ln_b, router_padded)


SEG_ALIGN = SUBLANES
MAX_SEG_PAD = N_EXPERTS * (SEG_ALIGN - 1)
BF16_ROWS = 2 * SUBLANES


def _window_rows(window):
    return -(-(TOP_K * window + MAX_SEG_PAD) // BF16_ROWS) * BF16_ROWS


def _for_each_piece(seg_refs, w, window, fn):
    loff_ref, goff_ref, cpad_ref = seg_refs
    for e in range(N_EXPERTS):
        lo, go, c = (ref[w * N_EXPERTS + e] for ref in (loff_ref, goff_ref, cpad_ref))
        size = window
        while size >= SEG_ALIGN:
            done = c & ~(2 * size - 1)

            @pl.when((c & size) != 0)
            def _(lo=lo, go=go, done=done, size=size):
                fn(pl.multiple_of(lo + done, SEG_ALIGN), pl.multiple_of(go + done, SEG_ALIGN), size)

            size //= 2


def _for_each_tail_piece(tail_ref, len_ref, max_size, fn):
    for e in range(N_EXPERTS):
        go, c = tail_ref[e], len_ref[e]
        size = max_size
        while size >= SEG_ALIGN:
            done = c & ~(2 * size - 1)

            @pl.when((c & size) != 0)
            def _(go=go, done=done, size=size):
                fn(pl.multiple_of(go + done, SEG_ALIGN), size)

            size //= 2


def _dispatch_kernel(loff_ref, goff_ref, cpad_ref, tail_ref, tlen_ref, rest_ref, dloc_ref, wts_ref,
                     h_ref, xs_hbm, buf, zbuf, sems):
    w, nw = pl.program_id(0), pl.num_programs(0)
    slot = w % 2
    window, d = h_ref.shape
    r = buf.shape[1]
    zrows = zbuf.shape[0]
    segs = (loff_ref, goff_ref, cpad_ref)

    def copy(buf_slot, lrow, grow, size):
        return pltpu.make_async_copy(buf.at[buf_slot, pl.ds(lrow, size)],
                                     xs_hbm.at[pl.ds(grow, size)], sems.at[buf_slot])

    def fill(grow, size):
        return pltpu.make_async_copy(zbuf.at[pl.ds(0, size)], xs_hbm.at[pl.ds(grow, size)],
                                     sems.at[2])

    def for_each_fill(act):
        _for_each_tail_piece(tail_ref, tlen_ref, zrows, lambda g, s: act(fill(g, s)))

        def body(k, _):
            act(fill(pl.multiple_of(rest_ref[0] + k * zrows, SEG_ALIGN), zrows))
            return 0
        lax.fori_loop(0, (xs_hbm.shape[0] - rest_ref[0]) // zrows, body, 0)

    @pl.when(w == 0)
    def _():
        zbuf[...] = jnp.zeros_like(zbuf)
        for_each_fill(lambda cp: cp.start())

    row = lax.broadcasted_iota(jnp.int32, (r, window), 0)
    hit0 = row == dloc_ref[0:1, :]
    hit1 = row == dloc_ref[1:2, :]
    perm = jnp.where(hit0 | hit1, 1.0, 0.0).astype(BF16)
    buf[slot, :, :d] = jnp.dot(perm, h_ref[...], preferred_element_type=F32)
    roww = jnp.sum(jnp.where(hit0, wts_ref[0:1, :], 0.0) + jnp.where(hit1, wts_ref[1:2, :], 0.0),
                   axis=1, keepdims=True)
    buf[slot, :, d:] = jnp.broadcast_to(roww, (r, LANES))

    _for_each_piece(segs, w, window, lambda l, g, s: copy(slot, l, g, s).start())

    @pl.when(w > 0)
    def _():
        _for_each_piece(segs, w - 1, window, lambda l, g, s: copy(1 - slot, l, g, s).wait())

    @pl.when(w == nw - 1)
    def _():
        _for_each_piece(segs, w, window, lambda l, g, s: copy(slot, l, g, s).wait())
        for_each_fill(lambda cp: cp.wait())


def _dispatch_call(plan, h, window, rows, n_rows):
    n, d = h.shape
    nw = n // window
    r = _window_rows(window)
    width = d + LANES
    grid_spec = pltpu.PrefetchScalarGridSpec(
        num_scalar_prefetch=6,
        grid=(nw,),
        in_specs=[
            pl.BlockSpec((None, TOP_K, window), lambda w, *_: (w, 0, 0)),
            pl.BlockSpec((None, TOP_K, window), lambda w, *_: (w, 0, 0)),
            pl.BlockSpec((window, d), lambda w, *_: (w, 0)),
        ],
        out_specs=pl.BlockSpec(memory_space=pl.ANY),
        scratch_shapes=[pltpu.VMEM((2, r, width), F32), pltpu.VMEM((rows // 2, width), F32),
                        pltpu.SemaphoreType.DMA((3,))],
    )
    return pl.pallas_call(
        _dispatch_kernel,
        grid_spec=grid_spec,
        out_shape=jax.ShapeDtypeStruct((n_rows, width), F32),
        compiler_params=_cparams(("arbitrary",), VMEM_LIMIT_BYTES),
        name="expert_dispatch",
    )(plan["loff"], plan["goff"], plan["cpad"], plan["tail"], plan["tail_len"], plan["rest"],
      plan["dloc_rows"], plan["wts_rows"], h)


def _expert_kernel(te_ref, nv_ref, xs_ref, wg_ref, wu_ref, wd_ref, y_ref, hb_scr, acc_scr):
    i, f, nf = pl.program_id(0), pl.program_id(1), pl.num_programs(1)
    d = y_ref.shape[-1]
    valid = i < nv_ref[0]

    @pl.when(valid & (f == 0))
    def _():
        hb_scr[...] = xs_ref[:, :d].astype(BF16)
        acc_scr[...] = jnp.zeros_like(acc_scr)

    @pl.when(valid)
    def _():
        acc_scr[...] += _swiglu_chunk(hb_scr[...], wg_ref[...].astype(BF16),
                                      wu_ref[...].astype(BF16), wd_ref[...].astype(BF16))

    @pl.when(valid & (f == nf - 1))
    def _():
        y_ref[...] = acc_scr[...] * xs_ref[:, d:d + 1]

    @pl.when(jnp.logical_not(valid) & (f == nf - 1))
    def _():
        y_ref[...] = jnp.zeros_like(y_ref)


def _expert_call(plan, xs, w_gu, w_dn, rows, cols):
    n_rows, width = xs.shape
    d = width - LANES
    nt = n_rows // rows
    ff = w_dn.shape[1]
    fc = cols
    nf = ff // fc
    tile = lambda i, nv: jnp.minimum(i, nv[0] - 1)
    chunk = lambda i, f, nv: jnp.where(i < nv[0], f, nf - 1)
    grid_spec = pltpu.PrefetchScalarGridSpec(
        num_scalar_prefetch=2,
        grid=(nt, nf),
        in_specs=[
            pl.BlockSpec((rows, width), lambda i, f, te, nv: (tile(i, nv), 0)),
            pl.BlockSpec((None, d, fc), lambda i, f, te, nv: (te[tile(i, nv)], 0, chunk(i, f, nv))),
            pl.BlockSpec((None, d, fc),
                         lambda i, f, te, nv: (te[tile(i, nv)], 0, nf + chunk(i, f, nv))),
            pl.BlockSpec((None, fc, d), lambda i, f, te, nv: (te[tile(i, nv)], chunk(i, f, nv), 0)),
        ],
        out_specs=pl.BlockSpec((rows, d), lambda i, f, te, nv: (i, 0)),
        scratch_shapes=[pltpu.VMEM((rows, d), BF16), pltpu.VMEM((rows, d), F32)],
    )
    return pl.pallas_call(
        _expert_kernel,
        grid_spec=grid_spec,
        out_shape=jax.ShapeDtypeStruct((n_rows, d), F32),
        compiler_params=_cparams(("arbitrary", "arbitrary"), VMEM_LIMIT_BYTES),
        name="expert_swiglu",
    )(plan["tile_expert"], plan["n_valid"], xs, w_gu, w_gu, w_dn)


def _combine_kernel(loff_ref, goff_ref, cpad_ref, dcol_ref, y_hbm, x_ref, mod_ref, g_ref, b_ref,
                    out_ref, ybuf, sems):
    w, nw = pl.program_id(0), pl.num_programs(0)
    slot = w % 2
    window, d = x_ref.shape
    r = ybuf.shape[1]
    segs = (loff_ref, goff_ref, cpad_ref)

    def copy(buf_slot, lrow, grow, size):
        return pltpu.make_async_copy(y_hbm.at[pl.ds(grow, size)],
                                     ybuf.at[buf_slot, pl.ds(lrow, size)], sems.at[buf_slot])

    @pl.when(w == 0)
    def _():
        ybuf[...] = jnp.zeros_like(ybuf)
        _for_each_piece(segs, 0, window, lambda l, g, s: copy(0, l, g, s).start())

    @pl.when(w + 1 < nw)
    def _():
        _for_each_piece(segs, w + 1, window, lambda l, g, s: copy(1 - slot, l, g, s).start())

    _for_each_piece(segs, w, window, lambda l, g, s: copy(slot, l, g, s).wait())

    lane = lax.broadcasted_iota(jnp.int32, (window, r), 1)
    hit = (lane == dcol_ref[:, 0:1]) | (lane == dcol_ref[:, 1:2])
    perm = jnp.where(hit, 1.0, 0.0).astype(BF16)
    hi, lo = _split_bf16(ybuf[slot])
    f = (jnp.dot(perm, hi, preferred_element_type=F32)
         + jnp.dot(perm, lo, preferred_element_type=F32))
    out_ref[...] = _layer_norm(ALPHA * x_ref[...] + _mod_slice(mod_ref, 5, d) * f,
                               g_ref[...], b_ref[...])


def _combine_call(plan, y, x, mod, ln_g, ln_b, window):
    b, l, d = x.shape
    wpb = l // window
    r = _window_rows(window)
    grid_spec = pltpu.PrefetchScalarGridSpec(
        num_scalar_prefetch=3,
        grid=(b * wpb,),
        in_specs=[
            pl.BlockSpec((window, TOP_K), lambda w, *_: (w, 0)),
            pl.BlockSpec(memory_space=pl.ANY),
            pl.BlockSpec((None, window, d), lambda w, *_: (w // wpb, w % wpb, 0)),
            pl.BlockSpec((None, 1, 6 * d), lambda w, *_: (w // wpb, 0, 0)),
            pl.BlockSpec((1, d), lambda w, *_: (0, 0)),
            pl.BlockSpec((1, d), lambda w, *_: (0, 0)),
        ],
        out_specs=pl.BlockSpec((None, window, d), lambda w, *_: (w // wpb, w % wpb, 0)),
        scratch_shapes=[pltpu.VMEM((2, r, d), F32), pltpu.SemaphoreType.DMA((2,))],
    )
    return pl.pallas_call(
        _combine_kernel,
        grid_spec=grid_spec,
        out_shape=jax.ShapeDtypeStruct((b, l, d), F32),
        compiler_params=_cparams(("arbitrary",), VMEM_LIMIT_BYTES),
        name="expert_combine",
    )(plan["loff"], plan["goff"], plan["cpad"], plan["dloc_cols"], y, x, mod, ln_g, ln_b)


def _dispatch_plan(info, counts, window, rows):
    n = info.shape[0]
    nw = n // window
    as_int = lambda lane0: info[:, lane0:lane0 + TOP_K].astype(jnp.int32).reshape(nw, window, TOP_K)
    dloc = as_int(ROUTE_ROW_LANE)
    count = counts[:, 0, :N_EXPERTS].astype(jnp.int32)
    cpad = -(-count // SEG_ALIGN) * SEG_ALIGN
    loff = jnp.cumsum(cpad, axis=1) - cpad
    group = -(-jnp.sum(cpad, axis=0) // rows) * rows
    gend = jnp.cumsum(group)
    goff = (gend - group)[None, :] + jnp.cumsum(cpad, axis=0) - cpad
    n_tiles = -(-(TOP_K * n + nw * MAX_SEG_PAD) // rows) + N_EXPERTS
    tile_start = jnp.arange(n_tiles) * rows
    tile_expert = jnp.minimum(jnp.sum(tile_start[:, None] >= gend[None, :], axis=1), N_EXPERTS - 1)
    wts = info[:, ROUTE_W_LANE:ROUTE_W_LANE + TOP_K].reshape(nw, window, TOP_K)
    flat = lambda a: a.reshape(-1).astype(jnp.int32)
    return dict(
        loff=flat(loff), goff=flat(goff), cpad=flat(cpad),
        dloc_rows=jnp.transpose(dloc, (0, 2, 1)).astype(jnp.int32),
        dloc_cols=dloc.reshape(n, TOP_K).astype(jnp.int32),
        wts_rows=jnp.transpose(wts, (0, 2, 1)),
        tail=flat(goff[-1] + cpad[-1]), tail_len=flat(gend - goff[-1] - cpad[-1]),
        rest=flat(gend[-1:]),
        tile_expert=tile_expert.astype(jnp.int32),
        n_valid=(gend[-1] // rows).reshape(1).astype(jnp.int32),
        n_rows=n_tiles * rows,
    )


def _attention_layer(x, ctx, mod_x, mod_c, ln_g, ln_b, w_qkv, w_o, sink, w_gu, w_dn, grid_w, ts):
    b, l, d = x.shape
    lc = ctx.shape[1]
    g0, b0, g1, b1 = ln_g[0:1], ln_b[0:1], ln_g[1:2], ln_b[1:2]
    w_qkv, w_o, w_gu, w_dn = (w.astype(BF16) for w in (w_qkv, w_o, w_gu, w_dn))
    ctx_flat = ctx.reshape(1, b * lc, d)
    n_kv = N_KV_HEADS * HEAD_DIM

    q, k, v = _qkv_call(x, mod_x, w_qkv, _rope_tables(l, grid_w), ts["qkv_rows"])
    qc, kc, vc = _qkv_call(ctx_flat, mod_c, w_qkv, _no_rope_tables(b * lc), ts["qkv_rows"])
    qc = qc.reshape(b, lc, -1)
    kc, vc = kc.reshape(b, lc, n_kv), vc.reshape(b, lc, n_kv)
    mod_cb = jnp.broadcast_to(mod_c, (b,) + mod_c.shape[1:])

    x = _attn_call(q, k, v, kc, vc, sink, w_o, x, mod_x, g0, b0, band=True)
    ctx = _attn_call(qc, None, None, kc, vc, sink, w_o, ctx, mod_cb, g0, b0, band=False)

    x = _ffn_call(x, mod_x, w_gu, w_dn, g1, b1, ts["ffn_rows"], ts["ffn_cols"])
    ctx = _ffn_call(ctx.reshape(1, b * lc, d), mod_c, w_gu, w_dn, g1, b1,
                    ts["ffn_rows"], ts["ffn_cols"]).reshape(b, lc, d)
    return x, ctx


def _recurrent_layer(x, ctx, mod_x, mod_c, ln_g, ln_b, w_in, conv_w, conv_b, lam, w_r, b_r, w_i,
                     b_i, w_out, router, w_gu, w_dn, ts):
    b, l, d = x.shape
    g0, b0, g1, b1 = ln_g[0:1], ln_b[0:1], ln_g[1:2], ln_b[1:2]
    w_in, w_out = w_in.astype(BF16), w_out.astype(BF16)
    wd = _paired_block_diag(w_r, w_i)
    conv_b = conv_b.reshape(1, -1)
    mod_cb = jnp.broadcast_to(mod_c, (b,) + mod_c.shape[1:])

    r = w_out.shape[0]
    rg = r // LRU_GROUPS
    gate, *xb = _inproj_call(x, mod_x, w_in, ts["proj_rows"])
    _, *xb_c = _inproj_call(ctx, mod_cb, w_in, ts["proj_rows"])
    zero = jnp.zeros((b, rg), F32)
    hf, hb = [], []
    for g in range(LRU_GROUPS):
        cs = slice(g * rg, (g + 1) * rg)
        pairs = slice(g * rg // (2 * LRU_BLOCK_DIM), (g + 1) * rg // (2 * LRU_BLOCK_DIM))
        scan = functools.partial(_scan_call, conv_w=conv_w[:, cs], conv_b=conv_b[:, cs],
                                 wd=wd[:, pairs], b_r=0.5 * b_r[:, cs], b_i=0.5 * b_i[:, cs],
                                 lam=lam[:, cs], steps=ts["scan_steps"])
        _, _, cf, cb = scan(xb_c[g].reshape(-1, b, rg), zero, zero)
        hf_g, hb_g, _, _ = scan(xb[g].reshape(l, b, rg), cf, cb)
        hf.append(hf_g.reshape(l, b * rg))
        hb.append(hb_g.reshape(l, b * rg))
    router_padded = jnp.concatenate(
        _split_bf16(jnp.pad(router, ((0, 0), (0, LANES - N_EXPERTS)))), axis=1)
    x, h, info, counts = _outproj_call(gate, hf, hb, w_out, x, mod_x, g0, b0, router_padded,
                                       ts["moe_window"])
    plan = _dispatch_plan(info, counts, ts["moe_window"], ts["moe_rows"])
    xs = _dispatch_call(plan, h, ts["moe_window"], ts["moe_rows"], plan["n_rows"])
    y = _expert_call(plan, xs, w_gu, w_dn, ts["moe_rows"], ts["ffn_cols"])
    return _combine_call(plan, y, x, mod_x, g1, b1, ts["moe_window"])


def kernel(x, c, ctx, c_ctx, w_mod, b_mod, ln_g, ln_b, attn_w_qkv, attn_w_o, attn_sink, ffn_w_gu,
           ffn_w_dn, lru_w_in, lru_conv_w, lru_conv_b, lru_lambda, lru_w_r, lru_b_r, lru_w_i,
           lru_b_i, lru_w_out, moe_router, moe_w_gu, moe_w_dn):
    b, l, d = x.shape
    assert w_mod.shape[0] == DEPTH and l % BLOCK == 0 and ctx.shape[1] % BLOCK == 0
    grid_w = 64
    ts = _tile_sizes()

    rows = -(-(b + 1) // SUBLANES) * SUBLANES
    cond = jnp.zeros((rows, d), F32).at[:b].set(c).at[b].set(c_ctx)
    mod = _mod_call(cond, w_mod, b_mod[:, None, :])
    mod_x = lambda i: mod[i, :b, None, :]
    mod_c = lambda i: mod[i, b:b + 1, None, :]

    x, ctx = _attention_layer(x, ctx, mod_x(0), mod_c(0), ln_g[0], ln_b[0], attn_w_qkv[0],
                              attn_w_o[0], attn_sink[0], ffn_w_gu[0], ffn_w_dn[0], grid_w, ts)
    return _recurrent_layer(x, ctx, mod_x(1), mod_c(1), ln_g[1], ln_b[1], lru_w_in[0],
                            lru_conv_w[0], lru_conv_b[0], lru_lambda[0], lru_w_r[0], lru_b_r[0],
                            lru_w_i[0], lru_b_i[0], lru_w_out[0], moe_router[0], moe_w_gu[0],
                            moe_w_dn[0], ts)
```

```python
import functools

import jax
import jax.numpy as jnp
import numpy as np
from jax import lax
from jax.experimental import pallas as pl
from jax.experimental.pallas import tpu as pltpu

N_HEADS = 8
N_KV_HEADS = 2
GROUP = N_HEADS // N_KV_HEADS
HEAD_DIM = 128
BLOCK = 128
AXIS_DIM = HEAD_DIM // 2
ROPE_BASE = 10000.0
LRU_BLOCKS = 8
LRU_BLOCK_DIM = 128
LRU_C = 8.0
N_EXPERTS = 8
TOP_K = 2
DEPTH = 2
ALPHA = (2 * DEPTH) ** 0.25
LN_EPS = 1e-5
NEG_INF = -1e30

LANES = 128
SUBLANES = 8
VMEM_LIMIT_BYTES = 56 * 1024 * 1024

F32 = jnp.float32
BF16 = jnp.bfloat16


def _tile_sizes():
    return dict(
        qkv_rows=512,
        ffn_rows=1024,
        ffn_cols=512,
        proj_rows=512,
        scan_steps=128,
        moe_window=512,
        moe_rows=1024,
    )


def _cparams(sem, vmem=None):
    return pltpu.CompilerParams(dimension_semantics=sem, vmem_limit_bytes=vmem)


def _layer_norm(h, g, b):
    mu = jnp.mean(h, axis=-1, keepdims=True)
    d = h - mu
    var = jnp.mean(d * d, axis=-1, keepdims=True)
    return d * lax.rsqrt(var + LN_EPS) * g + b


def _mod_slice(mod_ref, k, d):
    return mod_ref[:, k * d:(k + 1) * d]


def _mod_kernel(c_ref, w_ref, b_ref, o_ref):
    c = c_ref[...]
    s = c * jax.nn.sigmoid(c)
    o_ref[...] = jnp.dot(s, w_ref[...], preferred_element_type=F32,
                         precision=lax.Precision.HIGHEST) + b_ref[...]


def _mod_call(cond, w_mod, b_mod):
    rows, d = cond.shape
    depth, _, n = w_mod.shape
    tn = 1536
    return pl.pallas_call(
        _mod_kernel,
        grid=(depth, n // tn),
        in_specs=[
            pl.BlockSpec((rows, d), lambda i, j: (0, 0)),
            pl.BlockSpec((None, d, tn), lambda i, j: (i, 0, j)),
            pl.BlockSpec((None, 1, tn), lambda i, j: (i, 0, j)),
        ],
        out_specs=pl.BlockSpec((None, rows, tn), lambda i, j: (i, 0, j)),
        out_shape=jax.ShapeDtypeStruct((depth, rows, n), F32),
        compiler_params=_cparams(("arbitrary", "arbitrary")),
        name="mod_vectors",
    )(cond, w_mod, b_mod)


def _qkv_kernel(x_ref, mod_ref, w_ref, cos_ref, sa_ref, sb_ref, q_ref, k_ref, v_ref):
    d = x_ref.shape[-1]
    h = x_ref[...] * (1.0 + _mod_slice(mod_ref, 1, d)) + _mod_slice(mod_ref, 0, d)
    qkv = jnp.dot(h.astype(BF16), w_ref[...], preferred_element_type=F32)
    cos, sa, sb = cos_ref[...], sa_ref[...], sb_ref[...]
    scale = HEAD_DIM ** -0.5
    for head in range(N_HEADS + N_KV_HEADS):
        u = qkv[:, head * HEAD_DIM:(head + 1) * HEAD_DIM]
        r = (u * cos + pltpu.roll(u, HEAD_DIM - AXIS_DIM // 2, axis=1) * sa
             + pltpu.roll(u, AXIS_DIM // 2, axis=1) * sb)
        if head < N_HEADS:
            q_ref[:, head * HEAD_DIM:(head + 1) * HEAD_DIM] = (r * scale).astype(BF16)
        else:
            kh = head - N_HEADS
            k_ref[:, kh * HEAD_DIM:(kh + 1) * HEAD_DIM] = r.astype(BF16)
    kv = N_KV_HEADS * HEAD_DIM
    v_ref[...] = qkv[:, N_HEADS * HEAD_DIM + kv:].astype(BF16)


def _qkv_call(x, mod, w_qkv, tabs, rows):
    b, l, d = x.shape
    n_q, n_kv = N_HEADS * HEAD_DIM, N_KV_HEADS * HEAD_DIM
    tm = min(rows, l)
    tab_spec = pl.BlockSpec((tm, HEAD_DIM), lambda i, j: (j, 0))
    return pl.pallas_call(
        _qkv_kernel,
        grid=(b, l // tm),
        in_specs=[
            pl.BlockSpec((None, tm, d), lambda i, j: (i, j, 0)),
            pl.BlockSpec((None, 1, 6 * d), lambda i, j: (i, 0, 0)),
            pl.BlockSpec(w_qkv.shape, lambda i, j: (0, 0)),
            tab_spec, tab_spec, tab_spec,
        ],
        out_specs=[
            pl.BlockSpec((None, tm, n_q), lambda i, j: (i, j, 0)),
            pl.BlockSpec((None, tm, n_kv), lambda i, j: (i, j, 0)),
            pl.BlockSpec((None, tm, n_kv), lambda i, j: (i, j, 0)),
        ],
        out_shape=[
            jax.ShapeDtypeStruct((b, l, n_q), BF16),
            jax.ShapeDtypeStruct((b, l, n_kv), BF16),
            jax.ShapeDtypeStruct((b, l, n_kv), BF16),
        ],
        compiler_params=_cparams(("arbitrary", "arbitrary"), VMEM_LIMIT_BYTES),
        name="qkv_rope",
    )(x, mod, w_qkv, *tabs)


def _rope_tables(l, grid_w):
    f32 = np.float32
    t = np.arange(l)
    pos = np.stack([(t // grid_w).astype(f32), (t % grid_w).astype(f32)], axis=1)
    freqs = f32(ROPE_BASE) ** (-np.arange(0, AXIS_DIM, 2, dtype=f32) / f32(AXIS_DIM))
    lane = np.arange(HEAD_DIM)
    ang = (pos[:, lane // AXIS_DIM] * freqs[lane % (AXIS_DIM // 2)][None, :]).astype(f32)
    first = (lane % AXIS_DIM) < AXIS_DIM // 2
    cos, sin = np.cos(ang).astype(f32), np.sin(ang).astype(f32)
    zero = f32(0.0)
    return cos, np.where(first, -sin, zero), np.where(first, zero, sin)


def _no_rope_tables(l):
    z = np.zeros((l, HEAD_DIM), np.float32)
    return np.ones((l, HEAD_DIM), np.float32), z, z


def _nt_dot(a, b):
    return lax.dot_general(a, b, (((1,), (1,)), ((), ())), preferred_element_type=F32)


def _attend(qs, segs, sink_col):
    k_all = jnp.concatenate([k for k, _, _ in segs], axis=0)
    v_all = jnp.concatenate([v for _, v, _ in segs], axis=0)
    v_ext = jnp.concatenate([v_all, jnp.ones_like(v_all)], axis=1)
    s_all = _nt_dot(qs, k_all)
    tiles, col = [], 0
    for k, _, mask in segs:
        for c in range(col, col + k.shape[0], LANES):
            s = s_all[:, c:c + LANES]
            tiles.append(s if mask is None else jnp.where(mask, s, NEG_INF))
        col += k.shape[0]
    m_tile = tiles[0]
    for s in tiles[1:]:
        m_tile = jnp.maximum(m_tile, s)
    m = jnp.maximum(sink_col, jnp.max(m_tile, axis=-1, keepdims=True))
    p_all = jnp.concatenate([jnp.exp(s - m).astype(BF16) for s in tiles], axis=1)
    res = jnp.dot(p_all, v_ext, preferred_element_type=F32)
    hd = v_all.shape[1]
    return res[:, :hd] / (res[:, hd:hd + 1] + jnp.exp(sink_col - m))


def _attn_epilogue(o_scr, wo_ref, x_ref, mod_ref, g_ref, b_ref, out_ref):
    d = x_ref.shape[-1]
    proj = jnp.dot(o_scr[...], wo_ref[...], preferred_element_type=F32)
    out_ref[...] = _layer_norm(ALPHA * x_ref[...] + _mod_slice(mod_ref, 2, d) * proj,
                               g_ref[...], b_ref[...])


def _sink_column(sink_ref, kvh):
    return jnp.concatenate(
        [jnp.full((BLOCK, 1), sink_ref[kvh * GROUP + g], F32) for g in range(GROUP)], axis=0)


def _stack_group(q_ref, kvh):
    return jnp.concatenate(
        [q_ref[:, (kvh * GROUP + g) * HEAD_DIM:(kvh * GROUP + g + 1) * HEAD_DIM]
         for g in range(GROUP)], axis=0)


def _unstack_group(o, o_scr, kvh):
    for g in range(GROUP):
        col = (kvh * GROUP + g) * HEAD_DIM
        o_scr[:, col:col + HEAD_DIM] = o[g * BLOCK:(g + 1) * BLOCK].astype(BF16)


def _band_attn_kernel(sink_ref, q_ref, kp_ref, kc_ref, kn_ref, vp_ref, vc_ref, vn_ref,
                      kx_ref, vx_ref, wo_ref, x_ref, mod_ref, g_ref, b_ref, out_ref, o_scr):
    n, nb = pl.program_id(1), pl.num_programs(1)
    row = lax.broadcasted_iota(jnp.int32, (GROUP * BLOCK, BLOCK), 0) % BLOCK
    col = lax.broadcasted_iota(jnp.int32, (GROUP * BLOCK, BLOCK), 1)
    mask_prev = (col >= row) & (n > 0)
    mask_next = (col <= row) & (n < nb - 1)
    for kvh in range(N_KV_HEADS):
        hs = slice(kvh * HEAD_DIM, (kvh + 1) * HEAD_DIM)
        segs = [(kp_ref[:, hs], vp_ref[:, hs], mask_prev),
                (kc_ref[:, hs], vc_ref[:, hs], None),
                (kn_ref[:, hs], vn_ref[:, hs], mask_next),
                (kx_ref[:, hs], vx_ref[:, hs], None)]
        o = _attend(_stack_group(q_ref, kvh), segs, _sink_column(sink_ref, kvh))
        _unstack_group(o, o_scr, kvh)
    _attn_epilogue(o_scr, wo_ref, x_ref, mod_ref, g_ref, b_ref, out_ref)


def _ctx_attn_kernel(sink_ref, q_ref, kx_ref, vx_ref, wo_ref, x_ref, mod_ref, g_ref, b_ref,
                     out_ref, o_scr):
    for kvh in range(N_KV_HEADS):
        hs = slice(kvh * HEAD_DIM, (kvh + 1) * HEAD_DIM)
        o = _attend(_stack_group(q_ref, kvh), [(kx_ref[:, hs], vx_ref[:, hs], None)],
                    _sink_column(sink_ref, kvh))
        _unstack_group(o, o_scr, kvh)
    _attn_epilogue(o_scr, wo_ref, x_ref, mod_ref, g_ref, b_ref, out_ref)


def _attn_call(q, k, v, kx, vx, sink, w_o, x, mod, ln_g, ln_b, band):
    b, l, d = x.shape
    lc = kx.shape[1]
    nb = l // BLOCK
    n_q, n_kv = N_HEADS * HEAD_DIM, N_KV_HEADS * HEAD_DIM
    kv_blk = lambda f: pl.BlockSpec((None, BLOCK, n_kv), f)
    common_in = [
        pl.BlockSpec((None, lc, n_kv), lambda i, j: (i, 0, 0)),
        pl.BlockSpec((None, lc, n_kv), lambda i, j: (i, 0, 0)),
        pl.BlockSpec(w_o.shape, lambda i, j: (0, 0)),
        pl.BlockSpec((None, BLOCK, d), lambda i, j: (i, j, 0)),
        pl.BlockSpec((None, 1, 6 * d), lambda i, j: (i, 0, 0)),
        pl.BlockSpec((1, d), lambda i, j: (0, 0)),
        pl.BlockSpec((1, d), lambda i, j: (0, 0)),
    ]
    head_in = [pl.BlockSpec(memory_space=pltpu.SMEM),
               pl.BlockSpec((None, BLOCK, n_q), lambda i, j: (i, j, 0))]
    if band:
        prev = lambda i, j: (i, jnp.maximum(j - 1, 0), 0)
        cur = lambda i, j: (i, j, 0)
        nxt = lambda i, j: (i, jnp.minimum(j + 1, nb - 1), 0)
        in_specs = head_in + [kv_blk(prev), kv_blk(cur), kv_blk(nxt)] * 2 + common_in
        args = (sink, q, k, k, k, v, v, v, kx, vx, w_o, x, mod, ln_g, ln_b)
        body, name = _band_attn_kernel, "band_attention"
    else:
        in_specs = head_in + common_in
        args = (sink, q, kx, vx, w_o, x, mod, ln_g, ln_b)
        body, name = _ctx_attn_kernel, "context_attention"
    return pl.pallas_call(
        body,
        grid=(b, nb),
        in_specs=in_specs,
        out_specs=pl.BlockSpec((None, BLOCK, d), lambda i, j: (i, j, 0)),
        out_shape=jax.ShapeDtypeStruct((b, l, d), F32),
        scratch_shapes=[pltpu.VMEM((BLOCK, n_q), BF16)],
        compiler_params=_cparams(("arbitrary", "arbitrary"), VMEM_LIMIT_BYTES),
        name=name,
    )(*args)


def _swiglu_chunk(hb, wg, wu, wd):
    g = jnp.dot(hb, wg, preferred_element_type=F32)
    u = jnp.dot(hb, wu, preferred_element_type=F32)
    a = (g * jax.nn.sigmoid(g)) * u
    return jnp.dot(a.astype(BF16), wd, preferred_element_type=F32)


def _ffn_kernel(x_ref, mod_ref, wg_ref, wu_ref, wd_ref, g_ref, b_ref, out_ref, hb_scr, acc_scr):
    f, nf = pl.program_id(2), pl.num_programs(2)
    d = x_ref.shape[-1]

    @pl.when(f == 0)
    def _():
        h = x_ref[...] * (1.0 + _mod_slice(mod_ref, 4, d)) + _mod_slice(mod_ref, 3, d)
        hb_scr[...] = h.astype(BF16)
        acc_scr[...] = jnp.zeros_like(acc_scr)

    acc_scr[...] += _swiglu_chunk(hb_scr[...], wg_ref[...], wu_ref[...], wd_ref[...])

    @pl.when(f == nf - 1)
    def _():
        out_ref[...] = _layer_norm(ALPHA * x_ref[...] + _mod_slice(mod_ref, 5, d) * acc_scr[...],
                                   g_ref[...], b_ref[...])


def _ffn_call(x, mod, w_gu, w_dn, ln_g, ln_b, rows, cols):
    b, l, d = x.shape
    ff = w_dn.shape[0]
    tm, fc = min(rows, l), cols
    nf = ff // fc
    return pl.pallas_call(
        _ffn_kernel,
        grid=(b, l // tm, nf),
        in_specs=[
            pl.BlockSpec((None, tm, d), lambda i, j, f: (i, j, 0)),
            pl.BlockSpec((None, 1, 6 * d), lambda i, j, f: (i, 0, 0)),
            pl.BlockSpec((d, fc), lambda i, j, f: (0, f)),
            pl.BlockSpec((d, fc), lambda i, j, f: (0, nf + f)),
            pl.BlockSpec((fc, d), lambda i, j, f: (f, 0)),
            pl.BlockSpec((1, d), lambda i, j, f: (0, 0)),
            pl.BlockSpec((1, d), lambda i, j, f: (0, 0)),
        ],
        out_specs=pl.BlockSpec((None, tm, d), lambda i, j, f: (i, j, 0)),
        out_shape=jax.ShapeDtypeStruct((b, l, d), F32),
        scratch_shapes=[pltpu.VMEM((tm, d), BF16), pltpu.VMEM((tm, d), F32)],
        compiler_params=_cparams(("arbitrary", "arbitrary", "arbitrary"), VMEM_LIMIT_BYTES),
        name="dense_swiglu",
    )(x, mod, w_gu, w_gu, w_dn, ln_g, ln_b)


LRU_GROUPS = 2
SCAN_UNROLL = 8


def _inproj_kernel(x_ref, mod_ref, w_ref, gate_ref, *xb_refs):
    d = x_ref.shape[-1]
    r = gate_ref.shape[-1]
    rg = r // len(xb_refs)
    h = x_ref[...] * (1.0 + _mod_slice(mod_ref, 1, d)) + _mod_slice(mod_ref, 0, d)
    y = jnp.dot(h.astype(BF16), w_ref[...], preferred_element_type=F32)
    gate_ref[...] = y[:, :r]
    for g, xb_ref in enumerate(xb_refs):
        xb_ref[...] = y[:, r + g * rg:r + (g + 1) * rg]


def _inproj_call(x, mod, w_in, rows):
    b, l, d = x.shape
    r = w_in.shape[1] // 2
    rg = r // LRU_GROUPS
    tm = min(rows, l)
    return pl.pallas_call(
        _inproj_kernel,
        grid=(b, l // tm),
        in_specs=[
            pl.BlockSpec((None, tm, d), lambda i, j: (i, j, 0)),
            pl.BlockSpec((None, 1, 6 * d), lambda i, j: (i, 0, 0)),
            pl.BlockSpec(w_in.shape, lambda i, j: (0, 0)),
        ],
        out_specs=[pl.BlockSpec((None, tm, r), lambda i, j: (i, j, 0))]
        + [pl.BlockSpec((tm, rg), lambda i, j: (j, i))] * LRU_GROUPS,
        out_shape=[jax.ShapeDtypeStruct((b, l, r), F32)]
        + [jax.ShapeDtypeStruct((l, b * rg), F32)] * LRU_GROUPS,
        compiler_params=_cparams(("arbitrary", "arbitrary"), VMEM_LIMIT_BYTES),
        name="lru_in_proj",
    )(x, mod, w_in)


def _lru_coeffs(main_ref, prev_ref, next_ref, first, last, cw_ref, cb_ref, wd_ref, br_ref, bi_ref,
                lam_ref, a_scr, b_scr):
    tl, nb, r = main_ref.shape
    prev = jnp.where(first, 0.0, prev_ref[...])
    nxt = jnp.where(last, 0.0, next_ref[...])
    ext = jnp.concatenate([prev, main_ref[...], nxt], axis=0)
    u = cb_ref[...][None]
    for k in range(4):
        u = u + cw_ref[k:k + 1, :][None] * ext[k:k + tl]
    u = u.reshape(tl * nb, r)
    ub = u.astype(BF16)
    lam = lam_ref[...]
    softplus_neg = jnp.maximum(-lam, 0.0) + jnp.log1p(jnp.exp(-jnp.abs(lam)))
    half_rate = (-0.5 * LRU_C) * softplus_neg
    pw = 2 * LRU_BLOCK_DIM
    for p in range(r // pw):
        cs = slice(p * pw, (p + 1) * pw)
        z = jnp.dot(ub[:, cs], wd_ref[p], preferred_element_type=F32)
        tr = jnp.tanh(z[:, :pw] + br_ref[:, cs])
        ig = 0.5 + 0.5 * jnp.tanh(z[:, pw:] + bi_ref[:, cs])
        log_a = half_rate[:, cs] * tr + half_rate[:, cs]
        a = jnp.exp(log_a)
        gain = jnp.sqrt(-jnp.tanh(log_a) * (a * a + 1.0))
        a_scr[:, :, cs] = a.reshape(tl, nb, pw)
        b_scr[:, :, cs] = (gain * (ig * u[:, cs])).reshape(tl, nb, pw)


def _scan_kernel(fm_ref, fp_ref, fn_ref, bm_ref, bp_ref, bn_ref, h0f_ref, h0b_ref,
                 cw_ref, cb_ref, wdf_ref, wdb_ref, br_ref, bi_ref, lam_ref,
                 hf_ref, hb_ref, lastf_ref, lastb_ref,
                 af_scr, bf_scr, ab_scr, bb_scr, sf_scr, sb_scr):
    c, nc = pl.program_id(0), pl.num_programs(0)
    tl = fm_ref.shape[0]

    @pl.when(c == 0)
    def _():
        sf_scr[...] = h0f_ref[...]
        sb_scr[...] = h0b_ref[...]

    _lru_coeffs(fm_ref, fp_ref, fn_ref, c == 0, c == nc - 1, cw_ref, cb_ref, wdf_ref,
                br_ref.at[0:1], bi_ref.at[0:1], lam_ref.at[0:1], af_scr, bf_scr)
    _lru_coeffs(bm_ref, bp_ref, bn_ref, c == nc - 1, c == 0, cw_ref, cb_ref, wdb_ref,
                br_ref.at[1:2], bi_ref.at[1:2], lam_ref.at[1:2], ab_scr, bb_scr)

    def step(t, carry):
        hf, hb = carry
        hf = af_scr[t] * hf + bf_scr[t]
        hf_ref[t] = hf
        tb = tl - 1 - t
        hb = ab_scr[tb] * hb + bb_scr[tb]
        hb_ref[tb] = hb
        return hf, hb

    hf, hb = lax.fori_loop(0, tl, step, (sf_scr[...], sb_scr[...]), unroll=SCAN_UNROLL)
    sf_scr[...] = hf
    sb_scr[...] = hb

    @pl.when(c == nc - 1)
    def _():
        lastf_ref[...] = hf
        lastb_ref[...] = hb


def _scan_call(xb, h0f, h0b, conv_w, conv_b, wd, b_r, b_i, lam, steps):
    ls, nb, r = xb.shape
    tl = min(steps, ls)
    nc = ls // tl
    fwd, bwd = (lambda c: c), (lambda c: nc - 1 - c)
    main = lambda ch: pl.BlockSpec((tl, nb, r), lambda c: (ch(c), 0, 0))
    prev2 = lambda ch: pl.BlockSpec((2, nb, r), lambda c: (jnp.maximum(ch(c) * (tl // 2) - 1, 0), 0, 0))
    next1 = lambda ch: pl.BlockSpec((1, nb, r), lambda c: (jnp.minimum((ch(c) + 1) * tl, ls - 1), 0, 0))
    full = lambda a: pl.BlockSpec(a.shape, lambda c: (0,) * a.ndim)
    state = pl.BlockSpec((nb, r), lambda c: (0, 0))
    wd_spec = lambda dirn: pl.BlockSpec((None,) + wd.shape[1:], lambda c: (dirn, 0, 0, 0))
    return pl.pallas_call(
        _scan_kernel,
        grid=(nc,),
        in_specs=[main(fwd), prev2(fwd), next1(fwd), main(bwd), prev2(bwd), next1(bwd),
                  state, state, full(conv_w), full(conv_b), wd_spec(0), wd_spec(1),
                  full(b_r), full(b_i), full(lam)],
        out_specs=[main(fwd), main(bwd), state, state],
        out_shape=[jax.ShapeDtypeStruct((ls, nb, r), F32), jax.ShapeDtypeStruct((ls, nb, r), F32),
                   jax.ShapeDtypeStruct((nb, r), F32), jax.ShapeDtypeStruct((nb, r), F32)],
        scratch_shapes=[pltpu.VMEM((tl, nb, r), F32)] * 4 + [pltpu.VMEM((nb, r), F32)] * 2,
        compiler_params=_cparams(("arbitrary",), VMEM_LIMIT_BYTES),
        name="rglru_scan",
    )(xb, xb, xb, xb, xb, xb, h0f, h0b, conv_w, conv_b, wd, wd, b_r, b_i, lam)


def _paired_block_diag(w_r, w_i):
    def pair(w):
        ndir = w.shape[0]
        w = w.reshape(ndir, LRU_BLOCKS // 2, 2, LRU_BLOCK_DIM, LRU_BLOCK_DIM)
        z = jnp.zeros_like(w[:, :, 0])
        top = jnp.concatenate([w[:, :, 0], z], axis=-1)
        bot = jnp.concatenate([z, w[:, :, 1]], axis=-1)
        return jnp.concatenate([top, bot], axis=-2)
    return (0.5 * jnp.concatenate([pair(w_r), pair(w_i)], axis=-1)).astype(BF16)


ROUTE_IDX_LANE = N_EXPERTS
ROUTE_W_LANE = N_EXPERTS + TOP_K
ROUTE_ROW_LANE = N_EXPERTS + 2 * TOP_K


def _split_bf16(a):
    hi = a.astype(BF16)
    return hi, (a - hi.astype(F32)).astype(BF16)


def _route(h, wr_ref, info_ref, cnt_ref):
    h_hi, h_lo = _split_bf16(h)
    both = jnp.dot(h_hi, wr_ref[...], preferred_element_type=F32)
    logits = (both[:, :LANES] + both[:, LANES:]
              + jnp.dot(h_lo, wr_ref[:, :LANES], preferred_element_type=F32))
    lane = lax.broadcasted_iota(jnp.int32, logits.shape, 1)
    logits = jnp.where(lane < N_EXPERTS, logits, -jnp.inf)
    m1 = jnp.max(logits, axis=-1, keepdims=True)
    i1 = jnp.min(jnp.where(logits == m1, lane, LANES), axis=-1, keepdims=True)
    rest = jnp.where(lane == i1, -jnp.inf, logits)
    m2 = jnp.max(rest, axis=-1, keepdims=True)
    i2 = jnp.min(jnp.where(rest == m2, lane, LANES), axis=-1, keepdims=True)
    e2 = jnp.exp(m2 - m1)
    w1 = 1.0 / (1.0 + e2)
    w2 = e2 / (1.0 + e2)
    chosen = jnp.where((lane == i1) | (lane == i2), 1.0, 0.0)
    tm = h.shape[0]
    earlier = (lax.broadcasted_iota(jnp.int32, (tm, tm), 0)
               > lax.broadcasted_iota(jnp.int32, (tm, tm), 1))
    rank = jnp.dot(jnp.where(earlier, 1.0, 0.0).astype(BF16), chosen.astype(BF16),
                   preferred_element_type=F32)
    counts = jnp.sum(chosen, axis=0, keepdims=True)
    padded = jnp.ceil(counts * (1.0 / SEG_ALIGN)) * SEG_ALIGN
    before = (lax.broadcasted_iota(jnp.int32, (LANES, LANES), 0)
              < lax.broadcasted_iota(jnp.int32, (LANES, LANES), 1))
    start = jnp.dot(jnp.broadcast_to(padded, (SUBLANES, LANES)).astype(BF16),
                    jnp.where(before, 1.0, 0.0).astype(BF16), preferred_element_type=F32)[0:1]
    row = rank + start
    d1 = jnp.sum(jnp.where(lane == i1, row, 0.0), axis=-1, keepdims=True)
    d2 = jnp.sum(jnp.where(lane == i2, row, 0.0), axis=-1, keepdims=True)
    info = jnp.where(lane == i1, w1, jnp.where(lane == i2, w2, 0.0))
    for k, val in enumerate((i1.astype(F32), i2.astype(F32), w1, w2, d1, d2)):
        info = jnp.where(lane == ROUTE_IDX_LANE + k, val, info)
    info_ref[...] = info
    cnt_ref[...] = counts


def _outproj_kernel(gate_ref, *refs):
    state_refs, refs = refs[:2 * LRU_GROUPS], refs[2 * LRU_GROUPS:]
    w_ref, x_ref, mod_ref, g_ref, b_ref, wr_ref, out_ref, h_ref, info_ref, cnt_ref = refs
    d = x_ref.shape[-1]
    gt = gate_ref[...]
    gelu = 0.5 * gt * (1.0 + jnp.tanh(0.7978845608028654 * (gt + 0.044715 * (gt * gt * gt))))
    states = jnp.concatenate([state_refs[g][...] + state_refs[LRU_GROUPS + g][...]
                              for g in range(LRU_GROUPS)], axis=1)
    y = gelu * states
    proj = jnp.dot(y.astype(BF16), w_ref[...], preferred_element_type=F32)
    xn = _layer_norm(ALPHA * x_ref[...] + _mod_slice(mod_ref, 2, d) * proj, g_ref[...], b_ref[...])
    out_ref[...] = xn
    h = xn * (1.0 + _mod_slice(mod_ref, 4, d)) + _mod_slice(mod_ref, 3, d)
    h_ref[...] = h.astype(BF16)
    _route(h, wr_ref, info_ref, cnt_ref)


def _outproj_call(gate, hf, hb, w_out, x, mod, ln_g, ln_b, router_padded, rows):
    b, l, d = x.shape
    r = gate.shape[-1]
    tm = rows
    nt = l // tm
    tmajor = pl.BlockSpec((tm, r // LRU_GROUPS), lambda i, j: (j, i))
    return pl.pallas_call(
        _outproj_kernel,
        grid=(b, nt),
        in_specs=[
            pl.BlockSpec((None, tm, r), lambda i, j: (i, j, 0)),
            *([tmajor] * (2 * LRU_GROUPS)),
            pl.BlockSpec(w_out.shape, lambda i, j: (0, 0)),
            pl.BlockSpec((None, tm, d), lambda i, j: (i, j, 0)),
            pl.BlockSpec((None, 1, 6 * d), lambda i, j: (i, 0, 0)),
            pl.BlockSpec((1, d), lambda i, j: (0, 0)),
            pl.BlockSpec((1, d), lambda i, j: (0, 0)),
            pl.BlockSpec(router_padded.shape, lambda i, j: (0, 0)),
        ],
        out_specs=[
            pl.BlockSpec((None, tm, d), lambda i, j: (i, j, 0)),
            pl.BlockSpec((tm, d), lambda i, j: (i * nt + j, 0)),
            pl.BlockSpec((tm, LANES), lambda i, j: (i * nt + j, 0)),
            pl.BlockSpec((None, 1, LANES), lambda i, j: (i * nt + j, 0, 0)),
        ],
        out_shape=[jax.ShapeDtypeStruct((b, l, d), F32), jax.ShapeDtypeStruct((b * l, d), BF16),
                   jax.ShapeDtypeStruct((b * l, LANES), F32),
                   jax.ShapeDtypeStruct((b * nt, 1, LANES), F32)],
        compiler_params=_cparams(("arbitrary", "arbitrary"), VMEM_LIMIT_BYTES),
        name="lru_out_proj_route",
    )(gate, *hf, *hb, w_out, x, mod, ln_g, ln_b, router_padded)


SEG_ALIGN = SUBLANES
MAX_SEG_PAD = N_EXPERTS * (SEG_ALIGN - 1)
BF16_ROWS = 2 * SUBLANES


def _window_rows(window):
    return -(-(TOP_K * window + MAX_SEG_PAD) // BF16_ROWS) * BF16_ROWS


def _for_each_piece(seg_refs, w, window, fn):
    loff_ref, goff_ref, cpad_ref = seg_refs
    for e in range(N_EXPERTS):
        lo, go, c = (ref[w * N_EXPERTS + e] for ref in (loff_ref, goff_ref, cpad_ref))
        size = window
        while size >= SEG_ALIGN:
            done = c & ~(2 * size - 1)

            @pl.when((c & size) != 0)
            def _(lo=lo, go=go, done=done, size=size):
                fn(pl.multiple_of(lo + done, SEG_ALIGN), pl.multiple_of(go + done, SEG_ALIGN), size)

            size //= 2


def _for_each_tail_piece(tail_ref, len_ref, max_size, fn):
    for e in range(N_EXPERTS):
        go, c = tail_ref[e], len_ref[e]
        size = max_size
        while size >= SEG_ALIGN:
            done = c & ~(2 * size - 1)

            @pl.when((c & size) != 0)
            def _(go=go, done=done, size=size):
                fn(pl.multiple_of(go + done, SEG_ALIGN), size)

            size //= 2


def _dispatch_kernel(loff_ref, goff_ref, cpad_ref, tail_ref, tlen_ref, rest_ref, dloc_ref, wts_ref,
                     h_ref, xs_hbm, buf, zbuf, sems):
    w, nw = pl.program_id(0), pl.num_programs(0)
    slot = w % 2
    window, d = h_ref.shape
    r = buf.shape[1]
    zrows = zbuf.shape[0]
    segs = (loff_ref, goff_ref, cpad_ref)

    def copy(buf_slot, lrow, grow, size):
        return pltpu.make_async_copy(buf.at[buf_slot, pl.ds(lrow, size)],
                                     xs_hbm.at[pl.ds(grow, size)], sems.at[buf_slot])

    def fill(grow, size):
        return pltpu.make_async_copy(zbuf.at[pl.ds(0, size)], xs_hbm.at[pl.ds(grow, size)],
                                     sems.at[2])

    def for_each_fill(act):
        _for_each_tail_piece(tail_ref, tlen_ref, zrows, lambda g, s: act(fill(g, s)))

        def body(k, _):
            act(fill(pl.multiple_of(rest_ref[0] + k * zrows, SEG_ALIGN), zrows))
            return 0
        lax.fori_loop(0, (xs_hbm.shape[0] - rest_ref[0]) // zrows, body, 0)

    @pl.when(w == 0)
    def _():
        zbuf[...] = jnp.zeros_like(zbuf)
        for_each_fill(lambda cp: cp.start())

    row = lax.broadcasted_iota(jnp.int32, (r, window), 0)
    hit0 = row == dloc_ref[0:1, :]
    hit1 = row == dloc_ref[1:2, :]
    perm = jnp.where(hit0 | hit1, 1.0, 0.0).astype(BF16)
    buf[slot, :, :d] = jnp.dot(perm, h_ref[...], preferred_element_type=F32)
    roww = jnp.sum(jnp.where(hit0, wts_ref[0:1, :], 0.0) + jnp.where(hit1, wts_ref[1:2, :], 0.0),
                   axis=1, keepdims=True)
    buf[slot, :, d:] = jnp.broadcast_to(roww, (r, LANES))

    _for_each_piece(segs, w, window, lambda l, g, s: copy(slot, l, g, s).start())

    @pl.when(w > 0)
    def _():
        _for_each_piece(segs, w - 1, window, lambda l, g, s: copy(1 - slot, l, g, s).wait())

    @pl.when(w == nw - 1)
    def _():
        _for_each_piece(segs, w, window, lambda l, g, s: copy(slot, l, g, s).wait())
        for_each_fill(lambda cp: cp.wait())


def _dispatch_call(plan, h, window, rows, n_rows):
    n, d = h.shape
    nw = n // window
    r = _window_rows(window)
    width = d + LANES
    grid_spec = pltpu.PrefetchScalarGridSpec(
        num_scalar_prefetch=6,
        grid=(nw,),
        in_specs=[
            pl.BlockSpec((None, TOP_K, window), lambda w, *_: (w, 0, 0)),
            pl.BlockSpec((None, TOP_K, window), lambda w, *_: (w, 0, 0)),
            pl.BlockSpec((window, d), lambda w, *_: (w, 0)),
        ],
        out_specs=pl.BlockSpec(memory_space=pl.ANY),
        scratch_shapes=[pltpu.VMEM((2, r, width), F32), pltpu.VMEM((rows // 2, width), F32),
                        pltpu.SemaphoreType.DMA((3,))],
    )
    return pl.pallas_call(
        _dispatch_kernel,
        grid_spec=grid_spec,
        out_shape=jax.ShapeDtypeStruct((n_rows, width), F32),
        compiler_params=_cparams(("arbitrary",), VMEM_LIMIT_BYTES),
        name="expert_dispatch",
    )(plan["loff"], plan["goff"], plan["cpad"], plan["tail"], plan["tail_len"], plan["rest"],
      plan["dloc_rows"], plan["wts_rows"], h)


def _expert_kernel(te_ref, nv_ref, xs_ref, wg_ref, wu_ref, wd_ref, y_ref, hb_scr, acc_scr):
    i, f, nf = pl.program_id(0), pl.program_id(1), pl.num_programs(1)
    d = y_ref.shape[-1]
    valid = i < nv_ref[0]

    @pl.when(valid & (f == 0))
    def _():
        hb_scr[...] = xs_ref[:, :d].astype(BF16)
        acc_scr[...] = jnp.zeros_like(acc_scr)

    @pl.when(valid)
    def _():
        acc_scr[...] += _swiglu_chunk(hb_scr[...], wg_ref[...].astype(BF16),
                                      wu_ref[...].astype(BF16), wd_ref[...].astype(BF16))

    @pl.when(valid & (f == nf - 1))
    def _():
        y_ref[...] = acc_scr[...] * xs_ref[:, d:d + 1]

    @pl.when(jnp.logical_not(valid) & (f == nf - 1))
    def _():
        y_ref[...] = jnp.zeros_like(y_ref)


def _expert_call(plan, xs, w_gu, w_dn, rows, cols):
    n_rows, width = xs.shape
    d = width - LANES
    nt = n_rows // rows
    ff = w_dn.shape[1]
    fc = cols
    nf = ff // fc
    tile = lambda i, nv: jnp.minimum(i, nv[0] - 1)
    chunk = lambda i, f, nv: jnp.where(i < nv[0], f, nf - 1)
    grid_spec = pltpu.PrefetchScalarGridSpec(
        num_scalar_prefetch=2,
        grid=(nt, nf),
        in_specs=[
            pl.BlockSpec((rows, width), lambda i, f, te, nv: (tile(i, nv), 0)),
            pl.BlockSpec((None, d, fc), lambda i, f, te, nv: (te[tile(i, nv)], 0, chunk(i, f, nv))),
            pl.BlockSpec((None, d, fc),
                         lambda i, f, te, nv: (te[tile(i, nv)], 0, nf + chunk(i, f, nv))),
            pl.BlockSpec((None, fc, d), lambda i, f, te, nv: (te[tile(i, nv)], chunk(i, f, nv), 0)),
        ],
        out_specs=pl.BlockSpec((rows, d), lambda i, f, te, nv: (i, 0)),
        scratch_shapes=[pltpu.VMEM((rows, d), BF16), pltpu.VMEM((rows, d), F32)],
    )
    return pl.pallas_call(
        _expert_kernel,
        grid_spec=grid_spec,
        out_shape=jax.ShapeDtypeStruct((n_rows, d), F32),
        compiler_params=_cparams(("arbitrary", "arbitrary"), VMEM_LIMIT_BYTES),
        name="expert_swiglu",
    )(plan["tile_expert"], plan["n_valid"], xs, w_gu, w_gu, w_dn)


def _combine_kernel(loff_ref, goff_ref, cpad_ref, dcol_ref, y_hbm, x_ref, mod_ref, g_ref, b_ref,
                    out_ref, ybuf, sems):
    w, nw = pl.program_id(0), pl.num_programs(0)
    slot = w % 2
    window, d = x_ref.shape
    r = ybuf.shape[1]
    segs = (loff_ref, goff_ref, cpad_ref)

    def copy(buf_slot, lrow, grow, size):
        return pltpu.make_async_copy(y_hbm.at[pl.ds(grow, size)],
                                     ybuf.at[buf_slot, pl.ds(lrow, size)], sems.at[buf_slot])

    @pl.when(w == 0)
    def _():
        ybuf[...] = jnp.zeros_like(ybuf)
        _for_each_piece(segs, 0, window, lambda l, g, s: copy(0, l, g, s).start())

    @pl.when(w + 1 < nw)
    def _():
        _for_each_piece(segs, w + 1, window, lambda l, g, s: copy(1 - slot, l, g, s).start())

    _for_each_piece(segs, w, window, lambda l, g, s: copy(slot, l, g, s).wait())

    lane = lax.broadcasted_iota(jnp.int32, (window, r), 1)
    hit = (lane == dcol_ref[:, 0:1]) | (lane == dcol_ref[:, 1:2])
    perm = jnp.where(hit, 1.0, 0.0).astype(BF16)
    hi, lo = _split_bf16(ybuf[slot])
    f = (jnp.dot(perm, hi, preferred_element_type=F32)
         + jnp.dot(perm, lo, preferred_element_type=F32))
    out_ref[...] = _layer_norm(ALPHA * x_ref[...] + _mod_slice(mod_ref, 5, d) * f,
                               g_ref[...], b_ref[...])


def _combine_call(plan, y, x, mod, ln_g, ln_b, window):
    b, l, d = x.shape
    wpb = l // window
    r = _window_rows(window)
    grid_spec = pltpu.PrefetchScalarGridSpec(
        num_scalar_prefetch=3,
        grid=(b * wpb,),
        in_specs=[
            pl.BlockSpec((window, TOP_K), lambda w, *_: (w, 0)),
            pl.BlockSpec(memory_space=pl.ANY),
            pl.BlockSpec((None, window, d), lambda w, *_: (w // wpb, w % wpb, 0)),
            pl.BlockSpec((None, 1, 6 * d), lambda w, *_: (w // wpb, 0, 0)),
            pl.BlockSpec((1, d), lambda w, *_: (0, 0)),
            pl.BlockSpec((1, d), lambda w, *_: (0, 0)),
        ],
        out_specs=pl.BlockSpec((None, window, d), lambda w, *_: (w // wpb, w % wpb, 0)),
        scratch_shapes=[pltpu.VMEM((2, r, d), F32), pltpu.SemaphoreType.DMA((2,))],
    )
    return pl.pallas_call(
        _combine_kernel,
        grid_spec=grid_spec,
        out_shape=jax.ShapeDtypeStruct((b, l, d), F32),
        compiler_params=_cparams(("arbitrary",), VMEM_LIMIT_BYTES),
        name="expert_combine",
    )(plan["loff"], plan["goff"], plan["cpad"], plan["dloc_cols"], y, x, mod, ln_g, ln_b)


def _dispatch_plan(info, counts, window, rows):
    n = info.shape[0]
    nw = n // window
    as_int = lambda lane0: info[:, lane0:lane0 + TOP_K].astype(jnp.int32).reshape(nw, window, TOP_K)
    dloc = as_int(ROUTE_ROW_LANE)
    count = counts[:, 0, :N_EXPERTS].astype(jnp.int32)
    cpad = -(-count // SEG_ALIGN) * SEG_ALIGN
    loff = jnp.cumsum(cpad, axis=1) - cpad
    group = -(-jnp.sum(cpad, axis=0) // rows) * rows
    gend = jnp.cumsum(group)
    goff = (gend - group)[None, :] + jnp.cumsum(cpad, axis=0) - cpad
    n_tiles = -(-(TOP_K * n + nw * MAX_SEG_PAD) // rows) + N_EXPERTS
    tile_start = jnp.arange(n_tiles) * rows
    tile_expert = jnp.minimum(jnp.sum(tile_start[:, None] >= gend[None, :], axis=1), N_EXPERTS - 1)
    wts = info[:, ROUTE_W_LANE:ROUTE_W_LANE + TOP_K].reshape(nw, window, TOP_K)
    flat = lambda a: a.reshape(-1).astype(jnp.int32)
    return dict(
        loff=flat(loff), goff=flat(goff), cpad=flat(cpad),
        dloc_rows=jnp.transpose(dloc, (0, 2, 1)).astype(jnp.int32),
        dloc_cols=dloc.reshape(n, TOP_K).astype(jnp.int32),
        wts_rows=jnp.transpose(wts, (0, 2, 1)),
        tail=flat(goff[-1] + cpad[-1]), tail_len=flat(gend - goff[-1] - cpad[-1]),
        rest=flat(gend[-1:]),
        tile_expert=tile_expert.astype(jnp.int32),
        n_valid=(gend[-1] // rows).reshape(1).astype(jnp.int32),
        n_rows=n_tiles * rows,
    )


def _attention_layer(x, ctx, mod_x, mod_c, ln_g, ln_b, w_qkv, w_o, sink, w_gu, w_dn, grid_w, ts):
    b, l, d = x.shape
    lc = ctx.shape[1]
    g0, b0, g1, b1 = ln_g[0:1], ln_b[0:1], ln_g[1:2], ln_b[1:2]
    w_qkv, w_o, w_gu, w_dn = (w.astype(BF16) for w in (w_qkv, w_o, w_gu, w_dn))
    ctx_flat = ctx.reshape(1, b * lc, d)
    n_kv = N_KV_HEADS * HEAD_DIM

    q, k, v = _qkv_call(x, mod_x, w_qkv, _rope_tables(l, grid_w), ts["qkv_rows"])
    qc, kc, vc = _qkv_call(ctx_flat, mod_c, w_qkv, _no_rope_tables(b * lc), ts["qkv_rows"])
    qc = qc.reshape(b, lc, -1)
    kc, vc = kc.reshape(b, lc, n_kv), vc.reshape(b, lc, n_kv)
    mod_cb = jnp.broadcast_to(mod_c, (b,) + mod_c.shape[1:])

    x = _attn_call(q, k, v, kc, vc, sink, w_o, x, mod_x, g0, b0, band=True)
    ctx = _attn_call(qc, None, None, kc, vc, sink, w_o, ctx, mod_cb, g0, b0, band=False)

    x = _ffn_call(x, mod_x, w_gu, w_dn, g1, b1, ts["ffn_rows"], ts["ffn_cols"])
    ctx = _ffn_call(ctx.reshape(1, b * lc, d), mod_c, w_gu, w_dn, g1, b1,
                    ts["ffn_rows"], ts["ffn_cols"]).reshape(b, lc, d)
    return x, ctx


def _recurrent_layer(x, ctx, mod_x, mod_c, ln_g, ln_b, w_in, conv_w, conv_b, lam, w_r, b_r, w_i,
                     b_i, w_out, router, w_gu, w_dn, ts):
    b, l, d = x.shape
    g0, b0, g1, b1 = ln_g[0:1], ln_b[0:1], ln_g[1:2], ln_b[1:2]
    w_in, w_out = w_in.astype(BF16), w_out.astype(BF16)
    wd = _paired_block_diag(w_r, w_i)
    conv_b = conv_b.reshape(1, -1)
    mod_cb = jnp.broadcast_to(mod_c, (b,) + mod_c.shape[1:])

    r = w_out.shape[0]
    rg = r // LRU_GROUPS
    gate, *xb = _inproj_call(x, mod_x, w_in, ts["proj_rows"])
    _, *xb_c = _inproj_call(ctx, mod_cb, w_in, ts["proj_rows"])
    zero = jnp.zeros((b, rg), F32)
    hf, hb = [], []
    for g in range(LRU_GROUPS):
        cs = slice(g * rg, (g + 1) * rg)
        pairs = slice(g * rg // (2 * LRU_BLOCK_DIM), (g + 1) * rg // (2 * LRU_BLOCK_DIM))
        scan = functools.partial(_scan_call, conv_w=conv_w[:, cs], conv_b=conv_b[:, cs],
                                 wd=wd[:, pairs], b_r=0.5 * b_r[:, cs], b_i=0.5 * b_i[:, cs],
                                 lam=lam[:, cs], steps=ts["scan_steps"])
        _, _, cf, cb = scan(xb_c[g].reshape(-1, b, rg), zero, zero)
        hf_g, hb_g, _, _ = scan(xb[g].reshape(l, b, rg), cf, cb)
        hf.append(hf_g.reshape(l, b * rg))
        hb.append(hb_g.reshape(l, b * rg))
    router_padded = jnp.concatenate(
        _split_bf16(jnp.pad(router, ((0, 0), (0, LANES - N_EXPERTS)))), axis=1)
    x, h, info, counts = _outproj_call(gate, hf, hb, w_out, x, mod_x, g0, b0, router_padded,
                                       ts["moe_window"])
    plan = _dispatch_plan(info, counts, ts["moe_window"], ts["moe_rows"])
    xs = _dispatch_call(plan, h, ts["moe_window"], ts["moe_rows"], plan["n_rows"])
    y = _expert_call(plan, xs, w_gu, w_dn, ts["moe_rows"], ts["ffn_cols"])
    return _combine_call(plan, y, x, mod_x, g1, b1, ts["moe_window"])


def kernel(x, c, ctx, c_ctx, w_mod, b_mod, ln_g, ln_b, attn_w_qkv, attn_w_o, attn_sink, ffn_w_gu,
           ffn_w_dn, lru_w_in, lru_conv_w, lru_conv_b, lru_lambda, lru_w_r, lru_b_r, lru_w_i,
           lru_b_i, lru_w_out, moe_router, moe_w_gu, moe_w_dn):
    b, l, d = x.shape
    assert w_mod.shape[0] == DEPTH and l % BLOCK == 0 and ctx.shape[1] % BLOCK == 0
    grid_w = 64
    ts = _tile_sizes()

    rows = -(-(b + 1) // SUBLANES) * SUBLANES
    cond = jnp.zeros((rows, d), F32).at[:b].set(c).at[b].set(c_ctx)
    mod = _mod_call(cond, w_mod, b_mod[:, None, :])
    mod_x = lambda i: mod[i, :b, None, :]
    mod_c = lambda i: mod[i, b:b + 1, None, :]

    x, ctx = _attention_layer(x, ctx, mod_x(0), mod_c(0), ln_g[0], ln_b[0], attn_w_qkv[0],
                              attn_w_o[0], attn_sink[0], ffn_w_gu[0], ffn_w_dn[0], grid_w, ts)
    return _recurrent_layer(x, ctx, mod_x(1), mod_c(1), ln_g[1], ln_b[1], lru_w_in[0],
                            lru_conv_w[0], lru_conv_b[0], lru_lambda[0], lru_w_r[0], lru_b_r[0],
                            lru_w_i[0], lru_b_i[0], lru_w_out[0], moe_router[0], moe_w_gu[0],
                            moe_w_dn[0], ts)
```

```python
import functools

import jax
import jax.numpy as jnp
import numpy as np
from jax import lax
from jax.experimental import pallas as pl
from jax.experimental.pallas import tpu as pltpu

N_HEADS = 8
N_KV_HEADS = 2
GROUP = N_HEADS // N_KV_HEADS
HEAD_DIM = 128
BLOCK = 128
AXIS_DIM = HEAD_DIM // 2
ROPE_BASE = 10000.0
LRU_BLOCKS = 8
LRU_BLOCK_DIM = 128
LRU_C = 8.0
N_EXPERTS = 8
TOP_K = 2
DEPTH = 2
ALPHA = (2 * DEPTH) ** 0.25
LN_EPS = 1e-5
NEG_INF = -1e30

LANES = 128
SUBLANES = 8
VMEM_LIMIT_BYTES = 56 * 1024 * 1024

F32 = jnp.float32
BF16 = jnp.bfloat16


def _tile_sizes():
    return dict(
        qkv_rows=1024,
        ffn_rows=1024,
        ffn_cols=512,
        proj_rows=1024,
        scan_steps=128,
        moe_window=512,
        moe_rows=1024,
    )


def _cparams(sem, vmem=None):
    return pltpu.CompilerParams(dimension_semantics=sem, vmem_limit_bytes=vmem)


def _layer_norm(h, g, b):
    mu = jnp.mean(h, axis=-1, keepdims=True)
    d = h - mu
    var = jnp.mean(d * d, axis=-1, keepdims=True)
    return d * lax.rsqrt(var + LN_EPS) * g + b


def _mod_slice(mod_ref, k, d):
    return mod_ref[:, k * d:(k + 1) * d]


def _mod_kernel(c_ref, w_ref, b_ref, o_ref):
    c = c_ref[...]
    s = c * jax.nn.sigmoid(c)
    o_ref[...] = jnp.dot(s, w_ref[...], preferred_element_type=F32,
                         precision=lax.Precision.HIGHEST) + b_ref[...]


def _mod_call(cond, w_mod, b_mod):
    rows, d = cond.shape
    depth, _, n = w_mod.shape
    tn = 1536
    return pl.pallas_call(
        _mod_kernel,
        grid=(depth, n // tn),
        in_specs=[
            pl.BlockSpec((rows, d), lambda i, j: (0, 0)),
            pl.BlockSpec((None, d, tn), lambda i, j: (i, 0, j)),
            pl.BlockSpec((None, 1, tn), lambda i, j: (i, 0, j)),
        ],
        out_specs=pl.BlockSpec((None, rows, tn), lambda i, j: (i, 0, j)),
        out_shape=jax.ShapeDtypeStruct((depth, rows, n), F32),
        compiler_params=_cparams(("arbitrary", "arbitrary")),
        name="mod_vectors",
    )(cond, w_mod, b_mod)


def _qkv_kernel(x_ref, mod_ref, w_ref, cos_ref, sa_ref, sb_ref, q_ref, k_ref, v_ref):
    d = x_ref.shape[-1]
    h = x_ref[...] * (1.0 + _mod_slice(mod_ref, 1, d)) + _mod_slice(mod_ref, 0, d)
    qkv = jnp.dot(h.astype(BF16), w_ref[...], preferred_element_type=F32)
    cos, sa, sb = cos_ref[...], sa_ref[...], sb_ref[...]
    scale = HEAD_DIM ** -0.5
    for head in range(N_HEADS + N_KV_HEADS):
        u = qkv[:, head * HEAD_DIM:(head + 1) * HEAD_DIM]
        r = (u * cos + pltpu.roll(u, HEAD_DIM - AXIS_DIM // 2, axis=1) * sa
             + pltpu.roll(u, AXIS_DIM // 2, axis=1) * sb)
        if head < N_HEADS:
            q_ref[:, head * HEAD_DIM:(head + 1) * HEAD_DIM] = (r * scale).astype(BF16)
        else:
            kh = head - N_HEADS
            k_ref[:, kh * HEAD_DIM:(kh + 1) * HEAD_DIM] = r.astype(BF16)
    kv = N_KV_HEADS * HEAD_DIM
    v_ref[...] = qkv[:, N_HEADS * HEAD_DIM + kv:].astype(BF16)


def _qkv_call(x, mod, w_qkv, tabs, rows):
    b, l, d = x.shape
    n_q, n_kv = N_HEADS * HEAD_DIM, N_KV_HEADS * HEAD_DIM
    tm = min(rows, l)
    tab_spec = pl.BlockSpec((tm, HEAD_DIM), lambda i, j: (j, 0))
    return pl.pallas_call(
        _qkv_kernel,
        grid=(b, l // tm),
        in_specs=[
            pl.BlockSpec((None, tm, d), lambda i, j: (i, j, 0)),
            pl.BlockSpec((None, 1, 6 * d), lambda i, j: (i, 0, 0)),
            pl.BlockSpec(w_qkv.shape, lambda i, j: (0, 0)),
            tab_spec, tab_spec, tab_spec,
        ],
        out_specs=[
            pl.BlockSpec((None, tm, n_q), lambda i, j: (i, j, 0)),
            pl.BlockSpec((None, tm, n_kv), lambda i, j: (i, j, 0)),
            pl.BlockSpec((None, tm, n_kv), lambda i, j: (i, j, 0)),
        ],
        out_shape=[
            jax.ShapeDtypeStruct((b, l, n_q), BF16),
            jax.ShapeDtypeStruct((b, l, n_kv), BF16),
            jax.ShapeDtypeStruct((b, l, n_kv), BF16),
        ],
        compiler_params=_cparams(("arbitrary", "arbitrary"), VMEM_LIMIT_BYTES),
        name="qkv_rope",
    )(x, mod, w_qkv, *tabs)


def _rope_tables(l, grid_w):
    f32 = np.float32
    t = np.arange(l)
    pos = np.stack([(t // grid_w).astype(f32), (t % grid_w).astype(f32)], axis=1)
    freqs = f32(ROPE_BASE) ** (-np.arange(0, AXIS_DIM, 2, dtype=f32) / f32(AXIS_DIM))
    lane = np.arange(HEAD_DIM)
    ang = (pos[:, lane // AXIS_DIM] * freqs[lane % (AXIS_DIM // 2)][None, :]).astype(f32)
    first = (lane % AXIS_DIM) < AXIS_DIM // 2
    cos, sin = np.cos(ang).astype(f32), np.sin(ang).astype(f32)
    zero = f32(0.0)
    return cos, np.where(first, -sin, zero), np.where(first, zero, sin)


def _no_rope_tables(l):
    z = np.zeros((l, HEAD_DIM), np.float32)
    return np.ones((l, HEAD_DIM), np.float32), z, z


def _nt_dot(a, b):
    return lax.dot_general(a, b, (((1,), (1,)), ((), ())), preferred_element_type=F32)


def _attend(qs, segs, sink_col):
    k_all = jnp.concatenate([k for k, _, _ in segs], axis=0)
    v_all = jnp.concatenate([v for _, v, _ in segs], axis=0)
    v_ext = jnp.concatenate([v_all, jnp.ones_like(v_all)], axis=1)
    s_all = _nt_dot(qs, k_all)
    tiles, col = [], 0
    for k, _, mask in segs:
        for c in range(col, col + k.shape[0], LANES):
            s = s_all[:, c:c + LANES]
            tiles.append(s if mask is None else jnp.where(mask, s, NEG_INF))
        col += k.shape[0]
    m_tile = tiles[0]
    for s in tiles[1:]:
        m_tile = jnp.maximum(m_tile, s)
    m = jnp.maximum(sink_col, jnp.max(m_tile, axis=-1, keepdims=True))
    p_all = jnp.concatenate([jnp.exp(s - m).astype(BF16) for s in tiles], axis=1)
    res = jnp.dot(p_all, v_ext, preferred_element_type=F32)
    hd = v_all.shape[1]
    return res[:, :hd] / (res[:, hd:hd + 1] + jnp.exp(sink_col - m))


def _attn_epilogue(o_scr, wo_ref, x_ref, mod_ref, g_ref, b_ref, out_ref):
    d = x_ref.shape[-1]
    proj = jnp.dot(o_scr[...], wo_ref[...], preferred_element_type=F32)
    out_ref[...] = _layer_norm(ALPHA * x_ref[...] + _mod_slice(mod_ref, 2, d) * proj,
                               g_ref[...], b_ref[...])


def _sink_column(sink_ref, kvh):
    return jnp.concatenate(
        [jnp.full((BLOCK, 1), sink_ref[kvh * GROUP + g], F32) for g in range(GROUP)], axis=0)


def _stack_group(q_ref, kvh):
    return jnp.concatenate(
        [q_ref[:, (kvh * GROUP + g) * HEAD_DIM:(kvh * GROUP + g + 1) * HEAD_DIM]
         for g in range(GROUP)], axis=0)


def _unstack_group(o, o_scr, kvh):
    for g in range(GROUP):
        col = (kvh * GROUP + g) * HEAD_DIM
        o_scr[:, col:col + HEAD_DIM] = o[g * BLOCK:(g + 1) * BLOCK].astype(BF16)


def _band_attn_kernel(sink_ref, q_ref, kp_ref, kc_ref, kn_ref, vp_ref, vc_ref, vn_ref,
                      kx_ref, vx_ref, wo_ref, x_ref, mod_ref, g_ref, b_ref, out_ref, o_scr):
    n, nb = pl.program_id(1), pl.num_programs(1)
    row = lax.broadcasted_iota(jnp.int32, (GROUP * BLOCK, BLOCK), 0) % BLOCK
    col = lax.broadcasted_iota(jnp.int32, (GROUP * BLOCK, BLOCK), 1)
    mask_prev = (col >= row) & (n > 0)
    mask_next = (col <= row) & (n < nb - 1)
    for kvh in range(N_KV_HEADS):
        hs = slice(kvh * HEAD_DIM, (kvh + 1) * HEAD_DIM)
        segs = [(kp_ref[:, hs], vp_ref[:, hs], mask_prev),
                (kc_ref[:, hs], vc_ref[:, hs], None),
                (kn_ref[:, hs], vn_ref[:, hs], mask_next),
                (kx_ref[:, hs], vx_ref[:, hs], None)]
        o = _attend(_stack_group(q_ref, kvh), segs, _sink_column(sink_ref, kvh))
        _unstack_group(o, o_scr, kvh)
    _attn_epilogue(o_scr, wo_ref, x_ref, mod_ref, g_ref, b_ref, out_ref)


def _ctx_attn_kernel(sink_ref, q_ref, kx_ref, vx_ref, wo_ref, x_ref, mod_ref, g_ref, b_ref,
                     out_ref, o_scr):
    for kvh in range(N_KV_HEADS):
        hs = slice(kvh * HEAD_DIM, (kvh + 1) * HEAD_DIM)
        o = _attend(_stack_group(q_ref, kvh), [(kx_ref[:, hs], vx_ref[:, hs], None)],
                    _sink_column(sink_ref, kvh))
        _unstack_group(o, o_scr, kvh)
    _attn_epilogue(o_scr, wo_ref, x_ref, mod_ref, g_ref, b_ref, out_ref)


def _attn_call(q, k, v, kx, vx, sink, w_o, x, mod, ln_g, ln_b, band):
    b, l, d = x.shape
    lc = kx.shape[1]
    nb = l // BLOCK
    n_q, n_kv = N_HEADS * HEAD_DIM, N_KV_HEADS * HEAD_DIM
    kv_blk = lambda f: pl.BlockSpec((None, BLOCK, n_kv), f)
    common_in = [
        pl.BlockSpec((None, lc, n_kv), lambda i, j: (i, 0, 0)),
        pl.BlockSpec((None, lc, n_kv), lambda i, j: (i, 0, 0)),
        pl.BlockSpec(w_o.shape, lambda i, j: (0, 0)),
        pl.BlockSpec((None, BLOCK, d), lambda i, j: (i, j, 0)),
        pl.BlockSpec((None, 1, 6 * d), lambda i, j: (i, 0, 0)),
        pl.BlockSpec((1, d), lambda i, j: (0, 0)),
        pl.BlockSpec((1, d), lambda i, j: (0, 0)),
    ]
    head_in = [pl.BlockSpec(memory_space=pltpu.SMEM),
               pl.BlockSpec((None, BLOCK, n_q), lambda i, j: (i, j, 0))]
    if band:
        prev = lambda i, j: (i, jnp.maximum(j - 1, 0), 0)
        cur = lambda i, j: (i, j, 0)
        nxt = lambda i, j: (i, jnp.minimum(j + 1, nb - 1), 0)
        in_specs = head_in + [kv_blk(prev), kv_blk(cur), kv_blk(nxt)] * 2 + common_in
        args = (sink, q, k, k, k, v, v, v, kx, vx, w_o, x, mod, ln_g, ln_b)
        body, name = _band_attn_kernel, "band_attention"
    else:
        in_specs = head_in + common_in
        args = (sink, q, kx, vx, w_o, x, mod, ln_g, ln_b)
        body, name = _ctx_attn_kernel, "context_attention"
    return pl.pallas_call(
        body,
        grid=(b, nb),
        in_specs=in_specs,
        out_specs=pl.BlockSpec((None, BLOCK, d), lambda i, j: (i, j, 0)),
        out_shape=jax.ShapeDtypeStruct((b, l, d), F32),
        scratch_shapes=[pltpu.VMEM((BLOCK, n_q), BF16)],
        compiler_params=_cparams(("arbitrary", "arbitrary"), VMEM_LIMIT_BYTES),
        name=name,
    )(*args)


def _swiglu_chunk(hb, wg, wu, wd):
    g = jnp.dot(hb, wg, preferred_element_type=F32)
    u = jnp.dot(hb, wu, preferred_element_type=F32)
    a = (g * jax.nn.sigmoid(g)) * u
    return jnp.dot(a.astype(BF16), wd, preferred_element_type=F32)


def _ffn_kernel(xc_ref, xp_ref, modc_ref, modp_ref, wg_ref, wu_ref, wd_ref, g_ref, b_ref, out_ref,
                hb_scr, acc_scr):
    i, f = pl.program_id(0), pl.program_id(1)
    nt = pl.num_programs(0) - 1
    d = xc_ref.shape[-1]

    def finish_previous():
        out_ref[...] = _layer_norm(
            ALPHA * xp_ref[...] + _mod_slice(modp_ref, 5, d) * acc_scr[...], g_ref[...], b_ref[...])

    def start_current():
        h = xc_ref[...] * (1.0 + _mod_slice(modc_ref, 4, d)) + _mod_slice(modc_ref, 3, d)
        hb = h.astype(BF16)
        hb_scr[...] = hb
        acc_scr[...] = _swiglu_chunk(hb, wg_ref[...], wu_ref[...], wd_ref[...])

    @pl.when((f == 0) & (i == 0))
    def _():
        start_current()

    @pl.when((f == 0) & (i > 0) & (i < nt))
    def _():
        finish_previous()
        start_current()

    @pl.when((f == 0) & (i == nt))
    def _():
        finish_previous()

    @pl.when((f > 0) & (i < nt))
    def _():
        acc_scr[...] += _swiglu_chunk(hb_scr[...], wg_ref[...], wu_ref[...], wd_ref[...])


def _ffn_call(x, mod, w_gu, w_dn, ln_g, ln_b, rows, cols):
    b, l, d = x.shape
    ff = w_dn.shape[0]
    tm, fc = min(rows, l), cols
    nf = ff // fc
    lt = l // tm
    nt = b * lt
    cur = lambda i: jnp.minimum(i, nt - 1)
    prev = lambda i: jnp.maximum(i - 1, 0)
    chunk = lambda i, f: jnp.where(i < nt, f, nf - 1)
    x_spec = lambda t: pl.BlockSpec((None, tm, d), lambda i, f: (t(i) // lt, t(i) % lt, 0))
    mod_spec = lambda t: pl.BlockSpec((None, 1, 6 * d), lambda i, f: (t(i) // lt, 0, 0))
    return pl.pallas_call(
        _ffn_kernel,
        grid=(nt + 1, nf),
        in_specs=[
            x_spec(cur), x_spec(prev), mod_spec(cur), mod_spec(prev),
            pl.BlockSpec((d, fc), lambda i, f: (0, chunk(i, f))),
            pl.BlockSpec((d, fc), lambda i, f: (0, nf + chunk(i, f))),
            pl.BlockSpec((fc, d), lambda i, f: (chunk(i, f), 0)),
            pl.BlockSpec((1, d), lambda i, f: (0, 0)),
            pl.BlockSpec((1, d), lambda i, f: (0, 0)),
        ],
        out_specs=x_spec(prev),
        out_shape=jax.ShapeDtypeStruct((b, l, d), F32),
        scratch_shapes=[pltpu.VMEM((tm, d), BF16), pltpu.VMEM((tm, d), F32)],
        compiler_params=_cparams(("arbitrary", "arbitrary"), VMEM_LIMIT_BYTES),
        name="dense_swiglu",
    )(x, x, mod, mod, w_gu, w_gu, w_dn, ln_g, ln_b)


LRU_GROUPS = 2
SCAN_UNROLL = 8


def _inproj_kernel(x_ref, mod_ref, w_ref, gate_ref, *xb_refs):
    d = x_ref.shape[-1]
    r = gate_ref.shape[-1]
    rg = r // len(xb_refs)
    h = x_ref[...] * (1.0 + _mod_slice(mod_ref, 1, d)) + _mod_slice(mod_ref, 0, d)
    y = jnp.dot(h.astype(BF16), w_ref[...], preferred_element_type=F32)
    gate_ref[...] = y[:, :r]
    for g, xb_ref in enumerate(xb_refs):
        xb_ref[...] = y[:, r + g * rg:r + (g + 1) * rg]


def _inproj_call(x, mod, w_in, rows):
    b, l, d = x.shape
    r = w_in.shape[1] // 2
    rg = r // LRU_GROUPS
    tm = min(rows, l)
    return pl.pallas_call(
        _inproj_kernel,
        grid=(b, l // tm),
        in_specs=[
            pl.BlockSpec((None, tm, d), lambda i, j: (i, j, 0)),
            pl.BlockSpec((None, 1, 6 * d), lambda i, j: (i, 0, 0)),
            pl.BlockSpec(w_in.shape, lambda i, j: (0, 0)),
        ],
        out_specs=[pl.BlockSpec((None, tm, r), lambda i, j: (i, j, 0))]
        + [pl.BlockSpec((tm, rg), lambda i, j: (j, i))] * LRU_GROUPS,
        out_shape=[jax.ShapeDtypeStruct((b, l, r), F32)]
        + [jax.ShapeDtypeStruct((l, b * rg), F32)] * LRU_GROUPS,
        compiler_params=_cparams(("arbitrary", "arbitrary"), VMEM_LIMIT_BYTES),
        name="lru_in_proj",
    )(x, mod, w_in)


def _lru_coeffs(main_ref, prev_ref, next_ref, first, last, cw_ref, cb_ref, wd_ref, br_ref, bi_ref,
                lam_ref, a_scr, b_scr):
    tl, nb, r = main_ref.shape
    prev = jnp.where(first, 0.0, prev_ref[...])
    nxt = jnp.where(last, 0.0, next_ref[...])
    ext = jnp.concatenate([prev, main_ref[...], nxt], axis=0)
    u = cb_ref[...][None]
    for k in range(4):
        u = u + cw_ref[k:k + 1, :][None] * ext[k:k + tl]
    u = u.reshape(tl * nb, r)
    ub = u.astype(BF16)
    lam = lam_ref[...]
    softplus_neg = jnp.maximum(-lam, 0.0) + jnp.log1p(jnp.exp(-jnp.abs(lam)))
    half_rate = (-0.5 * LRU_C) * softplus_neg
    pw = 2 * LRU_BLOCK_DIM
    for p in range(r // pw):
        cs = slice(p * pw, (p + 1) * pw)
        z = jnp.dot(ub[:, cs], wd_ref[p], preferred_element_type=F32)
        tr = jnp.tanh(z[:, :pw] + br_ref[:, cs])
        ig = 0.5 + 0.5 * jnp.tanh(z[:, pw:] + bi_ref[:, cs])
        log_a = half_rate[:, cs] * tr + half_rate[:, cs]
        a = jnp.exp(log_a)
        gain = jnp.sqrt(-jnp.tanh(log_a) * (a * a + 1.0))
        a_scr[:, :, cs] = a.reshape(tl, nb, pw)
        b_scr[:, :, cs] = (gain * (ig * u[:, cs])).reshape(tl, nb, pw)


def _scan_kernel(fm_ref, fp_ref, fn_ref, bm_ref, bp_ref, bn_ref, h0f_ref, h0b_ref,
                 cw_ref, cb_ref, wdf_ref, wdb_ref, br_ref, bi_ref, lam_ref,
                 hf_ref, hb_ref, lastf_ref, lastb_ref,
                 af_scr, bf_scr, ab_scr, bb_scr, sf_scr, sb_scr):
    c, nc = pl.program_id(0), pl.num_programs(0)
    tl = fm_ref.shape[0]

    @pl.when(c == 0)
    def _():
        sf_scr[...] = h0f_ref[...]
        sb_scr[...] = h0b_ref[...]

    _lru_coeffs(fm_ref, fp_ref, fn_ref, c == 0, c == nc - 1, cw_ref, cb_ref, wdf_ref,
                br_ref.at[0:1], bi_ref.at[0:1], lam_ref.at[0:1], af_scr, bf_scr)
    _lru_coeffs(bm_ref, bp_ref, bn_ref, c == nc - 1, c == 0, cw_ref, cb_ref, wdb_ref,
                br_ref.at[1:2], bi_ref.at[1:2], lam_ref.at[1:2], ab_scr, bb_scr)

    def step(t, carry):
        hf, hb = carry
        hf = af_scr[t] * hf + bf_scr[t]
        hf_ref[t] = hf
        tb = tl - 1 - t
        hb = ab_scr[tb] * hb + bb_scr[tb]
        hb_ref[tb] = hb
        return hf, hb

    hf, hb = lax.fori_loop(0, tl, step, (sf_scr[...], sb_scr[...]), unroll=SCAN_UNROLL)
    sf_scr[...] = hf
    sb_scr[...] = hb

    @pl.when(c == nc - 1)
    def _():
        lastf_ref[...] = hf
        lastb_ref[...] = hb


def _scan_call(xb, h0f, h0b, conv_w, conv_b, wd, b_r, b_i, lam, steps):
    ls, nb, r = xb.shape
    tl = min(steps, ls)
    nc = ls // tl
    fwd, bwd = (lambda c: c), (lambda c: nc - 1 - c)
    main = lambda ch: pl.BlockSpec((tl, nb, r), lambda c: (ch(c), 0, 0))
    prev2 = lambda ch: pl.BlockSpec((2, nb, r), lambda c: (jnp.maximum(ch(c) * (tl // 2) - 1, 0), 0, 0))
    next1 = lambda ch: pl.BlockSpec((1, nb, r), lambda c: (jnp.minimum((ch(c) + 1) * tl, ls - 1), 0, 0))
    full = lambda a: pl.BlockSpec(a.shape, lambda c: (0,) * a.ndim)
    state = pl.BlockSpec((nb, r), lambda c: (0, 0))
    wd_spec = lambda dirn: pl.BlockSpec((None,) + wd.shape[1:], lambda c: (dirn, 0, 0, 0))
    return pl.pallas_call(
        _scan_kernel,
        grid=(nc,),
        in_specs=[main(fwd), prev2(fwd), next1(fwd), main(bwd), prev2(bwd), next1(bwd),
                  state, state, full(conv_w), full(conv_b), wd_spec(0), wd_spec(1),
                  full(b_r), full(b_i), full(lam)],
        out_specs=[main(fwd), main(bwd), state, state],
        out_shape=[jax.ShapeDtypeStruct((ls, nb, r), F32), jax.ShapeDtypeStruct((ls, nb, r), F32),
                   jax.ShapeDtypeStruct((nb, r), F32), jax.ShapeDtypeStruct((nb, r), F32)],
        scratch_shapes=[pltpu.VMEM((tl, nb, r), F32)] * 4 + [pltpu.VMEM((nb, r), F32)] * 2,
        compiler_params=_cparams(("arbitrary",), VMEM_LIMIT_BYTES),
        name="rglru_scan",
    )(xb, xb, xb, xb, xb, xb, h0f, h0b, conv_w, conv_b, wd, wd, b_r, b_i, lam)


def _paired_block_diag(w_r, w_i):
    def pair(w):
        ndir = w.shape[0]
        w = w.reshape(ndir, LRU_BLOCKS // 2, 2, LRU_BLOCK_DIM, LRU_BLOCK_DIM)
        z = jnp.zeros_like(w[:, :, 0])
        top = jnp.concatenate([w[:, :, 0], z], axis=-1)
        bot = jnp.concatenate([z, w[:, :, 1]], axis=-1)
        return jnp.concatenate([top, bot], axis=-2)
    return (0.5 * jnp.concatenate([pair(w_r), pair(w_i)], axis=-1)).astype(BF16)


ROUTE_IDX_LANE = N_EXPERTS
ROUTE_W_LANE = N_EXPERTS + TOP_K
ROUTE_ROW_LANE = N_EXPERTS + 2 * TOP_K


def _split_bf16(a):
    hi = a.astype(BF16)
    return hi, (a - hi.astype(F32)).astype(BF16)


def _route(h, wr_ref, info_ref, cnt_ref):
    h_hi, h_lo = _split_bf16(h)
    both = jnp.dot(h_hi, wr_ref[...], preferred_element_type=F32)
    logits = (both[:, :LANES] + both[:, LANES:]
              + jnp.dot(h_lo, wr_ref[:, :LANES], preferred_element_type=F32))
    lane = lax.broadcasted_iota(jnp.int32, logits.shape, 1)
    logits = jnp.where(lane < N_EXPERTS, logits, -jnp.inf)
    m1 = jnp.max(logits, axis=-1, keepdims=True)
    i1 = jnp.min(jnp.where(logits == m1, lane, LANES), axis=-1, keepdims=True)
    rest = jnp.where(lane == i1, -jnp.inf, logits)
    m2 = jnp.max(rest, axis=-1, keepdims=True)
    i2 = jnp.min(jnp.where(rest == m2, lane, LANES), axis=-1, keepdims=True)
    e2 = jnp.exp(m2 - m1)
    w1 = 1.0 / (1.0 + e2)
    w2 = e2 / (1.0 + e2)
    chosen = jnp.where((lane == i1) | (lane == i2), 1.0, 0.0)
    tm = h.shape[0]
    earlier = (lax.broadcasted_iota(jnp.int32, (tm, tm), 0)
               > lax.broadcasted_iota(jnp.int32, (tm, tm), 1))
    rank = jnp.dot(jnp.where(earlier, 1.0, 0.0).astype(BF16), chosen.astype(BF16),
                   preferred_element_type=F32)
    counts = jnp.sum(chosen, axis=0, keepdims=True)
    padded = jnp.ceil(counts * (1.0 / SEG_ALIGN)) * SEG_ALIGN
    before = (lax.broadcasted_iota(jnp.int32, (LANES, LANES), 0)
              < lax.broadcasted_iota(jnp.int32, (LANES, LANES), 1))
    start = jnp.dot(jnp.broadcast_to(padded, (SUBLANES, LANES)).astype(BF16),
                    jnp.where(before, 1.0, 0.0).astype(BF16), preferred_element_type=F32)[0:1]
    row = rank + start
    d1 = jnp.sum(jnp.where(lane == i1, row, 0.0), axis=-1, keepdims=True)
    d2 = jnp.sum(jnp.where(lane == i2, row, 0.0), axis=-1, keepdims=True)
    info = jnp.where(lane == i1, w1, jnp.where(lane == i2, w2, 0.0))
    for k, val in enumerate((i1.astype(F32), i2.astype(F32), w1, w2, d1, d2)):
        info = jnp.where(lane == ROUTE_IDX_LANE + k, val, info)
    info_ref[...] = info
    cnt_ref[...] = counts


def _outproj_kernel(gate_ref, *refs):
    state_refs, refs = refs[:2 * LRU_GROUPS], refs[2 * LRU_GROUPS:]
    w_ref, x_ref, mod_ref, g_ref, b_ref, wr_ref, out_ref, h_ref, info_ref, cnt_ref = refs
    d = x_ref.shape[-1]
    gt = gate_ref[...]
    gelu = 0.5 * gt * (1.0 + jnp.tanh(0.7978845608028654 * (gt + 0.044715 * (gt * gt * gt))))
    states = jnp.concatenate([state_refs[g][...] + state_refs[LRU_GROUPS + g][...]
                              for g in range(LRU_GROUPS)], axis=1)
    y = gelu * states
    proj = jnp.dot(y.astype(BF16), w_ref[...], preferred_element_type=F32)
    xn = _layer_norm(ALPHA * x_ref[...] + _mod_slice(mod_ref, 2, d) * proj, g_ref[...], b_ref[...])
    out_ref[...] = xn
    h = xn * (1.0 + _mod_slice(mod_ref, 4, d)) + _mod_slice(mod_ref, 3, d)
    h_ref[...] = h.astype(BF16)
    _route(h, wr_ref, info_ref, cnt_ref)


def _outproj_call(gate, hf, hb, w_out, x, mod, ln_g, ln_b, router_padded, rows):
    b, l, d = x.shape
    r = gate.shape[-1]
    tm = rows
    nt = l // tm
    tmajor = pl.BlockSpec((tm, r // LRU_GROUPS), lambda i, j: (j, i))
    return pl.pallas_call(
        _outproj_kernel,
        grid=(b, nt),
        in_specs=[
            pl.BlockSpec((None, tm, r), lambda i, j: (i, j, 0)),
            *([tmajor] * (2 * LRU_GROUPS)),
            pl.BlockSpec(w_out.shape, lambda i, j: (0, 0)),
            pl.BlockSpec((None, tm, d), lambda i, j: (i, j, 0)),
            pl.BlockSpec((None, 1, 6 * d), lambda i, j: (i, 0, 0)),
            pl.BlockSpec((1, d), lambda i, j: (0, 0)),
            pl.BlockSpec((1, d), lambda i, j: (0, 0)),
            pl.BlockSpec(router_padded.shape, lambda i, j: (0, 0)),
        ],
        out_specs=[
            pl.BlockSpec((None, tm, d), lambda i, j: (i, j, 0)),
            pl.BlockSpec((tm, d), lambda i, j: (i * nt + j, 0)),
            pl.BlockSpec((tm, LANES), lambda i, j: (i * nt + j, 0)),
            pl.BlockSpec((None, 1, LANES), lambda i, j: (i * nt + j, 0, 0)),
        ],
        out_shape=[jax.ShapeDtypeStruct((b, l, d), F32), jax.ShapeDtypeStruct((b * l, d), BF16),
                   jax.ShapeDtypeStruct((b * l, LANES), F32),
                   jax.ShapeDtypeStruct((b * nt, 1, LANES), F32)],
        compiler_params=_cparams(("arbitrary", "arbitrary"), VMEM_LIMIT_BYTES),
        name="lru_out_proj_route",
    )(gate, *hf, *hb, w_out, x, mod, ln_g, ln_b, router_padded)


SEG_ALIGN = SUBLANES
MAX_SEG_PAD = N_EXPERTS * (SEG_ALIGN - 1)
BF16_ROWS = 2 * SUBLANES


def _window_rows(window):
    return -(-(TOP_K * window + MAX_SEG_PAD) // BF16_ROWS) * BF16_ROWS


def _for_each_piece(seg_refs, w, window, fn):
    loff_ref, goff_ref, cpad_ref = seg_refs
    for e in range(N_EXPERTS):
        lo, go, c = (ref[w * N_EXPERTS + e] for ref in (loff_ref, goff_ref, cpad_ref))
        size = window
        while size >= SEG_ALIGN:
            done = c & ~(2 * size - 1)

            @pl.when((c & size) != 0)
            def _(lo=lo, go=go, done=done, size=size):
                fn(pl.multiple_of(lo + done, SEG_ALIGN), pl.multiple_of(go + done, SEG_ALIGN), size)

            size //= 2


def _for_each_tail_piece(tail_ref, len_ref, max_size, fn):
    for e in range(N_EXPERTS):
        go, c = tail_ref[e], len_ref[e]
        size = max_size
        while size >= SEG_ALIGN:
            done = c & ~(2 * size - 1)

            @pl.when((c & size) != 0)
            def _(go=go, done=done, size=size):
                fn(pl.multiple_of(go + done, SEG_ALIGN), size)

            size //= 2


def _dispatch_kernel(loff_ref, goff_ref, cpad_ref, tail_ref, tlen_ref, rest_ref, dloc_ref, wts_ref,
                     h_ref, xs_hbm, buf, zbuf, sems):
    w, nw = pl.program_id(0), pl.num_programs(0)
    slot = w % 2
    window, d = h_ref.shape
    r = buf.shape[1]
    zrows = zbuf.shape[0]
    segs = (loff_ref, goff_ref, cpad_ref)

    def copy(buf_slot, lrow, grow, size):
        return pltpu.make_async_copy(buf.at[buf_slot, pl.ds(lrow, size)],
                                     xs_hbm.at[pl.ds(grow, size)], sems.at[buf_slot])

    def fill(grow, size):
        return pltpu.make_async_copy(zbuf.at[pl.ds(0, size)], xs_hbm.at[pl.ds(grow, size)],
                                     sems.at[2])

    def for_each_fill(act):
        _for_each_tail_piece(tail_ref, tlen_ref, zrows, lambda g, s: act(fill(g, s)))

        def body(k, _):
            act(fill(pl.multiple_of(rest_ref[0] + k * zrows, SEG_ALIGN), zrows))
            return 0
        lax.fori_loop(0, (xs_hbm.shape[0] - rest_ref[0]) // zrows, body, 0)

    @pl.when(w == 0)
    def _():
        zbuf[...] = jnp.zeros_like(zbuf)
        for_each_fill(lambda cp: cp.start())

    row = lax.broadcasted_iota(jnp.int32, (r, window), 0)
    hit0 = row == dloc_ref[0:1, :]
    hit1 = row == dloc_ref[1:2, :]
    perm = jnp.where(hit0 | hit1, 1.0, 0.0).astype(BF16)
    buf[slot, :, :d] = jnp.dot(perm, h_ref[...], preferred_element_type=F32)
    roww = jnp.sum(jnp.where(hit0, wts_ref[0:1, :], 0.0) + jnp.where(hit1, wts_ref[1:2, :], 0.0),
                   axis=1, keepdims=True)
    buf[slot, :, d:] = jnp.broadcast_to(roww, (r, LANES))

    _for_each_piece(segs, w, window, lambda l, g, s: copy(slot, l, g, s).start())

    @pl.when(w > 0)
    def _():
        _for_each_piece(segs, w - 1, window, lambda l, g, s: copy(1 - slot, l, g, s).wait())

    @pl.when(w == nw - 1)
    def _():
        _for_each_piece(segs, w, window, lambda l, g, s: copy(slot, l, g, s).wait())
        for_each_fill(lambda cp: cp.wait())


def _dispatch_call(plan, h, window, rows, n_rows):
    n, d = h.shape
    nw = n // window
    r = _window_rows(window)
    width = d + LANES
    grid_spec = pltpu.PrefetchScalarGridSpec(
        num_scalar_prefetch=6,
        grid=(nw,),
        in_specs=[
            pl.BlockSpec((None, TOP_K, window), lambda w, *_: (w, 0, 0)),
            pl.BlockSpec((None, TOP_K, window), lambda w, *_: (w, 0, 0)),
            pl.BlockSpec((window, d), lambda w, *_: (w, 0)),
        ],
        out_specs=pl.BlockSpec(memory_space=pl.ANY),
        scratch_shapes=[pltpu.VMEM((2, r, width), F32), pltpu.VMEM((rows // 2, width), F32),
                        pltpu.SemaphoreType.DMA((3,))],
    )
    return pl.pallas_call(
        _dispatch_kernel,
        grid_spec=grid_spec,
        out_shape=jax.ShapeDtypeStruct((n_rows, width), F32),
        compiler_params=_cparams(("arbitrary",), VMEM_LIMIT_BYTES),
        name="expert_dispatch",
    )(plan["loff"], plan["goff"], plan["cpad"], plan["tail"], plan["tail_len"], plan["rest"],
      plan["dloc_rows"], plan["wts_rows"], h)


def _expert_kernel(te_ref, nv_ref, xs_ref, wg_ref, wu_ref, wd_ref, y_ref, hb_scr, acc_scr):
    i, f, nf = pl.program_id(0), pl.program_id(1), pl.num_programs(1)
    d = y_ref.shape[-1]
    valid = i < nv_ref[0]

    @pl.when(valid & (f == 0))
    def _():
        hb_scr[...] = xs_ref[:, :d].astype(BF16)
        acc_scr[...] = jnp.zeros_like(acc_scr)

    @pl.when(valid)
    def _():
        acc_scr[...] += _swiglu_chunk(hb_scr[...], wg_ref[...].astype(BF16),
                                      wu_ref[...].astype(BF16), wd_ref[...].astype(BF16))

    @pl.when(valid & (f == nf - 1))
    def _():
        y_ref[...] = acc_scr[...] * xs_ref[:, d:d + 1]

    @pl.when(jnp.logical_not(valid) & (f == nf - 1))
    def _():
        y_ref[...] = jnp.zeros_like(y_ref)


def _expert_call(plan, xs, w_gu, w_dn, rows, cols):
    n_rows, width = xs.shape
    d = width - LANES
    nt = n_rows // rows
    ff = w_dn.shape[1]
    fc = cols
    nf = ff // fc
    tile = lambda i, nv: jnp.minimum(i, nv[0] - 1)
    chunk = lambda i, f, nv: jnp.where(i < nv[0], f, nf - 1)
    grid_spec = pltpu.PrefetchScalarGridSpec(
        num_scalar_prefetch=2,
        grid=(nt, nf),
        in_specs=[
            pl.BlockSpec((rows, width), lambda i, f, te, nv: (tile(i, nv), 0)),
            pl.BlockSpec((None, d, fc), lambda i, f, te, nv: (te[tile(i, nv)], 0, chunk(i, f, nv))),
            pl.BlockSpec((None, d, fc),
                         lambda i, f, te, nv: (te[tile(i, nv)], 0, nf + chunk(i, f, nv))),
            pl.BlockSpec((None, fc, d), lambda i, f, te, nv: (te[tile(i, nv)], chunk(i, f, nv), 0)),
        ],
        out_specs=pl.BlockSpec((rows, d), lambda i, f, te, nv: (i, 0)),
        scratch_shapes=[pltpu.VMEM((rows, d), BF16), pltpu.VMEM((rows, d), F32)],
    )
    return pl.pallas_call(
        _expert_kernel,
        grid_spec=grid_spec,
        out_shape=jax.ShapeDtypeStruct((n_rows, d), F32),
        compiler_params=_cparams(("arbitrary", "arbitrary"), VMEM_LIMIT_BYTES),
        name="expert_swiglu",
    )(plan["tile_expert"], plan["n_valid"], xs, w_gu, w_gu, w_dn)


def _combine_kernel(loff_ref, goff_ref, cpad_ref, dcol_ref, y_hbm, x_ref, mod_ref, g_ref, b_ref,
                    out_ref, ybuf, sems):
    w, nw = pl.program_id(0), pl.num_programs(0)
    slot = w % 2
    window, d = x_ref.shape
    r = ybuf.shape[1]
    segs = (loff_ref, goff_ref, cpad_ref)

    def copy(buf_slot, lrow, grow, size):
        return pltpu.make_async_copy(y_hbm.at[pl.ds(grow, size)],
                                     ybuf.at[buf_slot, pl.ds(lrow, size)], sems.at[buf_slot])

    @pl.when(w == 0)
    def _():
        ybuf[...] = jnp.zeros_like(ybuf)
        _for_each_piece(segs, 0, window, lambda l, g, s: copy(0, l, g, s).start())

    @pl.when(w + 1 < nw)
    def _():
        _for_each_piece(segs, w + 1, window, lambda l, g, s: copy(1 - slot, l, g, s).start())

    _for_each_piece(segs, w, window, lambda l, g, s: copy(slot, l, g, s).wait())

    lane = lax.broadcasted_iota(jnp.int32, (window, r), 1)
    hit = (lane == dcol_ref[:, 0:1]) | (lane == dcol_ref[:, 1:2])
    perm = jnp.where(hit, 1.0, 0.0).astype(BF16)
    hi, lo = _split_bf16(ybuf[slot])
    f = (jnp.dot(perm, hi, preferred_element_type=F32)
         + jnp.dot(perm, lo, preferred_element_type=F32))
    out_ref[...] = _layer_norm(ALPHA * x_ref[...] + _mod_slice(mod_ref, 5, d) * f,
                               g_ref[...], b_ref[...])


def _combine_call(plan, y, x, mod, ln_g, ln_b, window):
    b, l, d = x.shape
    wpb = l // window
    r = _window_rows(window)
    grid_spec = pltpu.PrefetchScalarGridSpec(
        num_scalar_prefetch=3,
        grid=(b * wpb,),
        in_specs=[
            pl.BlockSpec((window, TOP_K), lambda w, *_: (w, 0)),
            pl.BlockSpec(memory_space=pl.ANY),
            pl.BlockSpec((None, window, d), lambda w, *_: (w // wpb, w % wpb, 0)),
            pl.BlockSpec((None, 1, 6 * d), lambda w, *_: (w // wpb, 0, 0)),
            pl.BlockSpec((1, d), lambda w, *_: (0, 0)),
            pl.BlockSpec((1, d), lambda w, *_: (0, 0)),
        ],
        out_specs=pl.BlockSpec((None, window, d), lambda w, *_: (w // wpb, w % wpb, 0)),
        scratch_shapes=[pltpu.VMEM((2, r, d), F32), pltpu.SemaphoreType.DMA((2,))],
    )
    return pl.pallas_call(
        _combine_kernel,
        grid_spec=grid_spec,
        out_shape=jax.ShapeDtypeStruct((b, l, d), F32),
        compiler_params=_cparams(("arbitrary",), VMEM_LIMIT_BYTES),
        name="expert_combine",
    )(plan["loff"], plan["goff"], plan["cpad"], plan["dloc_cols"], y, x, mod, ln_g, ln_b)


def _dispatch_plan(info, counts, window, rows):
    n = info.shape[0]
    nw = n // window
    as_int = lambda lane0: info[:, lane0:lane0 + TOP_K].astype(jnp.int32).reshape(nw, window, TOP_K)
    dloc = as_int(ROUTE_ROW_LANE)
    count = counts[:, 0, :N_EXPERTS].astype(jnp.int32)
    cpad = -(-count // SEG_ALIGN) * SEG_ALIGN
    loff = jnp.cumsum(cpad, axis=1) - cpad
    group = -(-jnp.sum(cpad, axis=0) // rows) * rows
    gend = jnp.cumsum(group)
    goff = (gend - group)[None, :] + jnp.cumsum(cpad, axis=0) - cpad
    n_tiles = -(-(TOP_K * n + nw * MAX_SEG_PAD) // rows) + N_EXPERTS
    tile_start = jnp.arange(n_tiles) * rows
    tile_expert = jnp.minimum(jnp.sum(tile_start[:, None] >= gend[None, :], axis=1), N_EXPERTS - 1)
    wts = info[:, ROUTE_W_LANE:ROUTE_W_LANE + TOP_K].reshape(nw, window, TOP_K)
    flat = lambda a: a.reshape(-1).astype(jnp.int32)
    return dict(
        loff=flat(loff), goff=flat(goff), cpad=flat(cpad),
        dloc_rows=jnp.transpose(dloc, (0, 2, 1)).astype(jnp.int32),
        dloc_cols=dloc.reshape(n, TOP_K).astype(jnp.int32),
        wts_rows=jnp.transpose(wts, (0, 2, 1)),
        tail=flat(goff[-1] + cpad[-1]), tail_len=flat(gend - goff[-1] - cpad[-1]),
        rest=flat(gend[-1:]),
        tile_expert=tile_expert.astype(jnp.int32),
        n_valid=(gend[-1] // rows).reshape(1).astype(jnp.int32),
        n_rows=n_tiles * rows,
    )


def _attention_layer(x, ctx, mod_x, mod_c, ln_g, ln_b, w_qkv, w_o, sink, w_gu, w_dn, grid_w, ts):
    b, l, d = x.shape
    lc = ctx.shape[1]
    g0, b0, g1, b1 = ln_g[0:1], ln_b[0:1], ln_g[1:2], ln_b[1:2]
    w_qkv, w_o, w_gu, w_dn = (w.astype(BF16) for w in (w_qkv, w_o, w_gu, w_dn))
    ctx_flat = ctx.reshape(1, b * lc, d)
    n_kv = N_KV_HEADS * HEAD_DIM

    q, k, v = _qkv_call(x, mod_x, w_qkv, _rope_tables(l, grid_w), ts["qkv_rows"])
    qc, kc, vc = _qkv_call(ctx_flat, mod_c, w_qkv, _no_rope_tables(b * lc), ts["qkv_rows"])
    qc = qc.reshape(b, lc, -1)
    kc, vc = kc.reshape(b, lc, n_kv), vc.reshape(b, lc, n_kv)
    mod_cb = jnp.broadcast_to(mod_c, (b,) + mod_c.shape[1:])

    x = _attn_call(q, k, v, kc, vc, sink, w_o, x, mod_x, g0, b0, band=True)
    ctx = _attn_call(qc, None, None, kc, vc, sink, w_o, ctx, mod_cb, g0, b0, band=False)

    x = _ffn_call(x, mod_x, w_gu, w_dn, g1, b1, ts["ffn_rows"], ts["ffn_cols"])
    ctx = _ffn_call(ctx.reshape(1, b * lc, d), mod_c, w_gu, w_dn, g1, b1,
                    ts["ffn_rows"], ts["ffn_cols"]).reshape(b, lc, d)
    return x, ctx


def _recurrent_layer(x, ctx, mod_x, mod_c, ln_g, ln_b, w_in, conv_w, conv_b, lam, w_r, b_r, w_i,
                     b_i, w_out, router, w_gu, w_dn, ts):
    b, l, d = x.shape
    g0, b0, g1, b1 = ln_g[0:1], ln_b[0:1], ln_g[1:2], ln_b[1:2]
    w_in, w_out = w_in.astype(BF16), w_out.astype(BF16)
    wd = _paired_block_diag(w_r, w_i)
    conv_b = conv_b.reshape(1, -1)
    mod_cb = jnp.broadcast_to(mod_c, (b,) + mod_c.shape[1:])

    r = w_out.shape[0]
    rg = r // LRU_GROUPS
    gate, *xb = _inproj_call(x, mod_x, w_in, ts["proj_rows"])
    _, *xb_c = _inproj_call(ctx, mod_cb, w_in, ts["proj_rows"])
    zero = jnp.zeros((b, rg), F32)
    hf, hb = [], []
    for g in range(LRU_GROUPS):
        cs = slice(g * rg, (g + 1) * rg)
        pairs = slice(g * rg // (2 * LRU_BLOCK_DIM), (g + 1) * rg // (2 * LRU_BLOCK_DIM))
        scan = functools.partial(_scan_call, conv_w=conv_w[:, cs], conv_b=conv_b[:, cs],
                                 wd=wd[:, pairs], b_r=0.5 * b_r[:, cs], b_i=0.5 * b_i[:, cs],
                                 lam=lam[:, cs], steps=ts["scan_steps"])
        _, _, cf, cb = scan(xb_c[g].reshape(-1, b, rg), zero, zero)
        hf_g, hb_g, _, _ = scan(xb[g].reshape(l, b, rg), cf, cb)
        hf.append(hf_g.reshape(l, b * rg))
        hb.append(hb_g.reshape(l, b * rg))
    router_padded = jnp.concatenate(
        _split_bf16(jnp.pad(router, ((0, 0), (0, LANES - N_EXPERTS)))), axis=1)
    x, h, info, counts = _outproj_call(gate, hf, hb, w_out, x, mod_x, g0, b0, router_padded,
                                       ts["moe_window"])
    plan = _dispatch_plan(info, counts, ts["moe_window"], ts["moe_rows"])
    xs = _dispatch_call(plan, h, ts["moe_window"], ts["moe_rows"], plan["n_rows"])
    y = _expert_call(plan, xs, w_gu, w_dn, ts["moe_rows"], ts["ffn_cols"])
    return _combine_call(plan, y, x, mod_x, g1, b1, ts["moe_window"])


def kernel(x, c, ctx, c_ctx, w_mod, b_mod, ln_g, ln_b, attn_w_qkv, attn_w_o, attn_sink, ffn_w_gu,
           ffn_w_dn, lru_w_in, lru_conv_w, lru_conv_b, lru_lambda, lru_w_r, lru_b_r, lru_w_i,
           lru_b_i, lru_w_out, moe_router, moe_w_gu, moe_w_dn):
    b, l, d = x.shape
    assert w_mod.shape[0] == DEPTH and l % BLOCK == 0 and ctx.shape[1] % BLOCK == 0
    grid_w = 64
    ts = _tile_sizes()

    rows = -(-(b + 1) // SUBLANES) * SUBLANES
    cond = jnp.zeros((rows, d), F32).at[:b].set(c).at[b].set(c_ctx)
    mod = _mod_call(cond, w_mod, b_mod[:, None, :])
    mod_x = lambda i: mod[i, :b, None, :]
    mod_c = lambda i: mod[i, b:b + 1, None, :]

    x, ctx = _attention_layer(x, ctx, mod_x(0), mod_c(0), ln_g[0], ln_b[0], attn_w_qkv[0],
                              attn_w_o[0], attn_sink[0], ffn_w_gu[0], ffn_w_dn[0], grid_w, ts)
    return _recurrent_layer(x, ctx, mod_x(1), mod_c(1), ln_g[1], ln_b[1], lru_w_in[0],
                            lru_conv_w[0], lru_conv_b[0], lru_lambda[0], lru_w_r[0], lru_b_r[0],
                            lru_w_i[0], lru_b_i[0], lru_w_out[0], moe_router[0], moe_w_gu[0],
                            moe_w_dn[0], ts)
```

```python
import functools

import jax
import jax.numpy as jnp
import numpy as np
from jax import lax
from jax.experimental import pallas as pl
from jax.experimental.pallas import tpu as pltpu

N_HEADS = 8
N_KV_HEADS = 2
GROUP = N_HEADS // N_KV_HEADS
HEAD_DIM = 128
BLOCK = 128
AXIS_DIM = HEAD_DIM // 2
ROPE_BASE = 10000.0
LRU_BLOCKS = 8
LRU_BLOCK_DIM = 128
LRU_C = 8.0
N_EXPERTS = 8
TOP_K = 2
DEPTH = 2
ALPHA = (2 * DEPTH) ** 0.25
LN_EPS = 1e-5
NEG_INF = -1e30

LANES = 128
SUBLANES = 8
VMEM_LIMIT_BYTES = 56 * 1024 * 1024

F32 = jnp.float32
BF16 = jnp.bfloat16


def _tile_sizes():
    return dict(
        qkv_rows=1024,
        ffn_rows=1024,
        ffn_cols=512,
        proj_rows=1024,
        scan_steps=128,
        moe_window=512,
        moe_rows=1024,
    )


def _cparams(sem, vmem=None):
    return pltpu.CompilerParams(dimension_semantics=sem, vmem_limit_bytes=vmem)


def _layer_norm(h, g, b):
    mu = jnp.mean(h, axis=-1, keepdims=True)
    d = h - mu
    var = jnp.mean(d * d, axis=-1, keepdims=True)
    return d * lax.rsqrt(var + LN_EPS) * g + b


def _mod_slice(mod_ref, k, d):
    return mod_ref[:, k * d:(k + 1) * d]


def _mod_kernel(c_ref, w_ref, b_ref, o_ref):
    c = c_ref[...]
    s = c * jax.nn.sigmoid(c)
    o_ref[...] = jnp.dot(s, w_ref[...], preferred_element_type=F32,
                         precision=lax.Precision.HIGHEST) + b_ref[...]


def _mod_call(cond, w_mod, b_mod):
    rows, d = cond.shape
    depth, _, n = w_mod.shape
    tn = 1536
    return pl.pallas_call(
        _mod_kernel,
        grid=(depth, n // tn),
        in_specs=[
            pl.BlockSpec((rows, d), lambda i, j: (0, 0)),
            pl.BlockSpec((None, d, tn), lambda i, j: (i, 0, j)),
            pl.BlockSpec((None, 1, tn), lambda i, j: (i, 0, j)),
        ],
        out_specs=pl.BlockSpec((None, rows, tn), lambda i, j: (i, 0, j)),
        out_shape=jax.ShapeDtypeStruct((depth, rows, n), F32),
        compiler_params=_cparams(("arbitrary", "arbitrary")),
        name="mod_vectors",
    )(cond, w_mod, b_mod)


def _qkv_kernel(x_ref, mod_ref, w_ref, cos_ref, sa_ref, sb_ref, q_ref, k_ref, v_ref):
    d = x_ref.shape[-1]
    h = x_ref[...] * (1.0 + _mod_slice(mod_ref, 1, d)) + _mod_slice(mod_ref, 0, d)
    qkv = jnp.dot(h.astype(BF16), w_ref[...], preferred_element_type=F32)
    cos, sa, sb = cos_ref[...], sa_ref[...], sb_ref[...]
    scale = HEAD_DIM ** -0.5
    for head in range(N_HEADS + N_KV_HEADS):
        u = qkv[:, head * HEAD_DIM:(head + 1) * HEAD_DIM]
        r = (u * cos + pltpu.roll(u, HEAD_DIM - AXIS_DIM // 2, axis=1) * sa
             + pltpu.roll(u, AXIS_DIM // 2, axis=1) * sb)
        if head < N_HEADS:
            q_ref[:, head * HEAD_DIM:(head + 1) * HEAD_DIM] = (r * scale).astype(BF16)
        else:
            kh = head - N_HEADS
            k_ref[:, kh * HEAD_DIM:(kh + 1) * HEAD_DIM] = r.astype(BF16)
    kv = N_KV_HEADS * HEAD_DIM
    v_ref[...] = qkv[:, N_HEADS * HEAD_DIM + kv:].astype(BF16)


def _qkv_call(x, mod, w_qkv, tabs, rows):
    b, l, d = x.shape
    n_q, n_kv = N_HEADS * HEAD_DIM, N_KV_HEADS * HEAD_DIM
    tm = min(rows, l)
    tab_spec = pl.BlockSpec((tm, HEAD_DIM), lambda i, j: (j, 0))
    return pl.pallas_call(
        _qkv_kernel,
        grid=(b, l // tm),
        in_specs=[
            pl.BlockSpec((None, tm, d), lambda i, j: (i, j, 0)),
            pl.BlockSpec((None, 1, 6 * d), lambda i, j: (i, 0, 0)),
            pl.BlockSpec(w_qkv.shape, lambda i, j: (0, 0)),
            tab_spec, tab_spec, tab_spec,
        ],
        out_specs=[
            pl.BlockSpec((None, tm, n_q), lambda i, j: (i, j, 0)),
            pl.BlockSpec((None, tm, n_kv), lambda i, j: (i, j, 0)),
            pl.BlockSpec((None, tm, n_kv), lambda i, j: (i, j, 0)),
        ],
        out_shape=[
            jax.ShapeDtypeStruct((b, l, n_q), BF16),
            jax.ShapeDtypeStruct((b, l, n_kv), BF16),
            jax.ShapeDtypeStruct((b, l, n_kv), BF16),
        ],
        compiler_params=_cparams(("arbitrary", "arbitrary"), VMEM_LIMIT_BYTES),
        name="qkv_rope",
    )(x, mod, w_qkv, *tabs)


def _rope_tables(l, grid_w):
    f32 = np.float32
    t = np.arange(l)
    pos = np.stack([(t // grid_w).astype(f32), (t % grid_w).astype(f32)], axis=1)
    freqs = f32(ROPE_BASE) ** (-np.arange(0, AXIS_DIM, 2, dtype=f32) / f32(AXIS_DIM))
    lane = np.arange(HEAD_DIM)
    ang = (pos[:, lane // AXIS_DIM] * freqs[lane % (AXIS_DIM // 2)][None, :]).astype(f32)
    first = (lane % AXIS_DIM) < AXIS_DIM // 2
    cos, sin = np.cos(ang).astype(f32), np.sin(ang).astype(f32)
    zero = f32(0.0)
    return cos, np.where(first, -sin, zero), np.where(first, zero, sin)


def _no_rope_tables(l):
    z = np.zeros((l, HEAD_DIM), np.float32)
    return np.ones((l, HEAD_DIM), np.float32), z, z


def _nt_dot(a, b):
    return lax.dot_general(a, b, (((1,), (1,)), ((), ())), preferred_element_type=F32)


def _attend(qs, segs, sink_col):
    k_all = jnp.concatenate([k for k, _, _ in segs], axis=0)
    v_all = jnp.concatenate([v for _, v, _ in segs], axis=0)
    v_ext = jnp.concatenate([v_all, jnp.ones_like(v_all)], axis=1)
    s_all = _nt_dot(qs, k_all)
    tiles, col = [], 0
    for k, _, mask in segs:
        for c in range(col, col + k.shape[0], LANES):
            s = s_all[:, c:c + LANES]
            tiles.append(s if mask is None else jnp.where(mask, s, NEG_INF))
        col += k.shape[0]
    m_tile = tiles[0]
    for s in tiles[1:]:
        m_tile = jnp.maximum(m_tile, s)
    m = jnp.maximum(sink_col, jnp.max(m_tile, axis=-1, keepdims=True))
    p_all = jnp.concatenate([jnp.exp(s - m).astype(BF16) for s in tiles], axis=1)
    res = jnp.dot(p_all, v_ext, preferred_element_type=F32)
    hd = v_all.shape[1]
    return res[:, :hd] / (res[:, hd:hd + 1] + jnp.exp(sink_col - m))


def _attn_epilogue(o_scr, wo_ref, x_ref, mod_ref, g_ref, b_ref, out_ref):
    d = x_ref.shape[-1]
    proj = jnp.dot(o_scr[...], wo_ref[...], preferred_element_type=F32)
    out_ref[...] = _layer_norm(ALPHA * x_ref[...] + _mod_slice(mod_ref, 2, d) * proj,
                               g_ref[...], b_ref[...])


def _sink_column(sink_ref, kvh):
    return jnp.concatenate(
        [jnp.full((BLOCK, 1), sink_ref[kvh * GROUP + g], F32) for g in range(GROUP)], axis=0)


def _stack_group(q_ref, kvh):
    return jnp.concatenate(
        [q_ref[:, (kvh * GROUP + g) * HEAD_DIM:(kvh * GROUP + g + 1) * HEAD_DIM]
         for g in range(GROUP)], axis=0)


def _unstack_group(o, o_scr, kvh):
    for g in range(GROUP):
        col = (kvh * GROUP + g) * HEAD_DIM
        o_scr[:, col:col + HEAD_DIM] = o[g * BLOCK:(g + 1) * BLOCK].astype(BF16)


def _band_attn_kernel(sink_ref, q_ref, kp_ref, kc_ref, kn_ref, vp_ref, vc_ref, vn_ref,
                      kx_ref, vx_ref, wo_ref, x_ref, mod_ref, g_ref, b_ref, out_ref, o_scr):
    n, nb = pl.program_id(1), pl.num_programs(1)
    row = lax.broadcasted_iota(jnp.int32, (GROUP * BLOCK, BLOCK), 0) % BLOCK
    col = lax.broadcasted_iota(jnp.int32, (GROUP * BLOCK, BLOCK), 1)
    mask_prev = (col >= row) & (n > 0)
    mask_next = (col <= row) & (n < nb - 1)
    for kvh in range(N_KV_HEADS):
        hs = slice(kvh * HEAD_DIM, (kvh + 1) * HEAD_DIM)
        segs = [(kp_ref[:, hs], vp_ref[:, hs], mask_prev),
                (kc_ref[:, hs], vc_ref[:, hs], None),
                (kn_ref[:, hs], vn_ref[:, hs], mask_next),
                (kx_ref[:, hs], vx_ref[:, hs], None)]
        o = _attend(_stack_group(q_ref, kvh), segs, _sink_column(sink_ref, kvh))
        _unstack_group(o, o_scr, kvh)
    _attn_epilogue(o_scr, wo_ref, x_ref, mod_ref, g_ref, b_ref, out_ref)


def _ctx_attn_kernel(sink_ref, q_ref, kx_ref, vx_ref, wo_ref, x_ref, mod_ref, g_ref, b_ref,
                     out_ref, o_scr):
    for kvh in range(N_KV_HEADS):
        hs = slice(kvh * HEAD_DIM, (kvh + 1) * HEAD_DIM)
        o = _attend(_stack_group(q_ref, kvh), [(kx_ref[:, hs], vx_ref[:, hs], None)],
                    _sink_column(sink_ref, kvh))
        _unstack_group(o, o_scr, kvh)
    _attn_epilogue(o_scr, wo_ref, x_ref, mod_ref, g_ref, b_ref, out_ref)


def _attn_call(q, k, v, kx, vx, sink, w_o, x, mod, ln_g, ln_b, band):
    b, l, d = x.shape
    lc = kx.shape[1]
    nb = l // BLOCK
    n_q, n_kv = N_HEADS * HEAD_DIM, N_KV_HEADS * HEAD_DIM
    kv_blk = lambda f: pl.BlockSpec((None, BLOCK, n_kv), f)
    common_in = [
        pl.BlockSpec((None, lc, n_kv), lambda i, j: (i, 0, 0)),
        pl.BlockSpec((None, lc, n_kv), lambda i, j: (i, 0, 0)),
        pl.BlockSpec(w_o.shape, lambda i, j: (0, 0)),
        pl.BlockSpec((None, BLOCK, d), lambda i, j: (i, j, 0)),
        pl.BlockSpec((None, 1, 6 * d), lambda i, j: (i, 0, 0)),
        pl.BlockSpec((1, d), lambda i, j: (0, 0)),
        pl.BlockSpec((1, d), lambda i, j: (0, 0)),
    ]
    head_in = [pl.BlockSpec(memory_space=pltpu.SMEM),
               pl.BlockSpec((None, BLOCK, n_q), lambda i, j: (i, j, 0))]
    if band:
        prev = lambda i, j: (i, jnp.maximum(j - 1, 0), 0)
        cur = lambda i, j: (i, j, 0)
        nxt = lambda i, j: (i, jnp.minimum(j + 1, nb - 1), 0)
        in_specs = head_in + [kv_blk(prev), kv_blk(cur), kv_blk(nxt)] * 2 + common_in
        args = (sink, q, k, k, k, v, v, v, kx, vx, w_o, x, mod, ln_g, ln_b)
        body, name = _band_attn_kernel, "band_attention"
    else:
        in_specs = head_in + common_in
        args = (sink, q, kx, vx, w_o, x, mod, ln_g, ln_b)
        body, name = _ctx_attn_kernel, "context_attention"
    return pl.pallas_call(
        body,
        grid=(b, nb),
        in_specs=in_specs,
        out_specs=pl.BlockSpec((None, BLOCK, d), lambda i, j: (i, j, 0)),
        out_shape=jax.ShapeDtypeStruct((b, l, d), F32),
        scratch_shapes=[pltpu.VMEM((BLOCK, n_q), BF16)],
        compiler_params=_cparams(("arbitrary", "arbitrary"), VMEM_LIMIT_BYTES),
        name=name,
    )(*args)


def _swiglu_chunk(hb, wg, wu, wd):
    g = jnp.dot(hb, wg, preferred_element_type=F32)
    u = jnp.dot(hb, wu, preferred_element_type=F32)
    a = (g * jax.nn.sigmoid(g)) * u
    return jnp.dot(a.astype(BF16), wd, preferred_element_type=F32)


def _ffn_kernel(xc_ref, xp_ref, modc_ref, modp_ref, wg_ref, wu_ref, wd_ref, g_ref, b_ref, out_ref,
                hb_scr, acc_scr):
    i, f = pl.program_id(0), pl.program_id(1)
    nt = pl.num_programs(0) - 1
    d = xc_ref.shape[-1]

    def finish_previous():
        out_ref[...] = _layer_norm(
            ALPHA * xp_ref[...] + _mod_slice(modp_ref, 5, d) * acc_scr[...], g_ref[...], b_ref[...])

    def start_current():
        h = xc_ref[...] * (1.0 + _mod_slice(modc_ref, 4, d)) + _mod_slice(modc_ref, 3, d)
        hb = h.astype(BF16)
        hb_scr[...] = hb
        acc_scr[...] = _swiglu_chunk(hb, wg_ref[...], wu_ref[...], wd_ref[...])

    @pl.when((f == 0) & (i == 0))
    def _():
        start_current()

    @pl.when((f == 0) & (i > 0) & (i < nt))
    def _():
        finish_previous()
        start_current()

    @pl.when((f == 0) & (i == nt))
    def _():
        finish_previous()

    @pl.when((f > 0) & (i < nt))
    def _():
        acc_scr[...] += _swiglu_chunk(hb_scr[...], wg_ref[...], wu_ref[...], wd_ref[...])


def _ffn_call(x, mod, w_gu, w_dn, ln_g, ln_b, rows, cols):
    b, l, d = x.shape
    ff = w_dn.shape[0]
    tm, fc = min(rows, l), cols
    nf = ff // fc
    lt = l // tm
    nt = b * lt
    cur = lambda i: jnp.minimum(i, nt - 1)
    prev = lambda i: jnp.maximum(i - 1, 0)
    chunk = lambda i, f: jnp.where(i < nt, f, nf - 1)
    x_spec = lambda t: pl.BlockSpec((None, tm, d), lambda i, f: (t(i) // lt, t(i) % lt, 0))
    mod_spec = lambda t: pl.BlockSpec((None, 1, 6 * d), lambda i, f: (t(i) // lt, 0, 0))
    return pl.pallas_call(
        _ffn_kernel,
        grid=(nt + 1, nf),
        in_specs=[
            x_spec(cur), x_spec(prev), mod_spec(cur), mod_spec(prev),
            pl.BlockSpec((d, fc), lambda i, f: (0, chunk(i, f))),
            pl.BlockSpec((d, fc), lambda i, f: (0, nf + chunk(i, f))),
            pl.BlockSpec((fc, d), lambda i, f: (chunk(i, f), 0)),
            pl.BlockSpec((1, d), lambda i, f: (0, 0)),
            pl.BlockSpec((1, d), lambda i, f: (0, 0)),
        ],
        out_specs=x_spec(prev),
        out_shape=jax.ShapeDtypeStruct((b, l, d), F32),
        scratch_shapes=[pltpu.VMEM((tm, d), BF16), pltpu.VMEM((tm, d), F32)],
        compiler_params=_cparams(("arbitrary", "arbitrary"), VMEM_LIMIT_BYTES),
        name="dense_swiglu",
    )(x, x, mod, mod, w_gu, w_gu, w_dn, ln_g, ln_b)


LRU_GROUPS = 2
SCAN_UNROLL = 8


def _inproj_kernel(x_ref, mod_ref, w_ref, gate_ref, *xb_refs):
    d = x_ref.shape[-1]
    r = gate_ref.shape[-1]
    rg = r // len(xb_refs)
    h = x_ref[...] * (1.0 + _mod_slice(mod_ref, 1, d)) + _mod_slice(mod_ref, 0, d)
    y = jnp.dot(h.astype(BF16), w_ref[...], preferred_element_type=F32)
    gate_ref[...] = y[:, :r]
    for g, xb_ref in enumerate(xb_refs):
        xb_ref[...] = y[:, r + g * rg:r + (g + 1) * rg]


def _inproj_call(x, mod, w_in, rows):
    b, l, d = x.shape
    r = w_in.shape[1] // 2
    rg = r // LRU_GROUPS
    tm = min(rows, l)
    return pl.pallas_call(
        _inproj_kernel,
        grid=(b, l // tm),
        in_specs=[
            pl.BlockSpec((None, tm, d), lambda i, j: (i, j, 0)),
            pl.BlockSpec((None, 1, 6 * d), lambda i, j: (i, 0, 0)),
            pl.BlockSpec(w_in.shape, lambda i, j: (0, 0)),
        ],
        out_specs=[pl.BlockSpec((None, tm, r), lambda i, j: (i, j, 0))]
        + [pl.BlockSpec((tm, rg), lambda i, j: (j, i))] * LRU_GROUPS,
        out_shape=[jax.ShapeDtypeStruct((b, l, r), F32)]
        + [jax.ShapeDtypeStruct((l, b * rg), F32)] * LRU_GROUPS,
        compiler_params=_cparams(("arbitrary", "arbitrary"), VMEM_LIMIT_BYTES),
        name="lru_in_proj",
    )(x, mod, w_in)


def _lru_coeffs(main_ref, prev_ref, next_ref, first, last, cw_ref, cb_ref, wd_ref, br_ref, bi_ref,
                lam_ref, a_scr, b_scr):
    tl, nb, r = main_ref.shape
    prev = jnp.where(first, 0.0, prev_ref[...])
    nxt = jnp.where(last, 0.0, next_ref[...])
    ext = jnp.concatenate([prev, main_ref[...], nxt], axis=0)
    u = cb_ref[...][None]
    for k in range(4):
        u = u + cw_ref[k:k + 1, :][None] * ext[k:k + tl]
    u = u.reshape(tl * nb, r)
    ub = u.astype(BF16)
    lam = lam_ref[...]
    softplus_neg = jnp.maximum(-lam, 0.0) + jnp.log1p(jnp.exp(-jnp.abs(lam)))
    half_rate = (-0.5 * LRU_C) * softplus_neg
    pw = 2 * LRU_BLOCK_DIM
    for p in range(r // pw):
        cs = slice(p * pw, (p + 1) * pw)
        z = jnp.dot(ub[:, cs], wd_ref[p], preferred_element_type=F32)
        tr = jnp.tanh(z[:, :pw] + br_ref[:, cs])
        ig = 0.5 + 0.5 * jnp.tanh(z[:, pw:] + bi_ref[:, cs])
        log_a = half_rate[:, cs] * tr + half_rate[:, cs]
        a = jnp.exp(log_a)
        gain = jnp.sqrt(-jnp.tanh(log_a) * (a * a + 1.0))
        a_scr[:, :, cs] = a.reshape(tl, nb, pw)
        b_scr[:, :, cs] = (gain * (ig * u[:, cs])).reshape(tl, nb, pw)


def _scan_kernel(fm_ref, fp_ref, fn_ref, bm_ref, bp_ref, bn_ref, h0f_ref, h0b_ref,
                 cw_ref, cb_ref, wdf_ref, wdb_ref, br_ref, bi_ref, lam_ref,
                 hf_ref, hb_ref, lastf_ref, lastb_ref,
                 af_scr, bf_scr, ab_scr, bb_scr, sf_scr, sb_scr):
    c, nc = pl.program_id(0), pl.num_programs(0)
    tl = fm_ref.shape[0]

    @pl.when(c == 0)
    def _():
        sf_scr[...] = h0f_ref[...]
        sb_scr[...] = h0b_ref[...]

    _lru_coeffs(fm_ref, fp_ref, fn_ref, c == 0, c == nc - 1, cw_ref, cb_ref, wdf_ref,
                br_ref.at[0:1], bi_ref.at[0:1], lam_ref.at[0:1], af_scr, bf_scr)
    _lru_coeffs(bm_ref, bp_ref, bn_ref, c == nc - 1, c == 0, cw_ref, cb_ref, wdb_ref,
                br_ref.at[1:2], bi_ref.at[1:2], lam_ref.at[1:2], ab_scr, bb_scr)

    def step(t, carry):
        hf, hb = carry
        hf = af_scr[t] * hf + bf_scr[t]
        hf_ref[t] = hf
        tb = tl - 1 - t
        hb = ab_scr[tb] * hb + bb_scr[tb]
        hb_ref[tb] = hb
        return hf, hb

    hf, hb = lax.fori_loop(0, tl, step, (sf_scr[...], sb_scr[...]), unroll=SCAN_UNROLL)
    sf_scr[...] = hf
    sb_scr[...] = hb

    @pl.when(c == nc - 1)
    def _():
        lastf_ref[...] = hf
        lastb_ref[...] = hb


def _scan_call(xb, h0f, h0b, conv_w, conv_b, wd, b_r, b_i, lam, steps):
    ls, nb, r = xb.shape
    tl = min(steps, ls)
    nc = ls // tl
    fwd, bwd = (lambda c: c), (lambda c: nc - 1 - c)
    main = lambda ch: pl.BlockSpec((tl, nb, r), lambda c: (ch(c), 0, 0))
    prev2 = lambda ch: pl.BlockSpec((2, nb, r), lambda c: (jnp.maximum(ch(c) * (tl // 2) - 1, 0), 0, 0))
    next1 = lambda ch: pl.BlockSpec((1, nb, r), lambda c: (jnp.minimum((ch(c) + 1) * tl, ls - 1), 0, 0))
    full = lambda a: pl.BlockSpec(a.shape, lambda c: (0,) * a.ndim)
    state = pl.BlockSpec((nb, r), lambda c: (0, 0))
    wd_spec = lambda dirn: pl.BlockSpec((None,) + wd.shape[1:], lambda c: (dirn, 0, 0, 0))
    return pl.pallas_call(
        _scan_kernel,
        grid=(nc,),
        in_specs=[main(fwd), prev2(fwd), next1(fwd), main(bwd), prev2(bwd), next1(bwd),
                  state, state, full(conv_w), full(conv_b), wd_spec(0), wd_spec(1),
                  full(b_r), full(b_i), full(lam)],
        out_specs=[main(fwd), main(bwd), state, state],
        out_shape=[jax.ShapeDtypeStruct((ls, nb, r), F32), jax.ShapeDtypeStruct((ls, nb, r), F32),
                   jax.ShapeDtypeStruct((nb, r), F32), jax.ShapeDtypeStruct((nb, r), F32)],
        scratch_shapes=[pltpu.VMEM((tl, nb, r), F32)] * 4 + [pltpu.VMEM((nb, r), F32)] * 2,
        compiler_params=_cparams(("arbitrary",), VMEM_LIMIT_BYTES),
        name="rglru_scan",
    )(xb, xb, xb, xb, xb, xb, h0f, h0b, conv_w, conv_b, wd, wd, b_r, b_i, lam)


def _paired_block_diag(w_r, w_i):
    def pair(w):
        ndir = w.shape[0]
        w = w.reshape(ndir, LRU_BLOCKS // 2, 2, LRU_BLOCK_DIM, LRU_BLOCK_DIM)
        z = jnp.zeros_like(w[:, :, 0])
        top = jnp.concatenate([w[:, :, 0], z], axis=-1)
        bot = jnp.concatenate([z, w[:, :, 1]], axis=-1)
        return jnp.concatenate([top, bot], axis=-2)
    return (0.5 * jnp.concatenate([pair(w_r), pair(w_i)], axis=-1)).astype(BF16)


ROUTE_IDX_LANE = N_EXPERTS
ROUTE_W_LANE = N_EXPERTS + TOP_K
ROUTE_ROW_LANE = N_EXPERTS + 2 * TOP_K


def _split_bf16(a):
    hi = a.astype(BF16)
    return hi, (a - hi.astype(F32)).astype(BF16)


def _route(h, wr_ref, info_ref, infot_ref, cnt_ref):
    h_hi, h_lo = _split_bf16(h)
    both = jnp.dot(h_hi, wr_ref[...], preferred_element_type=F32)
    logits = (both[:, :LANES] + both[:, LANES:]
              + jnp.dot(h_lo, wr_ref[:, :LANES], preferred_element_type=F32))
    lane = lax.broadcasted_iota(jnp.int32, logits.shape, 1)
    logits = jnp.where(lane < N_EXPERTS, logits, -jnp.inf)
    m1 = jnp.max(logits, axis=-1, keepdims=True)
    i1 = jnp.min(jnp.where(logits == m1, lane, LANES), axis=-1, keepdims=True)
    rest = jnp.where(lane == i1, -jnp.inf, logits)
    m2 = jnp.max(rest, axis=-1, keepdims=True)
    i2 = jnp.min(jnp.where(rest == m2, lane, LANES), axis=-1, keepdims=True)
    e2 = jnp.exp(m2 - m1)
    w1 = 1.0 / (1.0 + e2)
    w2 = e2 / (1.0 + e2)
    chosen = jnp.where((lane == i1) | (lane == i2), 1.0, 0.0)
    tm = h.shape[0]
    earlier = (lax.broadcasted_iota(jnp.int32, (tm, tm), 0)
               > lax.broadcasted_iota(jnp.int32, (tm, tm), 1))
    rank = jnp.dot(jnp.where(earlier, 1.0, 0.0).astype(BF16), chosen.astype(BF16),
                   preferred_element_type=F32)
    counts = jnp.sum(chosen, axis=0, keepdims=True)
    padded = jnp.ceil(counts * (1.0 / SEG_ALIGN)) * SEG_ALIGN
    before = (lax.broadcasted_iota(jnp.int32, (LANES, LANES), 0)
              < lax.broadcasted_iota(jnp.int32, (LANES, LANES), 1))
    start = jnp.dot(jnp.broadcast_to(padded, (SUBLANES, LANES)).astype(BF16),
                    jnp.where(before, 1.0, 0.0).astype(BF16), preferred_element_type=F32)[0:1]
    row = rank + start
    d1 = jnp.sum(jnp.where(lane == i1, row, 0.0), axis=-1, keepdims=True)
    d2 = jnp.sum(jnp.where(lane == i2, row, 0.0), axis=-1, keepdims=True)
    info = jnp.where(lane == i1, w1, jnp.where(lane == i2, w2, 0.0))
    for k, val in enumerate((i1.astype(F32), i2.astype(F32), w1, w2, d1, d2)):
        info = jnp.where(lane == ROUTE_IDX_LANE + k, val, info)
    info_ref[...] = info
    infot_ref[...] = info.T
    cnt_ref[...] = counts


def _outproj_kernel(gate_ref, *refs):
    state_refs, refs = refs[:2 * LRU_GROUPS], refs[2 * LRU_GROUPS:]
    (w_ref, x_ref, mod_ref, g_ref, b_ref, wr_ref,
     out_ref, h_ref, info_ref, infot_ref, cnt_ref) = refs
    d = x_ref.shape[-1]
    gt = gate_ref[...]
    gelu = 0.5 * gt * (1.0 + jnp.tanh(0.7978845608028654 * (gt + 0.044715 * (gt * gt * gt))))
    states = jnp.concatenate([state_refs[g][...] + state_refs[LRU_GROUPS + g][...]
                              for g in range(LRU_GROUPS)], axis=1)
    y = gelu * states
    proj = jnp.dot(y.astype(BF16), w_ref[...], preferred_element_type=F32)
    xn = _layer_norm(ALPHA * x_ref[...] + _mod_slice(mod_ref, 2, d) * proj, g_ref[...], b_ref[...])
    out_ref[...] = xn
    h = xn * (1.0 + _mod_slice(mod_ref, 4, d)) + _mod_slice(mod_ref, 3, d)
    h_ref[...] = h.astype(BF16)
    _route(h, wr_ref, info_ref, infot_ref, cnt_ref)


def _outproj_call(gate, hf, hb, w_out, x, mod, ln_g, ln_b, router_padded, rows):
    b, l, d = x.shape
    r = gate.shape[-1]
    tm = rows
    nt = l // tm
    tmajor = pl.BlockSpec((tm, r // LRU_GROUPS), lambda i, j: (j, i))
    return pl.pallas_call(
        _outproj_kernel,
        grid=(b, nt),
        in_specs=[
            pl.BlockSpec((None, tm, r), lambda i, j: (i, j, 0)),
            *([tmajor] * (2 * LRU_GROUPS)),
            pl.BlockSpec(w_out.shape, lambda i, j: (0, 0)),
            pl.BlockSpec((None, tm, d), lambda i, j: (i, j, 0)),
            pl.BlockSpec((None, 1, 6 * d), lambda i, j: (i, 0, 0)),
            pl.BlockSpec((1, d), lambda i, j: (0, 0)),
            pl.BlockSpec((1, d), lambda i, j: (0, 0)),
            pl.BlockSpec(router_padded.shape, lambda i, j: (0, 0)),
        ],
        out_specs=[
            pl.BlockSpec((None, tm, d), lambda i, j: (i, j, 0)),
            pl.BlockSpec((tm, d), lambda i, j: (i * nt + j, 0)),
            pl.BlockSpec((tm, LANES), lambda i, j: (i * nt + j, 0)),
            pl.BlockSpec((None, LANES, tm), lambda i, j: (i * nt + j, 0, 0)),
            pl.BlockSpec((None, 1, LANES), lambda i, j: (i * nt + j, 0, 0)),
        ],
        out_shape=[jax.ShapeDtypeStruct((b, l, d), F32), jax.ShapeDtypeStruct((b * l, d), BF16),
                   jax.ShapeDtypeStruct((b * l, LANES), F32),
                   jax.ShapeDtypeStruct((b * nt, LANES, tm), F32),
                   jax.ShapeDtypeStruct((b * nt, 1, LANES), F32)],
        compiler_params=_cparams(("arbitrary", "arbitrary"), VMEM_LIMIT_BYTES),
        name="lru_out_proj_route",
    )(gate, *hf, *hb, w_out, x, mod, ln_g, ln_b, router_padded)


SEG_ALIGN = SUBLANES
MAX_SEG_PAD = N_EXPERTS * (SEG_ALIGN - 1)
BF16_ROWS = 2 * SUBLANES


def _window_rows(window):
    return -(-(TOP_K * window + MAX_SEG_PAD) // BF16_ROWS) * BF16_ROWS


def _for_each_piece(seg_refs, w, window, fn):
    loff_ref, goff_ref, cpad_ref = seg_refs
    for e in range(N_EXPERTS):
        lo, go, c = (ref[w * N_EXPERTS + e] for ref in (loff_ref, goff_ref, cpad_ref))
        size = window
        while size >= SEG_ALIGN:
            done = c & ~(2 * size - 1)

            @pl.when((c & size) != 0)
            def _(lo=lo, go=go, done=done, size=size):
                fn(pl.multiple_of(lo + done, SEG_ALIGN), pl.multiple_of(go + done, SEG_ALIGN), size)

            size //= 2


def _for_each_tail_piece(tail_ref, len_ref, max_size, fn):
    for e in range(N_EXPERTS):
        go, c = tail_ref[e], len_ref[e]
        size = max_size
        while size >= SEG_ALIGN:
            done = c & ~(2 * size - 1)

            @pl.when((c & size) != 0)
            def _(go=go, done=done, size=size):
                fn(pl.multiple_of(go + done, SEG_ALIGN), size)

            size //= 2


def _dispatch_kernel(loff_ref, goff_ref, cpad_ref, tail_ref, tlen_ref, rest_ref, infot_ref,
                     h_ref, xs_hbm, buf, zbuf, sems):
    w, nw = pl.program_id(0), pl.num_programs(0)
    slot = w % 2
    window, d = h_ref.shape
    r = buf.shape[1]
    zrows = zbuf.shape[0]
    segs = (loff_ref, goff_ref, cpad_ref)

    def copy(buf_slot, lrow, grow, size):
        return pltpu.make_async_copy(buf.at[buf_slot, pl.ds(lrow, size)],
                                     xs_hbm.at[pl.ds(grow, size)], sems.at[buf_slot])

    def fill(grow, size):
        return pltpu.make_async_copy(zbuf.at[pl.ds(0, size)], xs_hbm.at[pl.ds(grow, size)],
                                     sems.at[2])

    def for_each_fill(act):
        _for_each_tail_piece(tail_ref, tlen_ref, zrows, lambda g, s: act(fill(g, s)))

        def body(k, _):
            act(fill(pl.multiple_of(rest_ref[0] + k * zrows, SEG_ALIGN), zrows))
            return 0
        lax.fori_loop(0, (xs_hbm.shape[0] - rest_ref[0]) // zrows, body, 0)

    @pl.when(w == 0)
    def _():
        zbuf[...] = jnp.zeros_like(zbuf)
        for_each_fill(lambda cp: cp.start())

    row = lax.broadcasted_iota(jnp.int32, (r, window), 0)
    dest = infot_ref[ROUTE_ROW_LANE:ROUTE_ROW_LANE + TOP_K, :].astype(jnp.int32)
    wts = infot_ref[ROUTE_W_LANE:ROUTE_W_LANE + TOP_K, :]
    hit0 = row == dest[0:1, :]
    hit1 = row == dest[1:2, :]
    perm = jnp.where(hit0 | hit1, 1.0, 0.0).astype(BF16)
    buf[slot, :, :d] = jnp.dot(perm, h_ref[...], preferred_element_type=F32)
    roww = jnp.sum(jnp.where(hit0, wts[0:1, :], 0.0) + jnp.where(hit1, wts[1:2, :], 0.0),
                   axis=1, keepdims=True)
    buf[slot, :, d:] = jnp.broadcast_to(roww, (r, LANES))

    _for_each_piece(segs, w, window, lambda l, g, s: copy(slot, l, g, s).start())

    @pl.when(w > 0)
    def _():
        _for_each_piece(segs, w - 1, window, lambda l, g, s: copy(1 - slot, l, g, s).wait())

    @pl.when(w == nw - 1)
    def _():
        _for_each_piece(segs, w, window, lambda l, g, s: copy(slot, l, g, s).wait())
        for_each_fill(lambda cp: cp.wait())


def _dispatch_call(plan, h, window, rows, n_rows):
    n, d = h.shape
    nw = n // window
    r = _window_rows(window)
    width = d + LANES
    grid_spec = pltpu.PrefetchScalarGridSpec(
        num_scalar_prefetch=6,
        grid=(nw,),
        in_specs=[
            pl.BlockSpec((None, LANES, window), lambda w, *_: (w, 0, 0)),
            pl.BlockSpec((window, d), lambda w, *_: (w, 0)),
        ],
        out_specs=pl.BlockSpec(memory_space=pl.ANY),
        scratch_shapes=[pltpu.VMEM((2, r, width), F32), pltpu.VMEM((rows // 2, width), F32),
                        pltpu.SemaphoreType.DMA((3,))],
    )
    return pl.pallas_call(
        _dispatch_kernel,
        grid_spec=grid_spec,
        out_shape=jax.ShapeDtypeStruct((n_rows, width), F32),
        compiler_params=_cparams(("arbitrary",), VMEM_LIMIT_BYTES),
        name="expert_dispatch",
    )(plan["loff"], plan["goff"], plan["cpad"], plan["tail"], plan["tail_len"], plan["rest"],
      plan["info_t"], h)


def _expert_kernel(te_ref, nv_ref, xs_ref, wg_ref, wu_ref, wd_ref, y_ref, hb_scr, acc_scr):
    i, f, nf = pl.program_id(0), pl.program_id(1), pl.num_programs(1)
    d = y_ref.shape[-1]
    valid = i < nv_ref[0]

    @pl.when(valid & (f == 0))
    def _():
        hb_scr[...] = xs_ref[:, :d].astype(BF16)
        acc_scr[...] = jnp.zeros_like(acc_scr)

    @pl.when(valid)
    def _():
        acc_scr[...] += _swiglu_chunk(hb_scr[...], wg_ref[...].astype(BF16),
                                      wu_ref[...].astype(BF16), wd_ref[...].astype(BF16))

    @pl.when(valid & (f == nf - 1))
    def _():
        y_ref[...] = acc_scr[...] * xs_ref[:, d:d + 1]

    @pl.when(jnp.logical_not(valid) & (f == nf - 1))
    def _():
        y_ref[...] = jnp.zeros_like(y_ref)


def _expert_call(plan, xs, w_gu, w_dn, rows, cols):
    n_rows, width = xs.shape
    d = width - LANES
    nt = n_rows // rows
    ff = w_dn.shape[1]
    fc = cols
    nf = ff // fc
    tile = lambda i, nv: jnp.minimum(i, nv[0] - 1)
    chunk = lambda i, f, nv: jnp.where(i < nv[0], f, nf - 1)
    grid_spec = pltpu.PrefetchScalarGridSpec(
        num_scalar_prefetch=2,
        grid=(nt, nf),
        in_specs=[
            pl.BlockSpec((rows, width), lambda i, f, te, nv: (tile(i, nv), 0)),
            pl.BlockSpec((None, d, fc), lambda i, f, te, nv: (te[tile(i, nv)], 0, chunk(i, f, nv))),
            pl.BlockSpec((None, d, fc),
                         lambda i, f, te, nv: (te[tile(i, nv)], 0, nf + chunk(i, f, nv))),
            pl.BlockSpec((None, fc, d), lambda i, f, te, nv: (te[tile(i, nv)], chunk(i, f, nv), 0)),
        ],
        out_specs=pl.BlockSpec((rows, d), lambda i, f, te, nv: (i, 0)),
        scratch_shapes=[pltpu.VMEM((rows, d), BF16), pltpu.VMEM((rows, d), F32)],
    )
    return pl.pallas_call(
        _expert_kernel,
        grid_spec=grid_spec,
        out_shape=jax.ShapeDtypeStruct((n_rows, d), F32),
        compiler_params=_cparams(("arbitrary", "arbitrary"), VMEM_LIMIT_BYTES),
        name="expert_swiglu",
    )(plan["tile_expert"], plan["n_valid"], xs, w_gu, w_gu, w_dn)


def _combine_kernel(loff_ref, goff_ref, cpad_ref, info_ref, y_hbm, x_ref, mod_ref, g_ref, b_ref,
                    out_ref, ybuf, sems):
    w, nw = pl.program_id(0), pl.num_programs(0)
    slot = w % 2
    window, d = x_ref.shape
    r = ybuf.shape[1]
    segs = (loff_ref, goff_ref, cpad_ref)

    def copy(buf_slot, lrow, grow, size):
        return pltpu.make_async_copy(y_hbm.at[pl.ds(grow, size)],
                                     ybuf.at[buf_slot, pl.ds(lrow, size)], sems.at[buf_slot])

    @pl.when(w == 0)
    def _():
        ybuf[...] = jnp.zeros_like(ybuf)
        _for_each_piece(segs, 0, window, lambda l, g, s: copy(0, l, g, s).start())

    @pl.when(w + 1 < nw)
    def _():
        _for_each_piece(segs, w + 1, window, lambda l, g, s: copy(1 - slot, l, g, s).start())

    _for_each_piece(segs, w, window, lambda l, g, s: copy(slot, l, g, s).wait())

    lane = lax.broadcasted_iota(jnp.int32, (window, r), 1)
    src = info_ref[:, ROUTE_ROW_LANE:ROUTE_ROW_LANE + TOP_K].astype(jnp.int32)
    hit = (lane == src[:, 0:1]) | (lane == src[:, 1:2])
    perm = jnp.where(hit, 1.0, 0.0).astype(BF16)
    hi, lo = _split_bf16(ybuf[slot])
    f = (jnp.dot(perm, hi, preferred_element_type=F32)
         + jnp.dot(perm, lo, preferred_element_type=F32))
    out_ref[...] = _layer_norm(ALPHA * x_ref[...] + _mod_slice(mod_ref, 5, d) * f,
                               g_ref[...], b_ref[...])


def _combine_call(plan, y, x, mod, ln_g, ln_b, window):
    b, l, d = x.shape
    wpb = l // window
    r = _window_rows(window)
    grid_spec = pltpu.PrefetchScalarGridSpec(
        num_scalar_prefetch=3,
        grid=(b * wpb,),
        in_specs=[
            pl.BlockSpec((window, LANES), lambda w, *_: (w, 0)),
            pl.BlockSpec(memory_space=pl.ANY),
            pl.BlockSpec((None, window, d), lambda w, *_: (w // wpb, w % wpb, 0)),
            pl.BlockSpec((None, 1, 6 * d), lambda w, *_: (w // wpb, 0, 0)),
            pl.BlockSpec((1, d), lambda w, *_: (0, 0)),
            pl.BlockSpec((1, d), lambda w, *_: (0, 0)),
        ],
        out_specs=pl.BlockSpec((None, window, d), lambda w, *_: (w // wpb, w % wpb, 0)),
        scratch_shapes=[pltpu.VMEM((2, r, d), F32), pltpu.SemaphoreType.DMA((2,))],
    )
    return pl.pallas_call(
        _combine_kernel,
        grid_spec=grid_spec,
        out_shape=jax.ShapeDtypeStruct((b, l, d), F32),
        compiler_params=_cparams(("arbitrary",), VMEM_LIMIT_BYTES),
        name="expert_combine",
    )(plan["loff"], plan["goff"], plan["cpad"], plan["info"], y, x, mod, ln_g, ln_b)


def _dispatch_plan(info, info_t, counts, window, rows):
    n = info.shape[0]
    nw = n // window
    count = counts[:, 0, :N_EXPERTS].astype(jnp.int32)
    cpad = -(-count // SEG_ALIGN) * SEG_ALIGN
    loff = jnp.cumsum(cpad, axis=1) - cpad
    group = -(-jnp.sum(cpad, axis=0) // rows) * rows
    gend = jnp.cumsum(group)
    goff = (gend - group)[None, :] + jnp.cumsum(cpad, axis=0) - cpad
    n_tiles = -(-(TOP_K * n + nw * MAX_SEG_PAD) // rows) + N_EXPERTS
    tile_start = jnp.arange(n_tiles) * rows
    tile_expert = jnp.minimum(jnp.sum(tile_start[:, None] >= gend[None, :], axis=1), N_EXPERTS - 1)
    flat = lambda a: a.reshape(-1).astype(jnp.int32)
    return dict(
        loff=flat(loff), goff=flat(goff), cpad=flat(cpad),
        info=info, info_t=info_t,
        tail=flat(goff[-1] + cpad[-1]), tail_len=flat(gend - goff[-1] - cpad[-1]),
        rest=flat(gend[-1:]),
        tile_expert=tile_expert.astype(jnp.int32),
        n_valid=(gend[-1] // rows).reshape(1).astype(jnp.int32),
        n_rows=n_tiles * rows,
    )


def _attention_layer(x, ctx, mod_x, mod_c, ln_g, ln_b, w_qkv, w_o, sink, w_gu, w_dn, grid_w, ts):
    b, l, d = x.shape
    lc = ctx.shape[1]
    g0, b0, g1, b1 = ln_g[0:1], ln_b[0:1], ln_g[1:2], ln_b[1:2]
    w_qkv, w_o, w_gu, w_dn = (w.astype(BF16) for w in (w_qkv, w_o, w_gu, w_dn))
    ctx_flat = ctx.reshape(1, b * lc, d)
    n_kv = N_KV_HEADS * HEAD_DIM

    q, k, v = _qkv_call(x, mod_x, w_qkv, _rope_tables(l, grid_w), ts["qkv_rows"])
    qc, kc, vc = _qkv_call(ctx_flat, mod_c, w_qkv, _no_rope_tables(b * lc), ts["qkv_rows"])
    qc = qc.reshape(b, lc, -1)
    kc, vc = kc.reshape(b, lc, n_kv), vc.reshape(b, lc, n_kv)
    mod_cb = jnp.broadcast_to(mod_c, (b,) + mod_c.shape[1:])

    x = _attn_call(q, k, v, kc, vc, sink, w_o, x, mod_x, g0, b0, band=True)
    ctx = _attn_call(qc, None, None, kc, vc, sink, w_o, ctx, mod_cb, g0, b0, band=False)

    x = _ffn_call(x, mod_x, w_gu, w_dn, g1, b1, ts["ffn_rows"], ts["ffn_cols"])
    ctx = _ffn_call(ctx.reshape(1, b * lc, d), mod_c, w_gu, w_dn, g1, b1,
                    ts["ffn_rows"], ts["ffn_cols"]).reshape(b, lc, d)
    return x, ctx


def _recurrent_layer(x, ctx, mod_x, mod_c, ln_g, ln_b, w_in, conv_w, conv_b, lam, w_r, b_r, w_i,
                     b_i, w_out, router, w_gu, w_dn, ts):
    b, l, d = x.shape
    g0, b0, g1, b1 = ln_g[0:1], ln_b[0:1], ln_g[1:2], ln_b[1:2]
    w_in, w_out = w_in.astype(BF16), w_out.astype(BF16)
    wd = _paired_block_diag(w_r, w_i)
    conv_b = conv_b.reshape(1, -1)
    mod_cb = jnp.broadcast_to(mod_c, (b,) + mod_c.shape[1:])

    r = w_out.shape[0]
    rg = r // LRU_GROUPS
    gate, *xb = _inproj_call(x, mod_x, w_in, ts["proj_rows"])
    _, *xb_c = _inproj_call(ctx, mod_cb, w_in, ts["proj_rows"])
    zero = jnp.zeros((b, rg), F32)
    hf, hb = [], []
    for g in range(LRU_GROUPS):
        cs = slice(g * rg, (g + 1) * rg)
        pairs = slice(g * rg // (2 * LRU_BLOCK_DIM), (g + 1) * rg // (2 * LRU_BLOCK_DIM))
        scan = functools.partial(_scan_call, conv_w=conv_w[:, cs], conv_b=conv_b[:, cs],
                                 wd=wd[:, pairs], b_r=0.5 * b_r[:, cs], b_i=0.5 * b_i[:, cs],
                                 lam=lam[:, cs], steps=ts["scan_steps"])
        _, _, cf, cb = scan(xb_c[g].reshape(-1, b, rg), zero, zero)
        hf_g, hb_g, _, _ = scan(xb[g].reshape(l, b, rg), cf, cb)
        hf.append(hf_g.reshape(l, b * rg))
        hb.append(hb_g.reshape(l, b * rg))
    router_padded = jnp.concatenate(
        _split_bf16(jnp.pad(router, ((0, 0), (0, LANES - N_EXPERTS)))), axis=1)
    x, h, info, info_t, counts = _outproj_call(gate, hf, hb, w_out, x, mod_x, g0, b0,
                                               router_padded, ts["moe_window"])
    plan = _dispatch_plan(info, info_t, counts, ts["moe_window"], ts["moe_rows"])
    xs = _dispatch_call(plan, h, ts["moe_window"], ts["moe_rows"], plan["n_rows"])
    y = _expert_call(plan, xs, w_gu, w_dn, ts["moe_rows"], ts["ffn_cols"])
    return _combine_call(plan, y, x, mod_x, g1, b1, ts["moe_window"])


def kernel(x, c, ctx, c_ctx, w_mod, b_mod, ln_g, ln_b, attn_w_qkv, attn_w_o, attn_sink, ffn_w_gu,
           ffn_w_dn, lru_w_in, lru_conv_w, lru_conv_b, lru_lambda, lru_w_r, lru_b_r, lru_w_i,
           lru_b_i, lru_w_out, moe_router, moe_w_gu, moe_w_dn):
    b, l, d = x.shape
    assert w_mod.shape[0] == DEPTH and l % BLOCK == 0 and ctx.shape[1] % BLOCK == 0
    grid_w = 64
    ts = _tile_sizes()

    rows = -(-(b + 1) // SUBLANES) * SUBLANES
    cond = jnp.zeros((rows, d), F32).at[:b].set(c).at[b].set(c_ctx)
    mod = _mod_call(cond, w_mod, b_mod[:, None, :])
    mod_x = lambda i: mod[i, :b, None, :]
    mod_c = lambda i: mod[i, b:b + 1, None, :]

    x, ctx = _attention_layer(x, ctx, mod_x(0), mod_c(0), ln_g[0], ln_b[0], attn_w_qkv[0],
                              attn_w_o[0], attn_sink[0], ffn_w_gu[0], ffn_w_dn[0], grid_w, ts)
    return _recurrent_layer(x, ctx, mod_x(1), mod_c(1), ln_g[1], ln_b[1], lru_w_in[0],
                            lru_conv_w[0], lru_conv_b[0], lru_lambda[0], lru_w_r[0], lru_b_r[0],
                            lru_w_i[0], lru_b_i[0], lru_w_out[0], moe_router[0], moe_w_gu[0],
                            moe_w_dn[0], ts)
```

```python
import functools

import jax
import jax.numpy as jnp
import numpy as np
from jax import lax
from jax.experimental import pallas as pl
from jax.experimental.pallas import tpu as pltpu

N_HEADS = 8
N_KV_HEADS = 2
GROUP = N_HEADS // N_KV_HEADS
HEAD_DIM = 128
BLOCK = 128
AXIS_DIM = HEAD_DIM // 2
ROPE_BASE = 10000.0
LRU_BLOCKS = 8
LRU_BLOCK_DIM = 128
LRU_C = 8.0
N_EXPERTS = 8
TOP_K = 2
DEPTH = 2
ALPHA = (2 * DEPTH) ** 0.25
LN_EPS = 1e-5
NEG_INF = -1e30

LANES = 128
SUBLANES = 8
VMEM_LIMIT_BYTES = 56 * 1024 * 1024

F32 = jnp.float32
BF16 = jnp.bfloat16


def _tile_sizes():
    return dict(
        qkv_rows=1024,
        ffn_rows=1024,
        ffn_cols=512,
        proj_rows=1024,
        scan_steps=128,
        moe_window=512,
        moe_rows=1024,
    )


def _cparams(sem, vmem=None):
    return pltpu.CompilerParams(dimension_semantics=sem, vmem_limit_bytes=vmem)


def _layer_norm(h, g, b):
    mu = jnp.mean(h, axis=-1, keepdims=True)
    d = h - mu
    var = jnp.mean(d * d, axis=-1, keepdims=True)
    return d * lax.rsqrt(var + LN_EPS) * g + b


def _mod_slice(mod_ref, k, d):
    return mod_ref[:, k * d:(k + 1) * d]


def _mod_kernel(c_ref, w_ref, b_ref, o_ref):
    c = c_ref[...]
    s = c * jax.nn.sigmoid(c)
    o_ref[...] = jnp.dot(s, w_ref[...], preferred_element_type=F32,
                         precision=lax.Precision.HIGHEST) + b_ref[...]


def _mod_call(cond, w_mod, b_mod):
    rows, d = cond.shape
    depth, _, n = w_mod.shape
    tn = 1536
    return pl.pallas_call(
        _mod_kernel,
        grid=(depth, n // tn),
        in_specs=[
            pl.BlockSpec((rows, d), lambda i, j: (0, 0)),
            pl.BlockSpec((None, d, tn), lambda i, j: (i, 0, j)),
            pl.BlockSpec((None, 1, tn), lambda i, j: (i, 0, j)),
        ],
        out_specs=pl.BlockSpec((None, rows, tn), lambda i, j: (i, 0, j)),
        out_shape=jax.ShapeDtypeStruct((depth, rows, n), F32),
        compiler_params=_cparams(("arbitrary", "arbitrary")),
        name="mod_vectors",
    )(cond, w_mod, b_mod)


def _qkv_kernel(x_ref, mod_ref, w_ref, cos_ref, sa_ref, sb_ref, q_ref, k_ref, v_ref):
    d = x_ref.shape[-1]
    h = x_ref[...] * (1.0 + _mod_slice(mod_ref, 1, d)) + _mod_slice(mod_ref, 0, d)
    qkv = jnp.dot(h.astype(BF16), w_ref[...], preferred_element_type=F32)
    cos, sa, sb = cos_ref[...], sa_ref[...], sb_ref[...]
    scale = HEAD_DIM ** -0.5
    for head in range(N_HEADS + N_KV_HEADS):
        u = qkv[:, head * HEAD_DIM:(head + 1) * HEAD_DIM]
        r = (u * cos + pltpu.roll(u, HEAD_DIM - AXIS_DIM // 2, axis=1) * sa
             + pltpu.roll(u, AXIS_DIM // 2, axis=1) * sb)
        if head < N_HEADS:
            q_ref[:, head * HEAD_DIM:(head + 1) * HEAD_DIM] = (r * scale).astype(BF16)
        else:
            kh = head - N_HEADS
            k_ref[:, kh * HEAD_DIM:(kh + 1) * HEAD_DIM] = r.astype(BF16)
    kv = N_KV_HEADS * HEAD_DIM
    v_ref[...] = qkv[:, N_HEADS * HEAD_DIM + kv:].astype(BF16)


def _qkv_call(x, mod, w_qkv, tabs, rows):
    b, l, d = x.shape
    n_q, n_kv = N_HEADS * HEAD_DIM, N_KV_HEADS * HEAD_DIM
    tm = min(rows, l)
    tab_spec = pl.BlockSpec((tm, HEAD_DIM), lambda i, j: (j, 0))
    return pl.pallas_call(
        _qkv_kernel,
        grid=(b, l // tm),
        in_specs=[
            pl.BlockSpec((None, tm, d), lambda i, j: (i, j, 0)),
            pl.BlockSpec((None, 1, 6 * d), lambda i, j: (i, 0, 0)),
            pl.BlockSpec(w_qkv.shape, lambda i, j: (0, 0)),
            tab_spec, tab_spec, tab_spec,
        ],
        out_specs=[
            pl.BlockSpec((None, tm, n_q), lambda i, j: (i, j, 0)),
            pl.BlockSpec((None, tm, n_kv), lambda i, j: (i, j, 0)),
            pl.BlockSpec((None, tm, n_kv), lambda i, j: (i, j, 0)),
        ],
        out_shape=[
            jax.ShapeDtypeStruct((b, l, n_q), BF16),
            jax.ShapeDtypeStruct((b, l, n_kv), BF16),
            jax.ShapeDtypeStruct((b, l, n_kv), BF16),
        ],
        compiler_params=_cparams(("arbitrary", "arbitrary"), VMEM_LIMIT_BYTES),
        name="qkv_rope",
    )(x, mod, w_qkv, *tabs)


def _rope_tables(l, grid_w):
    f32 = np.float32
    t = np.arange(l)
    pos = np.stack([(t // grid_w).astype(f32), (t % grid_w).astype(f32)], axis=1)
    freqs = f32(ROPE_BASE) ** (-np.arange(0, AXIS_DIM, 2, dtype=f32) / f32(AXIS_DIM))
    lane = np.arange(HEAD_DIM)
    ang = (pos[:, lane // AXIS_DIM] * freqs[lane % (AXIS_DIM // 2)][None, :]).astype(f32)
    first = (lane % AXIS_DIM) < AXIS_DIM // 2
    cos, sin = np.cos(ang).astype(f32), np.sin(ang).astype(f32)
    zero = f32(0.0)
    return cos, np.where(first, -sin, zero), np.where(first, zero, sin)


def _no_rope_tables(l):
    z = np.zeros((l, HEAD_DIM), np.float32)
    return np.ones((l, HEAD_DIM), np.float32), z, z


def _nt_dot(a, b):
    return lax.dot_general(a, b, (((1,), (1,)), ((), ())), preferred_element_type=F32)


def _attend(qs, segs, sink_col):
    k_all = jnp.concatenate([k for k, _, _ in segs], axis=0)
    v_all = jnp.concatenate([v for _, v, _ in segs], axis=0)
    v_ext = jnp.concatenate([v_all, jnp.ones_like(v_all)], axis=1)
    s_all = _nt_dot(qs, k_all)
    tiles, col = [], 0
    for k, _, mask in segs:
        for c in range(col, col + k.shape[0], LANES):
            s = s_all[:, c:c + LANES]
            tiles.append(s if mask is None else jnp.where(mask, s, NEG_INF))
        col += k.shape[0]
    m_tile = tiles[0]
    for s in tiles[1:]:
        m_tile = jnp.maximum(m_tile, s)
    m = jnp.maximum(sink_col, jnp.max(m_tile, axis=-1, keepdims=True))
    p_all = jnp.concatenate([jnp.exp(s - m).astype(BF16) for s in tiles], axis=1)
    res = jnp.dot(p_all, v_ext, preferred_element_type=F32)
    hd = v_all.shape[1]
    return res[:, :hd] / (res[:, hd:hd + 1] + jnp.exp(sink_col - m))


def _attn_epilogue(o_scr, wo_ref, x_ref, mod_ref, g_ref, b_ref, out_ref):
    d = x_ref.shape[-1]
    proj = jnp.dot(o_scr[...], wo_ref[...], preferred_element_type=F32)
    out_ref[...] = _layer_norm(ALPHA * x_ref[...] + _mod_slice(mod_ref, 2, d) * proj,
                               g_ref[...], b_ref[...])


def _sink_column(sink_ref, kvh):
    return jnp.concatenate(
        [jnp.full((BLOCK, 1), sink_ref[kvh * GROUP + g], F32) for g in range(GROUP)], axis=0)


def _stack_group(q_ref, kvh):
    return jnp.concatenate(
        [q_ref[:, (kvh * GROUP + g) * HEAD_DIM:(kvh * GROUP + g + 1) * HEAD_DIM]
         for g in range(GROUP)], axis=0)


def _unstack_group(o, o_scr, kvh):
    for g in range(GROUP):
        col = (kvh * GROUP + g) * HEAD_DIM
        o_scr[:, col:col + HEAD_DIM] = o[g * BLOCK:(g + 1) * BLOCK].astype(BF16)


def _band_attn_kernel(sink_ref, q_ref, kp_ref, kc_ref, kn_ref, vp_ref, vc_ref, vn_ref,
                      kx_ref, vx_ref, wo_ref, x_ref, mod_ref, g_ref, b_ref, out_ref, o_scr):
    n, nb = pl.program_id(1), pl.num_programs(1)
    row = lax.broadcasted_iota(jnp.int32, (GROUP * BLOCK, BLOCK), 0) % BLOCK
    col = lax.broadcasted_iota(jnp.int32, (GROUP * BLOCK, BLOCK), 1)
    mask_prev = (col >= row) & (n > 0)
    mask_next = (col <= row) & (n < nb - 1)
    for kvh in range(N_KV_HEADS):
        hs = slice(kvh * HEAD_DIM, (kvh + 1) * HEAD_DIM)
        segs = [(kp_ref[:, hs], vp_ref[:, hs], mask_prev),
                (kc_ref[:, hs], vc_ref[:, hs], None),
                (kn_ref[:, hs], vn_ref[:, hs], mask_next),
                (kx_ref[:, hs], vx_ref[:, hs], None)]
        o = _attend(_stack_group(q_ref, kvh), segs, _sink_column(sink_ref, kvh))
        _unstack_group(o, o_scr, kvh)
    _attn_epilogue(o_scr, wo_ref, x_ref, mod_ref, g_ref, b_ref, out_ref)


def _ctx_attn_kernel(sink_ref, q_ref, kx_ref, vx_ref, wo_ref, x_ref, mod_ref, g_ref, b_ref,
                     out_ref, o_scr):
    for kvh in range(N_KV_HEADS):
        hs = slice(kvh * HEAD_DIM, (kvh + 1) * HEAD_DIM)
        o = _attend(_stack_group(q_ref, kvh), [(kx_ref[:, hs], vx_ref[:, hs], None)],
                    _sink_column(sink_ref, kvh))
        _unstack_group(o, o_scr, kvh)
    _attn_epilogue(o_scr, wo_ref, x_ref, mod_ref, g_ref, b_ref, out_ref)


def _attn_call(q, k, v, kx, vx, sink, w_o, x, mod, ln_g, ln_b, band):
    b, l, d = x.shape
    lc = kx.shape[1]
    nb = l // BLOCK
    n_q, n_kv = N_HEADS * HEAD_DIM, N_KV_HEADS * HEAD_DIM
    kv_blk = lambda f: pl.BlockSpec((None, BLOCK, n_kv), f)
    common_in = [
        pl.BlockSpec((None, lc, n_kv), lambda i, j: (i, 0, 0)),
        pl.BlockSpec((None, lc, n_kv), lambda i, j: (i, 0, 0)),
        pl.BlockSpec(w_o.shape, lambda i, j: (0, 0)),
        pl.BlockSpec((None, BLOCK, d), lambda i, j: (i, j, 0)),
        pl.BlockSpec((None, 1, 6 * d), lambda i, j: (i, 0, 0)),
        pl.BlockSpec((1, d), lambda i, j: (0, 0)),
        pl.BlockSpec((1, d), lambda i, j: (0, 0)),
    ]
    head_in = [pl.BlockSpec(memory_space=pltpu.SMEM),
               pl.BlockSpec((None, BLOCK, n_q), lambda i, j: (i, j, 0))]
    if band:
        prev = lambda i, j: (i, jnp.maximum(j - 1, 0), 0)
        cur = lambda i, j: (i, j, 0)
        nxt = lambda i, j: (i, jnp.minimum(j + 1, nb - 1), 0)
        in_specs = head_in + [kv_blk(prev), kv_blk(cur), kv_blk(nxt)] * 2 + common_in
        args = (sink, q, k, k, k, v, v, v, kx, vx, w_o, x, mod, ln_g, ln_b)
        body, name = _band_attn_kernel, "band_attention"
    else:
        in_specs = head_in + common_in
        args = (sink, q, kx, vx, w_o, x, mod, ln_g, ln_b)
        body, name = _ctx_attn_kernel, "context_attention"
    return pl.pallas_call(
        body,
        grid=(b, nb),
        in_specs=in_specs,
        out_specs=pl.BlockSpec((None, BLOCK, d), lambda i, j: (i, j, 0)),
        out_shape=jax.ShapeDtypeStruct((b, l, d), F32),
        scratch_shapes=[pltpu.VMEM((BLOCK, n_q), BF16)],
        compiler_params=_cparams(("arbitrary", "arbitrary"), VMEM_LIMIT_BYTES),
        name=name,
    )(*args)


def _swiglu_chunk(hb, wg, wu, wd):
    g = jnp.dot(hb, wg, preferred_element_type=F32)
    u = jnp.dot(hb, wu, preferred_element_type=F32)
    a = (g * jax.nn.sigmoid(g)) * u
    return jnp.dot(a.astype(BF16), wd, preferred_element_type=F32)


def _ffn_kernel(xc_ref, xp_ref, modc_ref, modp_ref, wg_ref, wu_ref, wd_ref, g_ref, b_ref, out_ref,
                hb_scr, acc_scr):
    i, f = pl.program_id(0), pl.program_id(1)
    nt = pl.num_programs(0) - 1
    d = xc_ref.shape[-1]

    def finish_previous():
        out_ref[...] = _layer_norm(
            ALPHA * xp_ref[...] + _mod_slice(modp_ref, 5, d) * acc_scr[...], g_ref[...], b_ref[...])

    def start_current():
        h = xc_ref[...] * (1.0 + _mod_slice(modc_ref, 4, d)) + _mod_slice(modc_ref, 3, d)
        hb = h.astype(BF16)
        hb_scr[...] = hb
        acc_scr[...] = _swiglu_chunk(hb, wg_ref[...], wu_ref[...], wd_ref[...])

    @pl.when((f == 0) & (i == 0))
    def _():
        start_current()

    @pl.when((f == 0) & (i > 0) & (i < nt))
    def _():
        finish_previous()
        start_current()

    @pl.when((f == 0) & (i == nt))
    def _():
        finish_previous()

    @pl.when((f > 0) & (i < nt))
    def _():
        acc_scr[...] += _swiglu_chunk(hb_scr[...], wg_ref[...], wu_ref[...], wd_ref[...])


def _ffn_call(x, mod, w_gu, w_dn, ln_g, ln_b, rows, cols):
    b, l, d = x.shape
    ff = w_dn.shape[0]
    tm, fc = min(rows, l), cols
    nf = ff // fc
    lt = l // tm
    nt = b * lt
    cur = lambda i: jnp.minimum(i, nt - 1)
    prev = lambda i: jnp.maximum(i - 1, 0)
    chunk = lambda i, f: jnp.where(i < nt, f, nf - 1)
    x_spec = lambda t: pl.BlockSpec((None, tm, d), lambda i, f: (t(i) // lt, t(i) % lt, 0))
    mod_spec = lambda t: pl.BlockSpec((None, 1, 6 * d), lambda i, f: (t(i) // lt, 0, 0))
    return pl.pallas_call(
        _ffn_kernel,
        grid=(nt + 1, nf),
        in_specs=[
            x_spec(cur), x_spec(prev), mod_spec(cur), mod_spec(prev),
            pl.BlockSpec((d, fc), lambda i, f: (0, chunk(i, f))),
            pl.BlockSpec((d, fc), lambda i, f: (0, nf + chunk(i, f))),
            pl.BlockSpec((fc, d), lambda i, f: (chunk(i, f), 0)),
            pl.BlockSpec((1, d), lambda i, f: (0, 0)),
            pl.BlockSpec((1, d), lambda i, f: (0, 0)),
        ],
        out_specs=x_spec(prev),
        out_shape=jax.ShapeDtypeStruct((b, l, d), F32),
        scratch_shapes=[pltpu.VMEM((tm, d), BF16), pltpu.VMEM((tm, d), F32)],
        compiler_params=_cparams(("arbitrary", "arbitrary"), VMEM_LIMIT_BYTES),
        name="dense_swiglu",
    )(x, x, mod, mod, w_gu, w_gu, w_dn, ln_g, ln_b)


LRU_GROUPS = 2
SCAN_UNROLL = 8


def _inproj_kernel(x_ref, mod_ref, w_ref, gate_ref, *xb_refs):
    d = x_ref.shape[-1]
    r = gate_ref.shape[-1]
    rg = r // len(xb_refs)
    h = x_ref[...] * (1.0 + _mod_slice(mod_ref, 1, d)) + _mod_slice(mod_ref, 0, d)
    y = jnp.dot(h.astype(BF16), w_ref[...], preferred_element_type=F32)
    gate_ref[...] = y[:, :r]
    for g, xb_ref in enumerate(xb_refs):
        xb_ref[...] = y[:, r + g * rg:r + (g + 1) * rg]


def _inproj_call(x, mod, w_in, rows):
    b, l, d = x.shape
    r = w_in.shape[1] // 2
    rg = r // LRU_GROUPS
    tm = min(rows, l)
    return pl.pallas_call(
        _inproj_kernel,
        grid=(b, l // tm),
        in_specs=[
            pl.BlockSpec((None, tm, d), lambda i, j: (i, j, 0)),
            pl.BlockSpec((None, 1, 6 * d), lambda i, j: (i, 0, 0)),
            pl.BlockSpec(w_in.shape, lambda i, j: (0, 0)),
        ],
        out_specs=[pl.BlockSpec((None, tm, r), lambda i, j: (i, j, 0))]
        + [pl.BlockSpec((tm, rg), lambda i, j: (j, i))] * LRU_GROUPS,
        out_shape=[jax.ShapeDtypeStruct((b, l, r), F32)]
        + [jax.ShapeDtypeStruct((l, b * rg), F32)] * LRU_GROUPS,
        compiler_params=_cparams(("arbitrary", "arbitrary"), VMEM_LIMIT_BYTES),
        name="lru_in_proj",
    )(x, mod, w_in)


def _lru_coeffs(main_ref, prev_ref, next_ref, first, last, cw_ref, cb_ref, wd_ref, br_ref, bi_ref,
                lam_ref, a_scr, b_scr):
    tl, nb, r = main_ref.shape
    prev = jnp.where(first, 0.0, prev_ref[...])
    nxt = jnp.where(last, 0.0, next_ref[...])
    ext = jnp.concatenate([prev, main_ref[...], nxt], axis=0)
    u = cb_ref[...][None]
    for k in range(4):
        u = u + cw_ref[k:k + 1, :][None] * ext[k:k + tl]
    u = u.reshape(tl * nb, r)
    ub = u.astype(BF16)
    lam = lam_ref[...]
    softplus_neg = jnp.maximum(-lam, 0.0) + jnp.log1p(jnp.exp(-jnp.abs(lam)))
    half_rate = (-0.5 * LRU_C) * softplus_neg
    pw = 2 * LRU_BLOCK_DIM
    for p in range(r // pw):
        cs = slice(p * pw, (p + 1) * pw)
        z = jnp.dot(ub[:, cs], wd_ref[p], preferred_element_type=F32)
        tr = jnp.tanh(z[:, :pw] + br_ref[:, cs])
        ig = 0.5 + 0.5 * jnp.tanh(z[:, pw:] + bi_ref[:, cs])
        log_a = half_rate[:, cs] * tr + half_rate[:, cs]
        a = jnp.exp(log_a)
        gain = jnp.sqrt(-jnp.tanh(log_a) * (a * a + 1.0))
        a_scr[:, :, cs] = a.reshape(tl, nb, pw)
        b_scr[:, :, cs] = (gain * (ig * u[:, cs])).reshape(tl, nb, pw)


def _scan_kernel(fm_ref, fp_ref, fn_ref, bm_ref, bp_ref, bn_ref, h0f_ref, h0b_ref,
                 cw_ref, cb_ref, wdf_ref, wdb_ref, br_ref, bi_ref, lam_ref,
                 hf_ref, hb_ref, lastf_ref, lastb_ref,
                 af_scr, bf_scr, ab_scr, bb_scr, sf_scr, sb_scr):
    c, nc = pl.program_id(0), pl.num_programs(0)
    tl = fm_ref.shape[0]

    @pl.when(c == 0)
    def _():
        sf_scr[...] = h0f_ref[...]
        sb_scr[...] = h0b_ref[...]

    _lru_coeffs(fm_ref, fp_ref, fn_ref, c == 0, c == nc - 1, cw_ref, cb_ref, wdf_ref,
                br_ref.at[0:1], bi_ref.at[0:1], lam_ref.at[0:1], af_scr, bf_scr)
    _lru_coeffs(bm_ref, bp_ref, bn_ref, c == nc - 1, c == 0, cw_ref, cb_ref, wdb_ref,
                br_ref.at[1:2], bi_ref.at[1:2], lam_ref.at[1:2], ab_scr, bb_scr)

    def step(t, carry):
        hf, hb = carry
        hf = af_scr[t] * hf + bf_scr[t]
        hf_ref[t] = hf
        tb = tl - 1 - t
        hb = ab_scr[tb] * hb + bb_scr[tb]
        hb_ref[tb] = hb
        return hf, hb

    hf, hb = lax.fori_loop(0, tl, step, (sf_scr[...], sb_scr[...]), unroll=SCAN_UNROLL)
    sf_scr[...] = hf
    sb_scr[...] = hb

    @pl.when(c == nc - 1)
    def _():
        lastf_ref[...] = hf
        lastb_ref[...] = hb


def _scan_call(xb, h0f, h0b, conv_w, conv_b, wd, b_r, b_i, lam, steps):
    ls, nb, r = xb.shape
    tl = min(steps, ls)
    nc = ls // tl
    fwd, bwd = (lambda c: c), (lambda c: nc - 1 - c)
    main = lambda ch: pl.BlockSpec((tl, nb, r), lambda c: (ch(c), 0, 0))
    prev2 = lambda ch: pl.BlockSpec((2, nb, r), lambda c: (jnp.maximum(ch(c) * (tl // 2) - 1, 0), 0, 0))
    next1 = lambda ch: pl.BlockSpec((1, nb, r), lambda c: (jnp.minimum((ch(c) + 1) * tl, ls - 1), 0, 0))
    full = lambda a: pl.BlockSpec(a.shape, lambda c: (0,) * a.ndim)
    state = pl.BlockSpec((nb, r), lambda c: (0, 0))
    wd_spec = lambda dirn: pl.BlockSpec((None,) + wd.shape[1:], lambda c: (dirn, 0, 0, 0))
    return pl.pallas_call(
        _scan_kernel,
        grid=(nc,),
        in_specs=[main(fwd), prev2(fwd), next1(fwd), main(bwd), prev2(bwd), next1(bwd),
                  state, state, full(conv_w), full(conv_b), wd_spec(0), wd_spec(1),
                  full(b_r), full(b_i), full(lam)],
        out_specs=[main(fwd), main(bwd), state, state],
        out_shape=[jax.ShapeDtypeStruct((ls, nb, r), F32), jax.ShapeDtypeStruct((ls, nb, r), F32),
                   jax.ShapeDtypeStruct((nb, r), F32), jax.ShapeDtypeStruct((nb, r), F32)],
        scratch_shapes=[pltpu.VMEM((tl, nb, r), F32)] * 4 + [pltpu.VMEM((nb, r), F32)] * 2,
        compiler_params=_cparams(("arbitrary",), VMEM_LIMIT_BYTES),
        name="rglru_scan",
    )(xb, xb, xb, xb, xb, xb, h0f, h0b, conv_w, conv_b, wd, wd, b_r, b_i, lam)


def _paired_block_diag(w_r, w_i):
    def pair(w):
        ndir = w.shape[0]
        w = w.reshape(ndir, LRU_BLOCKS // 2, 2, LRU_BLOCK_DIM, LRU_BLOCK_DIM)
        z = jnp.zeros_like(w[:, :, 0])
        top = jnp.concatenate([w[:, :, 0], z], axis=-1)
        bot = jnp.concatenate([z, w[:, :, 1]], axis=-1)
        return jnp.concatenate([top, bot], axis=-2)
    return (0.5 * jnp.concatenate([pair(w_r), pair(w_i)], axis=-1)).astype(BF16)


ROUTE_IDX_LANE = N_EXPERTS
ROUTE_W_LANE = N_EXPERTS + TOP_K
ROUTE_ROW_LANE = N_EXPERTS + 2 * TOP_K


def _split_bf16(a):
    hi = a.astype(BF16)
    return hi, (a - hi.astype(F32)).astype(BF16)


def _route(h, wr_ref, info_ref, infot_ref, cnt_ref):
    h_hi, h_lo = _split_bf16(h)
    both = jnp.dot(h_hi, wr_ref[...], preferred_element_type=F32)
    logits = (both[:, :LANES] + both[:, LANES:]
              + jnp.dot(h_lo, wr_ref[:, :LANES], preferred_element_type=F32))
    lane = lax.broadcasted_iota(jnp.int32, logits.shape, 1)
    logits = jnp.where(lane < N_EXPERTS, logits, -jnp.inf)
    m1 = jnp.max(logits, axis=-1, keepdims=True)
    i1 = jnp.min(jnp.where(logits == m1, lane, LANES), axis=-1, keepdims=True)
    rest = jnp.where(lane == i1, -jnp.inf, logits)
    m2 = jnp.max(rest, axis=-1, keepdims=True)
    i2 = jnp.min(jnp.where(rest == m2, lane, LANES), axis=-1, keepdims=True)
    e2 = jnp.exp(m2 - m1)
    w1 = 1.0 / (1.0 + e2)
    w2 = e2 / (1.0 + e2)
    chosen = jnp.where((lane == i1) | (lane == i2), 1.0, 0.0)
    tm = h.shape[0]
    earlier = (lax.broadcasted_iota(jnp.int32, (tm, tm), 0)
               > lax.broadcasted_iota(jnp.int32, (tm, tm), 1))
    rank = jnp.dot(jnp.where(earlier, 1.0, 0.0).astype(BF16), chosen.astype(BF16),
                   preferred_element_type=F32)
    counts = jnp.sum(chosen, axis=0, keepdims=True)
    padded = jnp.ceil(counts * (1.0 / SEG_ALIGN)) * SEG_ALIGN
    before = (lax.broadcasted_iota(jnp.int32, (LANES, LANES), 0)
              < lax.broadcasted_iota(jnp.int32, (LANES, LANES), 1))
    start = jnp.dot(jnp.broadcast_to(padded, (SUBLANES, LANES)).astype(BF16),
                    jnp.where(before, 1.0, 0.0).astype(BF16), preferred_element_type=F32)[0:1]
    row = rank + start
    d1 = jnp.sum(jnp.where(lane == i1, row, 0.0), axis=-1, keepdims=True)
    d2 = jnp.sum(jnp.where(lane == i2, row, 0.0), axis=-1, keepdims=True)
    info = jnp.where(lane == i1, w1, jnp.where(lane == i2, w2, 0.0))
    for k, val in enumerate((i1.astype(F32), i2.astype(F32), w1, w2, d1, d2)):
        info = jnp.where(lane == ROUTE_IDX_LANE + k, val, info)
    info_ref[...] = info
    infot_ref[...] = info.T
    cnt_ref[...] = counts


def _outproj_kernel(gate_ref, *refs):
    state_refs, refs = refs[:2 * LRU_GROUPS], refs[2 * LRU_GROUPS:]
    (w_ref, x_ref, mod_ref, g_ref, b_ref, wr_ref,
     out_ref, h_ref, info_ref, infot_ref, cnt_ref) = refs
    d = x_ref.shape[-1]
    gt = gate_ref[...]
    gelu = 0.5 * gt * (1.0 + jnp.tanh(0.7978845608028654 * (gt + 0.044715 * (gt * gt * gt))))
    states = jnp.concatenate([state_refs[g][...] + state_refs[LRU_GROUPS + g][...]
                              for g in range(LRU_GROUPS)], axis=1)
    y = gelu * states
    proj = jnp.dot(y.astype(BF16), w_ref[...], preferred_element_type=F32)
    xn = _layer_norm(ALPHA * x_ref[...] + _mod_slice(mod_ref, 2, d) * proj, g_ref[...], b_ref[...])
    out_ref[...] = xn
    h = xn * (1.0 + _mod_slice(mod_ref, 4, d)) + _mod_slice(mod_ref, 3, d)
    h_ref[...] = h.astype(BF16)
    _route(h, wr_ref, info_ref, infot_ref, cnt_ref)


def _outproj_call(gate, hf, hb, w_out, x, mod, ln_g, ln_b, router_padded, rows):
    b, l, d = x.shape
    r = gate.shape[-1]
    tm = rows
    nt = l // tm
    tmajor = pl.BlockSpec((tm, r // LRU_GROUPS), lambda i, j: (j, i))
    return pl.pallas_call(
        _outproj_kernel,
        grid=(b, nt),
        in_specs=[
            pl.BlockSpec((None, tm, r), lambda i, j: (i, j, 0)),
            *([tmajor] * (2 * LRU_GROUPS)),
            pl.BlockSpec(w_out.shape, lambda i, j: (0, 0)),
            pl.BlockSpec((None, tm, d), lambda i, j: (i, j, 0)),
            pl.BlockSpec((None, 1, 6 * d), lambda i, j: (i, 0, 0)),
            pl.BlockSpec((1, d), lambda i, j: (0, 0)),
            pl.BlockSpec((1, d), lambda i, j: (0, 0)),
            pl.BlockSpec(router_padded.shape, lambda i, j: (0, 0)),
        ],
        out_specs=[
            pl.BlockSpec((None, tm, d), lambda i, j: (i, j, 0)),
            pl.BlockSpec((tm, d), lambda i, j: (i * nt + j, 0)),
            pl.BlockSpec((tm, LANES), lambda i, j: (i * nt + j, 0)),
            pl.BlockSpec((None, LANES, tm), lambda i, j: (i * nt + j, 0, 0)),
            pl.BlockSpec((None, 1, LANES), lambda i, j: (i * nt + j, 0, 0)),
        ],
        out_shape=[jax.ShapeDtypeStruct((b, l, d), F32), jax.ShapeDtypeStruct((b * l, d), BF16),
                   jax.ShapeDtypeStruct((b * l, LANES), F32),
                   jax.ShapeDtypeStruct((b * nt, LANES, tm), F32),
                   jax.ShapeDtypeStruct((b * nt, 1, LANES), F32)],
        compiler_params=_cparams(("arbitrary", "arbitrary"), VMEM_LIMIT_BYTES),
        name="lru_out_proj_route",
    )(gate, *hf, *hb, w_out, x, mod, ln_g, ln_b, router_padded)


SEG_ALIGN = SUBLANES
MAX_SEG_PAD = N_EXPERTS * (SEG_ALIGN - 1)
BF16_ROWS = 2 * SUBLANES


def _window_rows(window):
    return -(-(TOP_K * window + MAX_SEG_PAD) // BF16_ROWS) * BF16_ROWS


def _for_each_piece(seg_refs, w, window, fn):
    loff_ref, goff_ref, cpad_ref = seg_refs
    for e in range(N_EXPERTS):
        lo, go, c = (ref[w * N_EXPERTS + e] for ref in (loff_ref, goff_ref, cpad_ref))
        size = window
        while size >= SEG_ALIGN:
            done = c & ~(2 * size - 1)

            @pl.when((c & size) != 0)
            def _(lo=lo, go=go, done=done, size=size):
                fn(pl.multiple_of(lo + done, SEG_ALIGN), pl.multiple_of(go + done, SEG_ALIGN), size)

            size //= 2


def _for_each_tail_piece(tail_ref, len_ref, max_size, fn):
    for e in range(N_EXPERTS):
        go, c = tail_ref[e], len_ref[e]
        size = max_size
        while size >= SEG_ALIGN:
            done = c & ~(2 * size - 1)

            @pl.when((c & size) != 0)
            def _(go=go, done=done, size=size):
                fn(pl.multiple_of(go + done, SEG_ALIGN), size)

            size //= 2


def _dispatch_kernel(loff_ref, goff_ref, cpad_ref, tail_ref, tlen_ref, rest_ref, infot_ref,
                     h_ref, xs_hbm, buf, zbuf, sems):
    w, nw = pl.program_id(0), pl.num_programs(0)
    slot = w % 2
    window, d = h_ref.shape
    r = buf.shape[1]
    zrows = zbuf.shape[0]
    segs = (loff_ref, goff_ref, cpad_ref)

    def copy(buf_slot, lrow, grow, size):
        return pltpu.make_async_copy(buf.at[buf_slot, pl.ds(lrow, size)],
                                     xs_hbm.at[pl.ds(grow, size)], sems.at[buf_slot])

    def fill(grow, size):
        return pltpu.make_async_copy(zbuf.at[pl.ds(0, size)], xs_hbm.at[pl.ds(grow, size)],
                                     sems.at[2])

    def for_each_fill(act):
        _for_each_tail_piece(tail_ref, tlen_ref, zrows, lambda g, s: act(fill(g, s)))

        def body(k, _):
            act(fill(pl.multiple_of(rest_ref[0] + k * zrows, SEG_ALIGN), zrows))
            return 0
        lax.fori_loop(0, (xs_hbm.shape[0] - rest_ref[0]) // zrows, body, 0)

    @pl.when(w == 0)
    def _():
        zbuf[...] = jnp.zeros_like(zbuf)
        for_each_fill(lambda cp: cp.start())

    row = lax.broadcasted_iota(jnp.int32, (r, window), 0)
    dest = infot_ref[ROUTE_ROW_LANE:ROUTE_ROW_LANE + TOP_K, :].astype(jnp.int32)
    wts = infot_ref[ROUTE_W_LANE:ROUTE_W_LANE + TOP_K, :]
    hit0 = row == dest[0:1, :]
    hit1 = row == dest[1:2, :]
    perm = jnp.where(hit0 | hit1, 1.0, 0.0).astype(BF16)
    buf[slot, :, :d] = jnp.dot(perm, h_ref[...], preferred_element_type=F32)
    roww = jnp.sum(jnp.where(hit0, wts[0:1, :], 0.0) + jnp.where(hit1, wts[1:2, :], 0.0),
                   axis=1, keepdims=True)
    buf[slot, :, d:] = jnp.broadcast_to(roww, (r, LANES))

    _for_each_piece(segs, w, window, lambda l, g, s: copy(slot, l, g, s).start())

    @pl.when(w > 0)
    def _():
        _for_each_piece(segs, w - 1, window, lambda l, g, s: copy(1 - slot, l, g, s).wait())

    @pl.when(w == nw - 1)
    def _():
        _for_each_piece(segs, w, window, lambda l, g, s: copy(slot, l, g, s).wait())
        for_each_fill(lambda cp: cp.wait())


def _dispatch_call(plan, h, window, rows, n_rows):
    n, d = h.shape
    nw = n // window
    r = _window_rows(window)
    width = d + LANES
    grid_spec = pltpu.PrefetchScalarGridSpec(
        num_scalar_prefetch=6,
        grid=(nw,),
        in_specs=[
            pl.BlockSpec((None, LANES, window), lambda w, *_: (w, 0, 0)),
            pl.BlockSpec((window, d), lambda w, *_: (w, 0)),
        ],
        out_specs=pl.BlockSpec(memory_space=pl.ANY),
        scratch_shapes=[pltpu.VMEM((2, r, width), F32), pltpu.VMEM((rows // 2, width), F32),
                        pltpu.SemaphoreType.DMA((3,))],
    )
    return pl.pallas_call(
        _dispatch_kernel,
        grid_spec=grid_spec,
        out_shape=jax.ShapeDtypeStruct((n_rows, width), F32),
        compiler_params=_cparams(("arbitrary",), VMEM_LIMIT_BYTES),
        name="expert_dispatch",
    )(plan["loff"], plan["goff"], plan["cpad"], plan["tail"], plan["tail_len"], plan["rest"],
      plan["info_t"], h)


def _expert_kernel(te_ref, nv_ref, xs_ref, wg_ref, wu_ref, wd_ref, y_ref, hb_scr, acc_scr, w_scr):
    i, f = pl.program_id(0), pl.program_id(1)
    nv = nv_ref[0]
    d = y_ref.shape[-1]

    def chunk(hb):
        return _swiglu_chunk(hb, wg_ref[...].astype(BF16), wu_ref[...].astype(BF16),
                             wd_ref[...].astype(BF16))

    def finish_previous():
        y_ref[...] = acc_scr[...] * w_scr[:, 0:1]

    def start_current():
        hb = xs_ref[:, :d].astype(BF16)
        hb_scr[...] = hb
        w_scr[...] = xs_ref[:, d:]
        acc_scr[...] = chunk(hb)

    @pl.when((f == 0) & (i == 0))
    def _():
        start_current()

    @pl.when((f == 0) & (i > 0) & (i < nv))
    def _():
        finish_previous()
        start_current()

    @pl.when((f == 0) & (i > 0) & (i == nv))
    def _():
        finish_previous()

    @pl.when((f == 0) & (i > nv))
    def _():
        y_ref[...] = jnp.zeros_like(y_ref)

    @pl.when((f > 0) & (i < nv))
    def _():
        acc_scr[...] += chunk(hb_scr[...])


def _expert_call(plan, xs, w_gu, w_dn, rows, cols):
    n_rows, width = xs.shape
    d = width - LANES
    nt = n_rows // rows
    ff = w_dn.shape[1]
    fc = cols
    nf = ff // fc
    tile = lambda i, nv: jnp.minimum(i, nv[0] - 1)
    chunk = lambda i, f, nv: jnp.where(i < nv[0], f, nf - 1)
    grid_spec = pltpu.PrefetchScalarGridSpec(
        num_scalar_prefetch=2,
        grid=(nt + 1, nf),
        in_specs=[
            pl.BlockSpec((rows, width), lambda i, f, te, nv: (tile(i, nv), 0)),
            pl.BlockSpec((None, d, fc), lambda i, f, te, nv: (te[tile(i, nv)], 0, chunk(i, f, nv))),
            pl.BlockSpec((None, d, fc),
                         lambda i, f, te, nv: (te[tile(i, nv)], 0, nf + chunk(i, f, nv))),
            pl.BlockSpec((None, fc, d), lambda i, f, te, nv: (te[tile(i, nv)], chunk(i, f, nv), 0)),
        ],
        out_specs=pl.BlockSpec((rows, d), lambda i, f, te, nv: (jnp.maximum(i - 1, 0), 0)),
        scratch_shapes=[pltpu.VMEM((rows, d), BF16), pltpu.VMEM((rows, d), F32),
                        pltpu.VMEM((rows, LANES), F32)],
    )
    return pl.pallas_call(
        _expert_kernel,
        grid_spec=grid_spec,
        out_shape=jax.ShapeDtypeStruct((n_rows, d), F32),
        compiler_params=_cparams(("arbitrary", "arbitrary"), VMEM_LIMIT_BYTES),
        name="expert_swiglu",
    )(plan["tile_expert"], plan["n_valid"], xs, w_gu, w_gu, w_dn)


def _combine_kernel(loff_ref, goff_ref, cpad_ref, info_ref, y_hbm, x_ref, mod_ref, g_ref, b_ref,
                    out_ref, ybuf, sems):
    w, nw = pl.program_id(0), pl.num_programs(0)
    slot = w % 2
    window, d = x_ref.shape
    r = ybuf.shape[1]
    segs = (loff_ref, goff_ref, cpad_ref)

    def copy(buf_slot, lrow, grow, size):
        return pltpu.make_async_copy(y_hbm.at[pl.ds(grow, size)],
                                     ybuf.at[buf_slot, pl.ds(lrow, size)], sems.at[buf_slot])

    @pl.when(w == 0)
    def _():
        ybuf[...] = jnp.zeros_like(ybuf)
        _for_each_piece(segs, 0, window, lambda l, g, s: copy(0, l, g, s).start())

    @pl.when(w + 1 < nw)
    def _():
        _for_each_piece(segs, w + 1, window, lambda l, g, s: copy(1 - slot, l, g, s).start())

    _for_each_piece(segs, w, window, lambda l, g, s: copy(slot, l, g, s).wait())

    lane = lax.broadcasted_iota(jnp.int32, (window, r), 1)
    src = info_ref[:, ROUTE_ROW_LANE:ROUTE_ROW_LANE + TOP_K].astype(jnp.int32)
    hit = (lane == src[:, 0:1]) | (lane == src[:, 1:2])
    perm = jnp.where(hit, 1.0, 0.0).astype(BF16)
    hi, lo = _split_bf16(ybuf[slot])
    f = (jnp.dot(perm, hi, preferred_element_type=F32)
         + jnp.dot(perm, lo, preferred_element_type=F32))
    out_ref[...] = _layer_norm(ALPHA * x_ref[...] + _mod_slice(mod_ref, 5, d) * f,
                               g_ref[...], b_ref[...])


def _combine_call(plan, y, x, mod, ln_g, ln_b, window):
    b, l, d = x.shape
    wpb = l // window
    r = _window_rows(window)
    grid_spec = pltpu.PrefetchScalarGridSpec(
        num_scalar_prefetch=3,
        grid=(b * wpb,),
        in_specs=[
            pl.BlockSpec((window, LANES), lambda w, *_: (w, 0)),
            pl.BlockSpec(memory_space=pl.ANY),
            pl.BlockSpec((None, window, d), lambda w, *_: (w // wpb, w % wpb, 0)),
            pl.BlockSpec((None, 1, 6 * d), lambda w, *_: (w // wpb, 0, 0)),
            pl.BlockSpec((1, d), lambda w, *_: (0, 0)),
            pl.BlockSpec((1, d), lambda w, *_: (0, 0)),
        ],
        out_specs=pl.BlockSpec((None, window, d), lambda w, *_: (w // wpb, w % wpb, 0)),
        scratch_shapes=[pltpu.VMEM((2, r, d), F32), pltpu.SemaphoreType.DMA((2,))],
    )
    return pl.pallas_call(
        _combine_kernel,
        grid_spec=grid_spec,
        out_shape=jax.ShapeDtypeStruct((b, l, d), F32),
        compiler_params=_cparams(("arbitrary",), VMEM_LIMIT_BYTES),
        name="expert_combine",
    )(plan["loff"], plan["goff"], plan["cpad"], plan["info"], y, x, mod, ln_g, ln_b)


def _dispatch_plan(info, info_t, counts, window, rows):
    n = info.shape[0]
    nw = n // window
    count = counts[:, 0, :N_EXPERTS].astype(jnp.int32)
    cpad = -(-count // SEG_ALIGN) * SEG_ALIGN
    loff = jnp.cumsum(cpad, axis=1) - cpad
    group = -(-jnp.sum(cpad, axis=0) // rows) * rows
    gend = jnp.cumsum(group)
    goff = (gend - group)[None, :] + jnp.cumsum(cpad, axis=0) - cpad
    n_tiles = -(-(TOP_K * n + nw * MAX_SEG_PAD) // rows) + N_EXPERTS
    tile_start = jnp.arange(n_tiles) * rows
    tile_expert = jnp.minimum(jnp.sum(tile_start[:, None] >= gend[None, :], axis=1), N_EXPERTS - 1)
    flat = lambda a: a.reshape(-1).astype(jnp.int32)
    return dict(
        loff=flat(loff), goff=flat(goff), cpad=flat(cpad),
        info=info, info_t=info_t,
        tail=flat(goff[-1] + cpad[-1]), tail_len=flat(gend - goff[-1] - cpad[-1]),
        rest=flat(gend[-1:]),
        tile_expert=tile_expert.astype(jnp.int32),
        n_valid=(gend[-1] // rows).reshape(1).astype(jnp.int32),
        n_rows=n_tiles * rows,
    )


def _attention_layer(x, ctx, mod_x, mod_c, ln_g, ln_b, w_qkv, w_o, sink, w_gu, w_dn, grid_w, ts):
    b, l, d = x.shape
    lc = ctx.shape[1]
    g0, b0, g1, b1 = ln_g[0:1], ln_b[0:1], ln_g[1:2], ln_b[1:2]
    w_qkv, w_o, w_gu, w_dn = (w.astype(BF16) for w in (w_qkv, w_o, w_gu, w_dn))
    ctx_flat = ctx.reshape(1, b * lc, d)
    n_kv = N_KV_HEADS * HEAD_DIM

    q, k, v = _qkv_call(x, mod_x, w_qkv, _rope_tables(l, grid_w), ts["qkv_rows"])
    qc, kc, vc = _qkv_call(ctx_flat, mod_c, w_qkv, _no_rope_tables(b * lc), ts["qkv_rows"])
    qc = qc.reshape(b, lc, -1)
    kc, vc = kc.reshape(b, lc, n_kv), vc.reshape(b, lc, n_kv)
    mod_cb = jnp.broadcast_to(mod_c, (b,) + mod_c.shape[1:])

    x = _attn_call(q, k, v, kc, vc, sink, w_o, x, mod_x, g0, b0, band=True)
    ctx = _attn_call(qc, None, None, kc, vc, sink, w_o, ctx, mod_cb, g0, b0, band=False)

    x = _ffn_call(x, mod_x, w_gu, w_dn, g1, b1, ts["ffn_rows"], ts["ffn_cols"])
    ctx = _ffn_call(ctx.reshape(1, b * lc, d), mod_c, w_gu, w_dn, g1, b1,
                    ts["ffn_rows"], ts["ffn_cols"]).reshape(b, lc, d)
    return x, ctx


def _recurrent_layer(x, ctx, mod_x, mod_c, ln_g, ln_b, w_in, conv_w, conv_b, lam, w_r, b_r, w_i,
                     b_i, w_out, router, w_gu, w_dn, ts):
    b, l, d = x.shape
    g0, b0, g1, b1 = ln_g[0:1], ln_b[0:1], ln_g[1:2], ln_b[1:2]
    w_in, w_out = w_in.astype(BF16), w_out.astype(BF16)
    wd = _paired_block_diag(w_r, w_i)
    conv_b = conv_b.reshape(1, -1)
    mod_cb = jnp.broadcast_to(mod_c, (b,) + mod_c.shape[1:])

    r = w_out.shape[0]
    rg = r // LRU_GROUPS
    gate, *xb = _inproj_call(x, mod_x, w_in, ts["proj_rows"])
    _, *xb_c = _inproj_call(ctx, mod_cb, w_in, ts["proj_rows"])
    zero = jnp.zeros((b, rg), F32)
    hf, hb = [], []
    for g in range(LRU_GROUPS):
        cs = slice(g * rg, (g + 1) * rg)
        pairs = slice(g * rg // (2 * LRU_BLOCK_DIM), (g + 1) * rg // (2 * LRU_BLOCK_DIM))
        scan = functools.partial(_scan_call, conv_w=conv_w[:, cs], conv_b=conv_b[:, cs],
                                 wd=wd[:, pairs], b_r=0.5 * b_r[:, cs], b_i=0.5 * b_i[:, cs],
                                 lam=lam[:, cs], steps=ts["scan_steps"])
        _, _, cf, cb = scan(xb_c[g].reshape(-1, b, rg), zero, zero)
        hf_g, hb_g, _, _ = scan(xb[g].reshape(l, b, rg), cf, cb)
        hf.append(hf_g.reshape(l, b * rg))
        hb.append(hb_g.reshape(l, b * rg))
    router_padded = jnp.concatenate(
        _split_bf16(jnp.pad(router, ((0, 0), (0, LANES - N_EXPERTS)))), axis=1)
    x, h, info, info_t, counts = _outproj_call(gate, hf, hb, w_out, x, mod_x, g0, b0,
                                               router_padded, ts["moe_window"])
    plan = _dispatch_plan(info, info_t, counts, ts["moe_window"], ts["moe_rows"])
    xs = _dispatch_call(plan, h, ts["moe_window"], ts["moe_rows"], plan["n_rows"])
    y = _expert_call(plan, xs, w_gu, w_dn, ts["moe_rows"], ts["ffn_cols"])
    return _combine_call(plan, y, x, mod_x, g1, b1, ts["moe_window"])


def kernel(x, c, ctx, c_ctx, w_mod, b_mod, ln_g, ln_b, attn_w_qkv, attn_w_o, attn_sink, ffn_w_gu,
           ffn_w_dn, lru_w_in, lru_conv_w, lru_conv_b, lru_lambda, lru_w_r, lru_b_r, lru_w_i,
           lru_b_i, lru_w_out, moe_router, moe_w_gu, moe_w_dn):
    b, l, d = x.shape
    assert w_mod.shape[0] == DEPTH and l % BLOCK == 0 and ctx.shape[1] % BLOCK == 0
    grid_w = 64
    ts = _tile_sizes()

    rows = -(-(b + 1) // SUBLANES) * SUBLANES
    cond = jnp.zeros((rows, d), F32).at[:b].set(c).at[b].set(c_ctx)
    mod = _mod_call(cond, w_mod, b_mod[:, None, :])
    mod_x = lambda i: mod[i, :b, None, :]
    mod_c = lambda i: mod[i, b:b + 1, None, :]

    x, ctx = _attention_layer(x, ctx, mod_x(0), mod_c(0), ln_g[0], ln_b[0], attn_w_qkv[0],
                              attn_w_o[0], attn_sink[0], ffn_w_gu[0], ffn_w_dn[0], grid_w, ts)
    return _recurrent_layer(x, ctx, mod_x(1), mod_c(1), ln_g[1], ln_b[1], lru_w_in[0],
                            lru_conv_w[0], lru_conv_b[0], lru_lambda[0], lru_w_r[0], lru_b_r[0],
                            lru_w_i[0], lru_b_i[0], lru_w_out[0], moe_router[0], moe_w_gu[0],
                            moe_w_dn[0], ts)
```

```python
import functools

import jax
import jax.numpy as jnp
import numpy as np
from jax import lax
from jax.experimental import pallas as pl
from jax.experimental.pallas import tpu as pltpu

N_HEADS = 8
N_KV_HEADS = 2
GROUP = N_HEADS // N_KV_HEADS
HEAD_DIM = 128
BLOCK = 128
AXIS_DIM = HEAD_DIM // 2
ROPE_BASE = 10000.0
LRU_BLOCKS = 8
LRU_BLOCK_DIM = 128
LRU_C = 8.0
N_EXPERTS = 8
TOP_K = 2
DEPTH = 2
ALPHA = (2 * DEPTH) ** 0.25
LN_EPS = 1e-5
NEG_INF = -1e30

LANES = 128
SUBLANES = 8
VMEM_LIMIT_BYTES = 56 * 1024 * 1024

F32 = jnp.float32
BF16 = jnp.bfloat16


def _tile_sizes():
    return dict(
        qkv_rows=1024,
        ffn_rows=1024,
        ffn_cols=512,
        proj_rows=1024,
        scan_steps=128,
        moe_window=512,
        moe_rows=1024,
    )


def _cparams(sem, vmem=None):
    return pltpu.CompilerParams(dimension_semantics=sem, vmem_limit_bytes=vmem)


def _layer_norm(h, g, b):
    mu = jnp.mean(h, axis=-1, keepdims=True)
    d = h - mu
    var = jnp.mean(d * d, axis=-1, keepdims=True)
    return d * lax.rsqrt(var + LN_EPS) * g + b


def _mod_slice(mod_ref, k, d):
    return mod_ref[:, k * d:(k + 1) * d]


def _mod_kernel(c_ref, w_ref, b_ref, o_ref):
    c = c_ref[...]
    s = c * jax.nn.sigmoid(c)
    o_ref[...] = jnp.dot(s, w_ref[...], preferred_element_type=F32,
                         precision=lax.Precision.HIGHEST) + b_ref[...]


def _mod_call(cond, w_mod, b_mod):
    rows, d = cond.shape
    depth, _, n = w_mod.shape
    tn = 1536
    return pl.pallas_call(
        _mod_kernel,
        grid=(depth, n // tn),
        in_specs=[
            pl.BlockSpec((rows, d), lambda i, j: (0, 0)),
            pl.BlockSpec((None, d, tn), lambda i, j: (i, 0, j)),
            pl.BlockSpec((None, 1, tn), lambda i, j: (i, 0, j)),
        ],
        out_specs=pl.BlockSpec((None, rows, tn), lambda i, j: (i, 0, j)),
        out_shape=jax.ShapeDtypeStruct((depth, rows, n), F32),
        compiler_params=_cparams(("arbitrary", "arbitrary")),
        name="mod_vectors",
    )(cond, w_mod, b_mod)


def _qkv_kernel(x_ref, mod_ref, w_ref, cos_ref, sa_ref, sb_ref, q_ref, k_ref, v_ref):
    d = x_ref.shape[-1]
    h = x_ref[...] * (1.0 + _mod_slice(mod_ref, 1, d)) + _mod_slice(mod_ref, 0, d)
    qkv = jnp.dot(h.astype(BF16), w_ref[...], preferred_element_type=F32)
    cos, sa, sb = cos_ref[...], sa_ref[...], sb_ref[...]
    scale = HEAD_DIM ** -0.5
    for head in range(N_HEADS + N_KV_HEADS):
        u = qkv[:, head * HEAD_DIM:(head + 1) * HEAD_DIM]
        r = (u * cos + pltpu.roll(u, HEAD_DIM - AXIS_DIM // 2, axis=1) * sa
             + pltpu.roll(u, AXIS_DIM // 2, axis=1) * sb)
        if head < N_HEADS:
            q_ref[:, head * HEAD_DIM:(head + 1) * HEAD_DIM] = (r * scale).astype(BF16)
        else:
            kh = head - N_HEADS
            k_ref[:, kh * HEAD_DIM:(kh + 1) * HEAD_DIM] = r.astype(BF16)
    kv = N_KV_HEADS * HEAD_DIM
    v_ref[...] = qkv[:, N_HEADS * HEAD_DIM + kv:].astype(BF16)


def _qkv_call(x, mod, w_qkv, tabs, rows):
    b, l, d = x.shape
    n_q, n_kv = N_HEADS * HEAD_DIM, N_KV_HEADS * HEAD_DIM
    tm = min(rows, l)
    tab_spec = pl.BlockSpec((tm, HEAD_DIM), lambda i, j: (j, 0))
    return pl.pallas_call(
        _qkv_kernel,
        grid=(b, l // tm),
        in_specs=[
            pl.BlockSpec((None, tm, d), lambda i, j: (i, j, 0)),
            pl.BlockSpec((None, 1, 6 * d), lambda i, j: (i, 0, 0)),
            pl.BlockSpec(w_qkv.shape, lambda i, j: (0, 0)),
            tab_spec, tab_spec, tab_spec,
        ],
        out_specs=[
            pl.BlockSpec((None, tm, n_q), lambda i, j: (i, j, 0)),
            pl.BlockSpec((None, tm, n_kv), lambda i, j: (i, j, 0)),
            pl.BlockSpec((None, tm, n_kv), lambda i, j: (i, j, 0)),
        ],
        out_shape=[
            jax.ShapeDtypeStruct((b, l, n_q), BF16),
            jax.ShapeDtypeStruct((b, l, n_kv), BF16),
            jax.ShapeDtypeStruct((b, l, n_kv), BF16),
        ],
        compiler_params=_cparams(("arbitrary", "arbitrary"), VMEM_LIMIT_BYTES),
        name="qkv_rope",
    )(x, mod, w_qkv, *tabs)


def _rope_tables(l, grid_w):
    f32 = np.float32
    t = np.arange(l)
    pos = np.stack([(t // grid_w).astype(f32), (t % grid_w).astype(f32)], axis=1)
    freqs = f32(ROPE_BASE) ** (-np.arange(0, AXIS_DIM, 2, dtype=f32) / f32(AXIS_DIM))
    lane = np.arange(HEAD_DIM)
    ang = (pos[:, lane // AXIS_DIM] * freqs[lane % (AXIS_DIM // 2)][None, :]).astype(f32)
    first = (lane % AXIS_DIM) < AXIS_DIM // 2
    cos, sin = np.cos(ang).astype(f32), np.sin(ang).astype(f32)
    zero = f32(0.0)
    return cos, np.where(first, -sin, zero), np.where(first, zero, sin)


def _no_rope_tables(l):
    z = np.zeros((l, HEAD_DIM), np.float32)
    return np.ones((l, HEAD_DIM), np.float32), z, z


def _nt_dot(a, b):
    return lax.dot_general(a, b, (((1,), (1,)), ((), ())), preferred_element_type=F32)


def _attend(qs, segs, sink_col):
    k_all = jnp.concatenate([k for k, _, _ in segs], axis=0)
    v_all = jnp.concatenate([v for _, v, _ in segs], axis=0)
    v_ext = jnp.concatenate([v_all, jnp.ones_like(v_all)], axis=1)
    s_all = _nt_dot(qs, k_all)
    tiles, col = [], 0
    for k, _, mask in segs:
        for c in range(col, col + k.shape[0], LANES):
            s = s_all[:, c:c + LANES]
            tiles.append(s if mask is None else jnp.where(mask, s, NEG_INF))
        col += k.shape[0]
    m_tile = tiles[0]
    for s in tiles[1:]:
        m_tile = jnp.maximum(m_tile, s)
    m = jnp.maximum(sink_col, jnp.max(m_tile, axis=-1, keepdims=True))
    p_all = jnp.concatenate([jnp.exp(s - m).astype(BF16) for s in tiles], axis=1)
    res = jnp.dot(p_all, v_ext, preferred_element_type=F32)
    hd = v_all.shape[1]
    return res[:, :hd] / (res[:, hd:hd + 1] + jnp.exp(sink_col - m))


def _attn_epilogue(o_scr, wo_ref, x_ref, mod_ref, g_ref, b_ref, out_ref):
    d = x_ref.shape[-1]
    proj = jnp.dot(o_scr[...], wo_ref[...], preferred_element_type=F32)
    out_ref[...] = _layer_norm(ALPHA * x_ref[...] + _mod_slice(mod_ref, 2, d) * proj,
                               g_ref[...], b_ref[...])


def _sink_column(sink_ref, kvh):
    return jnp.concatenate(
        [jnp.full((BLOCK, 1), sink_ref[kvh * GROUP + g], F32) for g in range(GROUP)], axis=0)


def _stack_group(q_ref, kvh):
    return jnp.concatenate(
        [q_ref[:, (kvh * GROUP + g) * HEAD_DIM:(kvh * GROUP + g + 1) * HEAD_DIM]
         for g in range(GROUP)], axis=0)


def _unstack_group(o, o_scr, kvh):
    for g in range(GROUP):
        col = (kvh * GROUP + g) * HEAD_DIM
        o_scr[:, col:col + HEAD_DIM] = o[g * BLOCK:(g + 1) * BLOCK].astype(BF16)


def _band_attn_kernel(sink_ref, q_ref, kp_ref, kc_ref, kn_ref, vp_ref, vc_ref, vn_ref,
                      kx_ref, vx_ref, wo_ref, x_ref, mod_ref, g_ref, b_ref, out_ref, o_scr):
    n, nb = pl.program_id(1), pl.num_programs(1)
    row = lax.broadcasted_iota(jnp.int32, (GROUP * BLOCK, BLOCK), 0) % BLOCK
    col = lax.broadcasted_iota(jnp.int32, (GROUP * BLOCK, BLOCK), 1)
    mask_prev = (col >= row) & (n > 0)
    mask_next = (col <= row) & (n < nb - 1)
    for kvh in range(N_KV_HEADS):
        hs = slice(kvh * HEAD_DIM, (kvh + 1) * HEAD_DIM)
        segs = [(kp_ref[:, hs], vp_ref[:, hs], mask_prev),
                (kc_ref[:, hs], vc_ref[:, hs], None),
                (kn_ref[:, hs], vn_ref[:, hs], mask_next),
                (kx_ref[:, hs], vx_ref[:, hs], None)]
        o = _attend(_stack_group(q_ref, kvh), segs, _sink_column(sink_ref, kvh))
        _unstack_group(o, o_scr, kvh)
    _attn_epilogue(o_scr, wo_ref, x_ref, mod_ref, g_ref, b_ref, out_ref)


def _ctx_attn_kernel(sink_ref, q_ref, kx_ref, vx_ref, wo_ref, x_ref, mod_ref, g_ref, b_ref,
                     out_ref, o_scr):
    for kvh in range(N_KV_HEADS):
        hs = slice(kvh * HEAD_DIM, (kvh + 1) * HEAD_DIM)
        o = _attend(_stack_group(q_ref, kvh), [(kx_ref[:, hs], vx_ref[:, hs], None)],
                    _sink_column(sink_ref, kvh))
        _unstack_group(o, o_scr, kvh)
    _attn_epilogue(o_scr, wo_ref, x_ref, mod_ref, g_ref, b_ref, out_ref)


def _attn_call(q, k, v, kx, vx, sink, w_o, x, mod, ln_g, ln_b, band):
    b, l, d = x.shape
    lc = kx.shape[1]
    nb = l // BLOCK
    n_q, n_kv = N_HEADS * HEAD_DIM, N_KV_HEADS * HEAD_DIM
    kv_blk = lambda f: pl.BlockSpec((None, BLOCK, n_kv), f)
    common_in = [
        pl.BlockSpec((None, lc, n_kv), lambda i, j: (i, 0, 0)),
        pl.BlockSpec((None, lc, n_kv), lambda i, j: (i, 0, 0)),
        pl.BlockSpec(w_o.shape, lambda i, j: (0, 0)),
        pl.BlockSpec((None, BLOCK, d), lambda i, j: (i, j, 0)),
        pl.BlockSpec((None, 1, 6 * d), lambda i, j: (i, 0, 0)),
        pl.BlockSpec((1, d), lambda i, j: (0, 0)),
        pl.BlockSpec((1, d), lambda i, j: (0, 0)),
    ]
    head_in = [pl.BlockSpec(memory_space=pltpu.SMEM),
               pl.BlockSpec((None, BLOCK, n_q), lambda i, j: (i, j, 0))]
    if band:
        prev = lambda i, j: (i, jnp.maximum(j - 1, 0), 0)
        cur = lambda i, j: (i, j, 0)
        nxt = lambda i, j: (i, jnp.minimum(j + 1, nb - 1), 0)
        in_specs = head_in + [kv_blk(prev), kv_blk(cur), kv_blk(nxt)] * 2 + common_in
        args = (sink, q, k, k, k, v, v, v, kx, vx, w_o, x, mod, ln_g, ln_b)
        body, name = _band_attn_kernel, "band_attention"
    else:
        in_specs = head_in + common_in
        args = (sink, q, kx, vx, w_o, x, mod, ln_g, ln_b)
        body, name = _ctx_attn_kernel, "context_attention"
    return pl.pallas_call(
        body,
        grid=(b, nb),
        in_specs=in_specs,
        out_specs=pl.BlockSpec((None, BLOCK, d), lambda i, j: (i, j, 0)),
        out_shape=jax.ShapeDtypeStruct((b, l, d), F32),
        scratch_shapes=[pltpu.VMEM((BLOCK, n_q), BF16)],
        compiler_params=_cparams(("arbitrary", "arbitrary"), VMEM_LIMIT_BYTES),
        name=name,
    )(*args)


def _swiglu_chunk(hb, wg, wu, wd):
    g = jnp.dot(hb, wg, preferred_element_type=F32)
    u = jnp.dot(hb, wu, preferred_element_type=F32)
    a = (g * jax.nn.sigmoid(g)) * u
    return jnp.dot(a.astype(BF16), wd, preferred_element_type=F32)


def _ffn_kernel(xc_ref, xp_ref, modc_ref, modp_ref, wg_ref, wu_ref, wd_ref, g_ref, b_ref, out_ref,
                hb_scr, acc_scr):
    i, f = pl.program_id(0), pl.program_id(1)
    nt = pl.num_programs(0) - 1
    d = xc_ref.shape[-1]

    def finish_previous():
        out_ref[...] = _layer_norm(
            ALPHA * xp_ref[...] + _mod_slice(modp_ref, 5, d) * acc_scr[...], g_ref[...], b_ref[...])

    def start_current():
        h = xc_ref[...] * (1.0 + _mod_slice(modc_ref, 4, d)) + _mod_slice(modc_ref, 3, d)
        hb = h.astype(BF16)
        hb_scr[...] = hb
        acc_scr[...] = _swiglu_chunk(hb, wg_ref[...], wu_ref[...], wd_ref[...])

    @pl.when((f == 0) & (i == 0))
    def _():
        start_current()

    @pl.when((f == 0) & (i > 0) & (i < nt))
    def _():
        finish_previous()
        start_current()

    @pl.when((f == 0) & (i == nt))
    def _():
        finish_previous()

    @pl.when((f > 0) & (i < nt))
    def _():
        acc_scr[...] += _swiglu_chunk(hb_scr[...], wg_ref[...], wu_ref[...], wd_ref[...])


def _ffn_call(x, mod, w_gu, w_dn, ln_g, ln_b, rows, cols):
    b, l, d = x.shape
    ff = w_dn.shape[0]
    tm, fc = min(rows, l), cols
    nf = ff // fc
    lt = l // tm
    nt = b * lt
    cur = lambda i: jnp.minimum(i, nt - 1)
    prev = lambda i: jnp.maximum(i - 1, 0)
    chunk = lambda i, f: jnp.where(i < nt, f, nf - 1)
    x_spec = lambda t: pl.BlockSpec((None, tm, d), lambda i, f: (t(i) // lt, t(i) % lt, 0))
    mod_spec = lambda t: pl.BlockSpec((None, 1, 6 * d), lambda i, f: (t(i) // lt, 0, 0))
    return pl.pallas_call(
        _ffn_kernel,
        grid=(nt + 1, nf),
        in_specs=[
            x_spec(cur), x_spec(prev), mod_spec(cur), mod_spec(prev),
            pl.BlockSpec((d, fc), lambda i, f: (0, chunk(i, f))),
            pl.BlockSpec((d, fc), lambda i, f: (0, nf + chunk(i, f))),
            pl.BlockSpec((fc, d), lambda i, f: (chunk(i, f), 0)),
            pl.BlockSpec((1, d), lambda i, f: (0, 0)),
            pl.BlockSpec((1, d), lambda i, f: (0, 0)),
        ],
        out_specs=x_spec(prev),
        out_shape=jax.ShapeDtypeStruct((b, l, d), F32),
        scratch_shapes=[pltpu.VMEM((tm, d), BF16), pltpu.VMEM((tm, d), F32)],
        compiler_params=_cparams(("arbitrary", "arbitrary"), VMEM_LIMIT_BYTES),
        name="dense_swiglu",
    )(x, x, mod, mod, w_gu, w_gu, w_dn, ln_g, ln_b)


LRU_GROUPS = 2
SCAN_UNROLL = 8


def _inproj_kernel(x_ref, mod_ref, w_ref, gate_ref, *xb_refs):
    d = x_ref.shape[-1]
    r = gate_ref.shape[-1]
    rg = r // len(xb_refs)
    h = x_ref[...] * (1.0 + _mod_slice(mod_ref, 1, d)) + _mod_slice(mod_ref, 0, d)
    y = jnp.dot(h.astype(BF16), w_ref[...], preferred_element_type=F32)
    gate_ref[...] = y[:, :r]
    for g, xb_ref in enumerate(xb_refs):
        xb_ref[...] = y[:, r + g * rg:r + (g + 1) * rg]


def _inproj_call(x, mod, w_in, rows):
    b, l, d = x.shape
    r = w_in.shape[1] // 2
    rg = r // LRU_GROUPS
    tm = min(rows, l)
    return pl.pallas_call(
        _inproj_kernel,
        grid=(b, l // tm),
        in_specs=[
            pl.BlockSpec((None, tm, d), lambda i, j: (i, j, 0)),
            pl.BlockSpec((None, 1, 6 * d), lambda i, j: (i, 0, 0)),
            pl.BlockSpec(w_in.shape, lambda i, j: (0, 0)),
        ],
        out_specs=[pl.BlockSpec((None, tm, r), lambda i, j: (i, j, 0))]
        + [pl.BlockSpec((tm, rg), lambda i, j: (j, i))] * LRU_GROUPS,
        out_shape=[jax.ShapeDtypeStruct((b, l, r), F32)]
        + [jax.ShapeDtypeStruct((l, b * rg), F32)] * LRU_GROUPS,
        compiler_params=_cparams(("arbitrary", "arbitrary"), VMEM_LIMIT_BYTES),
        name="lru_in_proj",
    )(x, mod, w_in)


def _lru_coeffs(main_ref, prev_ref, next_ref, first, last, cw_ref, cb_ref, wd_ref, br_ref, bi_ref,
                lam_ref, a_scr, b_scr):
    tl, nb, r = main_ref.shape
    prev = jnp.where(first, 0.0, prev_ref[...])
    nxt = jnp.where(last, 0.0, next_ref[...])
    ext = jnp.concatenate([prev, main_ref[...], nxt], axis=0)
    u = cb_ref[...][None]
    for k in range(4):
        u = u + cw_ref[k:k + 1, :][None] * ext[k:k + tl]
    u = u.reshape(tl * nb, r)
    ub = u.astype(BF16)
    lam = lam_ref[...]
    softplus_neg = jnp.maximum(-lam, 0.0) + jnp.log1p(jnp.exp(-jnp.abs(lam)))
    half_rate = (-0.5 * LRU_C) * softplus_neg
    pw = 2 * LRU_BLOCK_DIM
    for p in range(r // pw):
        cs = slice(p * pw, (p + 1) * pw)
        z = jnp.dot(ub[:, cs], wd_ref[p], preferred_element_type=F32)
        tr = jnp.tanh(z[:, :pw] + br_ref[:, cs])
        ig = 0.5 + 0.5 * jnp.tanh(z[:, pw:] + bi_ref[:, cs])
        log_a = half_rate[:, cs] * tr + half_rate[:, cs]
        a = jnp.exp(log_a)
        gain = jnp.sqrt(-jnp.tanh(log_a) * (a * a + 1.0))
        a_scr[:, :, cs] = a.reshape(tl, nb, pw)
        b_scr[:, :, cs] = (gain * (ig * u[:, cs])).reshape(tl, nb, pw)


def _scan_kernel(fm_ref, fp_ref, fn_ref, bm_ref, bp_ref, bn_ref, h0f_ref, h0b_ref,
                 cw_ref, cb_ref, wdf_ref, wdb_ref, br_ref, bi_ref, lam_ref,
                 hf_ref, hb_ref, lastf_ref, lastb_ref,
                 af_scr, bf_scr, ab_scr, bb_scr, sf_scr, sb_scr):
    c, nc = pl.program_id(0), pl.num_programs(0)
    tl = fm_ref.shape[0]

    @pl.when(c == 0)
    def _():
        sf_scr[...] = h0f_ref[...]
        sb_scr[...] = h0b_ref[...]

    _lru_coeffs(fm_ref, fp_ref, fn_ref, c == 0, c == nc - 1, cw_ref, cb_ref, wdf_ref,
                br_ref.at[0:1], bi_ref.at[0:1], lam_ref.at[0:1], af_scr, bf_scr)
    _lru_coeffs(bm_ref, bp_ref, bn_ref, c == nc - 1, c == 0, cw_ref, cb_ref, wdb_ref,
                br_ref.at[1:2], bi_ref.at[1:2], lam_ref.at[1:2], ab_scr, bb_scr)

    def step(t, carry):
        hf, hb = carry
        hf = af_scr[t] * hf + bf_scr[t]
        hf_ref[t] = hf
        tb = tl - 1 - t
        hb = ab_scr[tb] * hb + bb_scr[tb]
        hb_ref[tb] = hb
        return hf, hb

    hf, hb = lax.fori_loop(0, tl, step, (sf_scr[...], sb_scr[...]), unroll=SCAN_UNROLL)
    sf_scr[...] = hf
    sb_scr[...] = hb

    @pl.when(c == nc - 1)
    def _():
        lastf_ref[...] = hf
        lastb_ref[...] = hb


def _scan_call(xb, h0f, h0b, conv_w, conv_b, wd, b_r, b_i, lam, steps):
    ls, nb, r = xb.shape
    tl = min(steps, ls)
    nc = ls // tl
    fwd, bwd = (lambda c: c), (lambda c: nc - 1 - c)
    main = lambda ch: pl.BlockSpec((tl, nb, r), lambda c: (ch(c), 0, 0))
    prev2 = lambda ch: pl.BlockSpec((2, nb, r), lambda c: (jnp.maximum(ch(c) * (tl // 2) - 1, 0), 0, 0))
    next1 = lambda ch: pl.BlockSpec((1, nb, r), lambda c: (jnp.minimum((ch(c) + 1) * tl, ls - 1), 0, 0))
    full = lambda a: pl.BlockSpec(a.shape, lambda c: (0,) * a.ndim)
    state = pl.BlockSpec((nb, r), lambda c: (0, 0))
    wd_spec = lambda dirn: pl.BlockSpec((None,) + wd.shape[1:], lambda c: (dirn, 0, 0, 0))
    return pl.pallas_call(
        _scan_kernel,
        grid=(nc,),
        in_specs=[main(fwd), prev2(fwd), next1(fwd), main(bwd), prev2(bwd), next1(bwd),
                  state, state, full(conv_w), full(conv_b), wd_spec(0), wd_spec(1),
                  full(b_r), full(b_i), full(lam)],
        out_specs=[main(fwd), main(bwd), state, state],
        out_shape=[jax.ShapeDtypeStruct((ls, nb, r), F32), jax.ShapeDtypeStruct((ls, nb, r), F32),
                   jax.ShapeDtypeStruct((nb, r), F32), jax.ShapeDtypeStruct((nb, r), F32)],
        scratch_shapes=[pltpu.VMEM((tl, nb, r), F32)] * 4 + [pltpu.VMEM((nb, r), F32)] * 2,
        compiler_params=_cparams(("arbitrary",), VMEM_LIMIT_BYTES),
        name="rglru_scan",
    )(xb, xb, xb, xb, xb, xb, h0f, h0b, conv_w, conv_b, wd, wd, b_r, b_i, lam)


def _paired_block_diag(w_r, w_i):
    def pair(w):
        ndir = w.shape[0]
        w = w.reshape(ndir, LRU_BLOCKS // 2, 2, LRU_BLOCK_DIM, LRU_BLOCK_DIM)
        z = jnp.zeros_like(w[:, :, 0])
        top = jnp.concatenate([w[:, :, 0], z], axis=-1)
        bot = jnp.concatenate([z, w[:, :, 1]], axis=-1)
        return jnp.concatenate([top, bot], axis=-2)
    return (0.5 * jnp.concatenate([pair(w_r), pair(w_i)], axis=-1)).astype(BF16)


ROUTE_IDX_LANE = N_EXPERTS
ROUTE_W_LANE = N_EXPERTS + TOP_K
ROUTE_ROW_LANE = N_EXPERTS + 2 * TOP_K


def _split_bf16(a):
    hi = a.astype(BF16)
    return hi, (a - hi.astype(F32)).astype(BF16)


def _route(h, wr_ref, info_ref, infot_ref, cnt_ref):
    h_hi, h_lo = _split_bf16(h)
    both = jnp.dot(h_hi, wr_ref[...], preferred_element_type=F32)
    logits = (both[:, :LANES] + both[:, LANES:]
              + jnp.dot(h_lo, wr_ref[:, :LANES], preferred_element_type=F32))
    lane = lax.broadcasted_iota(jnp.int32, logits.shape, 1)
    logits = jnp.where(lane < N_EXPERTS, logits, -jnp.inf)
    m1 = jnp.max(logits, axis=-1, keepdims=True)
    i1 = jnp.min(jnp.where(logits == m1, lane, LANES), axis=-1, keepdims=True)
    rest = jnp.where(lane == i1, -jnp.inf, logits)
    m2 = jnp.max(rest, axis=-1, keepdims=True)
    i2 = jnp.min(jnp.where(rest == m2, lane, LANES), axis=-1, keepdims=True)
    e2 = jnp.exp(m2 - m1)
    w1 = 1.0 / (1.0 + e2)
    w2 = e2 / (1.0 + e2)
    chosen = jnp.where((lane == i1) | (lane == i2), 1.0, 0.0)
    tm = h.shape[0]
    earlier = (lax.broadcasted_iota(jnp.int32, (tm, tm), 0)
               > lax.broadcasted_iota(jnp.int32, (tm, tm), 1))
    rank = jnp.dot(jnp.where(earlier, 1.0, 0.0).astype(BF16), chosen.astype(BF16),
                   preferred_element_type=F32)
    counts = jnp.sum(chosen, axis=0, keepdims=True)
    padded = jnp.ceil(counts * (1.0 / SEG_ALIGN)) * SEG_ALIGN
    before = (lax.broadcasted_iota(jnp.int32, (LANES, LANES), 0)
              < lax.broadcasted_iota(jnp.int32, (LANES, LANES), 1))
    start = jnp.dot(jnp.broadcast_to(padded, (SUBLANES, LANES)).astype(BF16),
                    jnp.where(before, 1.0, 0.0).astype(BF16), preferred_element_type=F32)[0:1]
    row = rank + start
    d1 = jnp.sum(jnp.where(lane == i1, row, 0.0), axis=-1, keepdims=True)
    d2 = jnp.sum(jnp.where(lane == i2, row, 0.0), axis=-1, keepdims=True)
    info = jnp.where(lane == i1, w1, jnp.where(lane == i2, w2, 0.0))
    for k, val in enumerate((i1.astype(F32), i2.astype(F32), w1, w2, d1, d2)):
        info = jnp.where(lane == ROUTE_IDX_LANE + k, val, info)
    info_ref[...] = info
    infot_ref[...] = info.T
    cnt_ref[...] = counts


def _outproj_kernel(gate_ref, *refs):
    state_refs, refs = refs[:2 * LRU_GROUPS], refs[2 * LRU_GROUPS:]
    (w_ref, x_ref, mod_ref, g_ref, b_ref, wr_ref,
     out_ref, h_ref, info_ref, infot_ref, cnt_ref) = refs
    d = x_ref.shape[-1]
    gt = gate_ref[...]
    gelu = 0.5 * gt * (1.0 + jnp.tanh(0.7978845608028654 * (gt + 0.044715 * (gt * gt * gt))))
    states = jnp.concatenate([state_refs[g][...] + state_refs[LRU_GROUPS + g][...]
                              for g in range(LRU_GROUPS)], axis=1)
    y = gelu * states
    proj = jnp.dot(y.astype(BF16), w_ref[...], preferred_element_type=F32)
    xn = _layer_norm(ALPHA * x_ref[...] + _mod_slice(mod_ref, 2, d) * proj, g_ref[...], b_ref[...])
    out_ref[...] = xn
    h = xn * (1.0 + _mod_slice(mod_ref, 4, d)) + _mod_slice(mod_ref, 3, d)
    h_ref[...] = h.astype(BF16)
    _route(h, wr_ref, info_ref, infot_ref, cnt_ref)


def _outproj_call(gate, hf, hb, w_out, x, mod, ln_g, ln_b, router_padded, rows):
    b, l, d = x.shape
    r = gate.shape[-1]
    tm = rows
    nt = l // tm
    tmajor = pl.BlockSpec((tm, r // LRU_GROUPS), lambda i, j: (j, i))
    return pl.pallas_call(
        _outproj_kernel,
        grid=(b, nt),
        in_specs=[
            pl.BlockSpec((None, tm, r), lambda i, j: (i, j, 0)),
            *([tmajor] * (2 * LRU_GROUPS)),
            pl.BlockSpec(w_out.shape, lambda i, j: (0, 0)),
            pl.BlockSpec((None, tm, d), lambda i, j: (i, j, 0)),
            pl.BlockSpec((None, 1, 6 * d), lambda i, j: (i, 0, 0)),
            pl.BlockSpec((1, d), lambda i, j: (0, 0)),
            pl.BlockSpec((1, d), lambda i, j: (0, 0)),
            pl.BlockSpec(router_padded.shape, lambda i, j: (0, 0)),
        ],
        out_specs=[
            pl.BlockSpec((None, tm, d), lambda i, j: (i, j, 0)),
            pl.BlockSpec((tm, d), lambda i, j: (i * nt + j, 0)),
            pl.BlockSpec((tm, LANES), lambda i, j: (i * nt + j, 0)),
            pl.BlockSpec((None, LANES, tm), lambda i, j: (i * nt + j, 0, 0)),
            pl.BlockSpec((None, 1, LANES), lambda i, j: (i * nt + j, 0, 0)),
        ],
        out_shape=[jax.ShapeDtypeStruct((b, l, d), F32), jax.ShapeDtypeStruct((b * l, d), BF16),
                   jax.ShapeDtypeStruct((b * l, LANES), F32),
                   jax.ShapeDtypeStruct((b * nt, LANES, tm), F32),
                   jax.ShapeDtypeStruct((b * nt, 1, LANES), F32)],
        compiler_params=_cparams(("arbitrary", "arbitrary"), VMEM_LIMIT_BYTES),
        name="lru_out_proj_route",
    )(gate, *hf, *hb, w_out, x, mod, ln_g, ln_b, router_padded)


SEG_ALIGN = SUBLANES
MAX_SEG_PAD = N_EXPERTS * (SEG_ALIGN - 1)
BF16_ROWS = 2 * SUBLANES


def _window_rows(window):
    return -(-(TOP_K * window + MAX_SEG_PAD) // BF16_ROWS) * BF16_ROWS


def _for_each_piece(seg_refs, w, window, fn):
    loff_ref, goff_ref, cpad_ref = seg_refs
    for e in range(N_EXPERTS):
        lo, go, c = (ref[w * N_EXPERTS + e] for ref in (loff_ref, goff_ref, cpad_ref))
        size = window
        while size >= SEG_ALIGN:
            done = c & ~(2 * size - 1)

            @pl.when((c & size) != 0)
            def _(lo=lo, go=go, done=done, size=size):
                fn(pl.multiple_of(lo + done, SEG_ALIGN), pl.multiple_of(go + done, SEG_ALIGN), size)

            size //= 2


def _for_each_tail_piece(tail_ref, len_ref, max_size, fn):
    for e in range(N_EXPERTS):
        go, c = tail_ref[e], len_ref[e]
        size = max_size
        while size >= SEG_ALIGN:
            done = c & ~(2 * size - 1)

            @pl.when((c & size) != 0)
            def _(go=go, done=done, size=size):
                fn(pl.multiple_of(go + done, SEG_ALIGN), size)

            size //= 2


def _dispatch_kernel(loff_ref, goff_ref, cpad_ref, tail_ref, tlen_ref, rest_ref, infot_ref,
                     h_ref, xs_hbm, buf, zbuf, sems):
    w, nw = pl.program_id(0), pl.num_programs(0)
    slot = w % 2
    window, d = h_ref.shape
    r = buf.shape[1]
    zrows = zbuf.shape[0]
    segs = (loff_ref, goff_ref, cpad_ref)

    def copy(buf_slot, lrow, grow, size):
        return pltpu.make_async_copy(buf.at[buf_slot, pl.ds(lrow, size)],
                                     xs_hbm.at[pl.ds(grow, size)], sems.at[buf_slot])

    def fill(grow, size):
        return pltpu.make_async_copy(zbuf.at[pl.ds(0, size)], xs_hbm.at[pl.ds(grow, size)],
                                     sems.at[2])

    def for_each_fill(act):
        _for_each_tail_piece(tail_ref, tlen_ref, zrows, lambda g, s: act(fill(g, s)))

        def body(k, _):
            act(fill(pl.multiple_of(rest_ref[0] + k * zrows, SEG_ALIGN), zrows))
            return 0
        lax.fori_loop(0, (xs_hbm.shape[0] - rest_ref[0]) // zrows, body, 0)

    @pl.when(w == 0)
    def _():
        zbuf[...] = jnp.zeros_like(zbuf)
        for_each_fill(lambda cp: cp.start())

    row = lax.broadcasted_iota(jnp.int32, (r, window), 0)
    dest = infot_ref[ROUTE_ROW_LANE:ROUTE_ROW_LANE + TOP_K, :].astype(jnp.int32)
    wts = infot_ref[ROUTE_W_LANE:ROUTE_W_LANE + TOP_K, :]
    hit0 = row == dest[0:1, :]
    hit1 = row == dest[1:2, :]
    perm = jnp.where(hit0 | hit1, 1.0, 0.0).astype(BF16)
    buf[slot, :, :d] = jnp.dot(perm, h_ref[...], preferred_element_type=F32)
    roww = jnp.sum(jnp.where(hit0, wts[0:1, :], 0.0) + jnp.where(hit1, wts[1:2, :], 0.0),
                   axis=1, keepdims=True)
    buf[slot, :, d:] = jnp.broadcast_to(roww, (r, LANES))

    _for_each_piece(segs, w, window, lambda l, g, s: copy(slot, l, g, s).start())

    @pl.when(w > 0)
    def _():
        _for_each_piece(segs, w - 1, window, lambda l, g, s: copy(1 - slot, l, g, s).wait())

    @pl.when(w == nw - 1)
    def _():
        _for_each_piece(segs, w, window, lambda l, g, s: copy(slot, l, g, s).wait())
        for_each_fill(lambda cp: cp.wait())


def _dispatch_call(plan, h, window, rows, n_rows):
    n, d = h.shape
    nw = n // window
    r = _window_rows(window)
    width = d + LANES
    grid_spec = pltpu.PrefetchScalarGridSpec(
        num_scalar_prefetch=6,
        grid=(nw,),
        in_specs=[
            pl.BlockSpec((None, LANES, window), lambda w, *_: (w, 0, 0)),
            pl.BlockSpec((window, d), lambda w, *_: (w, 0)),
        ],
        out_specs=pl.BlockSpec(memory_space=pl.ANY),
        scratch_shapes=[pltpu.VMEM((2, r, width), F32), pltpu.VMEM((rows // 2, width), F32),
                        pltpu.SemaphoreType.DMA((3,))],
    )
    return pl.pallas_call(
        _dispatch_kernel,
        grid_spec=grid_spec,
        out_shape=jax.ShapeDtypeStruct((n_rows, width), F32),
        compiler_params=_cparams(("arbitrary",), VMEM_LIMIT_BYTES),
        name="expert_dispatch",
    )(plan["loff"], plan["goff"], plan["cpad"], plan["tail"], plan["tail_len"], plan["rest"],
      plan["info_t"], h)


def _expert_kernel(te_ref, nv_ref, xs_ref, wg_ref, wu_ref, wd_ref, y_ref, hb_scr, acc_scr, w_scr):
    i, f = pl.program_id(0), pl.program_id(1)
    nv = nv_ref[0]
    d = y_ref.shape[-1]

    def chunk(hb):
        return _swiglu_chunk(hb, wg_ref[...].astype(BF16), wu_ref[...].astype(BF16),
                             wd_ref[...].astype(BF16))

    def finish_previous():
        y_ref[...] = acc_scr[...] * w_scr[:, 0:1]

    def start_current():
        hb = xs_ref[:, :d].astype(BF16)
        hb_scr[...] = hb
        w_scr[...] = xs_ref[:, d:]
        acc_scr[...] = chunk(hb)

    @pl.when((f == 0) & (i == 0))
    def _():
        start_current()

    @pl.when((f == 0) & (i > 0) & (i < nv))
    def _():
        finish_previous()
        start_current()

    @pl.when((f == 0) & (i > 0) & (i == nv))
    def _():
        finish_previous()

    @pl.when((f == 0) & (i > nv))
    def _():
        y_ref[...] = jnp.zeros_like(y_ref)

    @pl.when((f > 0) & (i < nv))
    def _():
        acc_scr[...] += chunk(hb_scr[...])


def _expert_call(plan, xs, w_gu, w_dn, rows, cols):
    n_rows, width = xs.shape
    d = width - LANES
    nt = n_rows // rows
    ff = w_dn.shape[1]
    fc = cols
    nf = ff // fc
    tile = lambda i, nv: jnp.minimum(i, nv[0] - 1)
    chunk = lambda i, f, nv: jnp.where(i < nv[0], f, nf - 1)
    grid_spec = pltpu.PrefetchScalarGridSpec(
        num_scalar_prefetch=2,
        grid=(nt + 1, nf),
        in_specs=[
            pl.BlockSpec((rows, width), lambda i, f, te, nv: (tile(i, nv), 0)),
            pl.BlockSpec((None, d, fc), lambda i, f, te, nv: (te[tile(i, nv)], 0, chunk(i, f, nv))),
            pl.BlockSpec((None, d, fc),
                         lambda i, f, te, nv: (te[tile(i, nv)], 0, nf + chunk(i, f, nv))),
            pl.BlockSpec((None, fc, d), lambda i, f, te, nv: (te[tile(i, nv)], chunk(i, f, nv), 0)),
        ],
        out_specs=pl.BlockSpec((rows, d), lambda i, f, te, nv: (jnp.maximum(i - 1, 0), 0)),
        scratch_shapes=[pltpu.VMEM((rows, d), BF16), pltpu.VMEM((rows, d), F32),
                        pltpu.VMEM((rows, LANES), F32)],
    )
    return pl.pallas_call(
        _expert_kernel,
        grid_spec=grid_spec,
        out_shape=jax.ShapeDtypeStruct((n_rows, d), F32),
        compiler_params=_cparams(("arbitrary", "arbitrary"), VMEM_LIMIT_BYTES),
        name="expert_swiglu",
    )(plan["tile_expert"], plan["n_valid"], xs, w_gu, w_gu, w_dn)


def _combine_kernel(loff_ref, goff_ref, cpad_ref, info_ref, y_hbm, x_ref, mod_ref, g_ref, b_ref,
                    out_ref, ybuf, sems):
    w, nw = pl.program_id(0), pl.num_programs(0)
    slot = w % 2
    window, d = x_ref.shape
    r = ybuf.shape[1]
    segs = (loff_ref, goff_ref, cpad_ref)

    def copy(buf_slot, lrow, grow, size):
        return pltpu.make_async_copy(y_hbm.at[pl.ds(grow, size)],
                                     ybuf.at[buf_slot, pl.ds(lrow, size)], sems.at[buf_slot])

    @pl.when(w == 0)
    def _():
        ybuf[...] = jnp.zeros_like(ybuf)
        _for_each_piece(segs, 0, window, lambda l, g, s: copy(0, l, g, s).start())

    @pl.when(w + 1 < nw)
    def _():
        _for_each_piece(segs, w + 1, window, lambda l, g, s: copy(1 - slot, l, g, s).start())

    _for_each_piece(segs, w, window, lambda l, g, s: copy(slot, l, g, s).wait())

    lane = lax.broadcasted_iota(jnp.int32, (window, r), 1)
    src = info_ref[:, ROUTE_ROW_LANE:ROUTE_ROW_LANE + TOP_K].astype(jnp.int32)
    hit = (lane == src[:, 0:1]) | (lane == src[:, 1:2])
    perm = jnp.where(hit, 1.0, 0.0).astype(BF16)
    hi, lo = _split_bf16(ybuf[slot])
    f = (jnp.dot(perm, hi, preferred_element_type=F32)
         + jnp.dot(perm, lo, preferred_element_type=F32))
    out_ref[...] = _layer_norm(ALPHA * x_ref[...] + _mod_slice(mod_ref, 5, d) * f,
                               g_ref[...], b_ref[...])


def _combine_call(plan, y, x, mod, ln_g, ln_b, window):
    b, l, d = x.shape
    wpb = l // window
    r = _window_rows(window)
    grid_spec = pltpu.PrefetchScalarGridSpec(
        num_scalar_prefetch=3,
        grid=(b * wpb,),
        in_specs=[
            pl.BlockSpec((window, LANES), lambda w, *_: (w, 0)),
            pl.BlockSpec(memory_space=pl.ANY),
            pl.BlockSpec((None, window, d), lambda w, *_: (w // wpb, w % wpb, 0)),
            pl.BlockSpec((None, 1, 6 * d), lambda w, *_: (w // wpb, 0, 0)),
            pl.BlockSpec((1, d), lambda w, *_: (0, 0)),
            pl.BlockSpec((1, d), lambda w, *_: (0, 0)),
        ],
        out_specs=pl.BlockSpec((None, window, d), lambda w, *_: (w // wpb, w % wpb, 0)),
        scratch_shapes=[pltpu.VMEM((2, r, d), F32), pltpu.SemaphoreType.DMA((2,))],
    )
    return pl.pallas_call(
        _combine_kernel,
        grid_spec=grid_spec,
        out_shape=jax.ShapeDtypeStruct((b, l, d), F32),
        compiler_params=_cparams(("arbitrary",), VMEM_LIMIT_BYTES),
        name="expert_combine",
    )(plan["loff"], plan["goff"], plan["cpad"], plan["info"], y, x, mod, ln_g, ln_b)


def _dispatch_plan(info, info_t, counts, window, rows):
    n = info.shape[0]
    nw = n // window
    count = counts[:, 0, :N_EXPERTS].astype(jnp.int32)
    cpad = -(-count // SEG_ALIGN) * SEG_ALIGN
    loff = jnp.cumsum(cpad, axis=1) - cpad
    group = -(-jnp.sum(cpad, axis=0) // rows) * rows
    gend = jnp.cumsum(group)
    goff = (gend - group)[None, :] + jnp.cumsum(cpad, axis=0) - cpad
    n_tiles = -(-(TOP_K * n + nw * MAX_SEG_PAD) // rows) + N_EXPERTS
    tile_start = jnp.arange(n_tiles) * rows
    tile_expert = jnp.minimum(jnp.sum(tile_start[:, None] >= gend[None, :], axis=1), N_EXPERTS - 1)
    flat = lambda a: a.reshape(-1).astype(jnp.int32)
    return dict(
        loff=flat(loff), goff=flat(goff), cpad=flat(cpad),
        info=info, info_t=info_t,
        tail=flat(goff[-1] + cpad[-1]), tail_len=flat(gend - goff[-1] - cpad[-1]),
        rest=flat(gend[-1:]),
        tile_expert=tile_expert.astype(jnp.int32),
        n_valid=(gend[-1] // rows).reshape(1).astype(jnp.int32),
        n_rows=n_tiles * rows,
    )


def _attention_layer(x, ctx, mod_x, mod_c, ln_g, ln_b, w_qkv, w_o, sink, w_gu, w_dn, grid_w, ts):
    b, l, d = x.shape
    lc = ctx.shape[1]
    g0, b0, g1, b1 = ln_g[0:1], ln_b[0:1], ln_g[1:2], ln_b[1:2]
    w_qkv, w_o, w_gu, w_dn = (w.astype(BF16) for w in (w_qkv, w_o, w_gu, w_dn))
    ctx_flat = ctx.reshape(1, b * lc, d)
    n_kv = N_KV_HEADS * HEAD_DIM

    q, k, v = _qkv_call(x, mod_x, w_qkv, _rope_tables(l, grid_w), ts["qkv_rows"])
    qc, kc, vc = _qkv_call(ctx_flat, mod_c, w_qkv, _no_rope_tables(b * lc), ts["qkv_rows"])
    qc = qc.reshape(b, lc, -1)
    kc, vc = kc.reshape(b, lc, n_kv), vc.reshape(b, lc, n_kv)
    mod_cb = jnp.broadcast_to(mod_c, (b,) + mod_c.shape[1:])

    x = _attn_call(q, k, v, kc, vc, sink, w_o, x, mod_x, g0, b0, band=True)
    ctx = _attn_call(qc, None, None, kc, vc, sink, w_o, ctx, mod_cb, g0, b0, band=False)

    x = _ffn_call(x, mod_x, w_gu, w_dn, g1, b1, ts["ffn_rows"], ts["ffn_cols"])
    ctx = _ffn_call(ctx.reshape(1, b * lc, d), mod_c, w_gu, w_dn, g1, b1,
                    ts["ffn_rows"], ts["ffn_cols"]).reshape(b, lc, d)
    return x, ctx


def _recurrent_layer(x, ctx, mod_x, mod_c, ln_g, ln_b, w_in, conv_w, conv_b, lam, w_r, b_r, w_i,
                     b_i, w_out, router, w_gu, w_dn, ts):
    b, l, d = x.shape
    g0, b0, g1, b1 = ln_g[0:1], ln_b[0:1], ln_g[1:2], ln_b[1:2]
    w_in, w_out = w_in.astype(BF16), w_out.astype(BF16)
    wd = _paired_block_diag(w_r, w_i)
    conv_b = conv_b.reshape(1, -1)
    mod_cb = jnp.broadcast_to(mod_c, (b,) + mod_c.shape[1:])

    r = w_out.shape[0]
    rg = r // LRU_GROUPS
    gate, *xb = _inproj_call(x, mod_x, w_in, ts["proj_rows"])
    _, *xb_c = _inproj_call(ctx, mod_cb, w_in, ts["proj_rows"])
    zero = jnp.zeros((b, rg), F32)
    hf, hb = [], []
    for g in range(LRU_GROUPS):
        cs = slice(g * rg, (g + 1) * rg)
        pairs = slice(g * rg // (2 * LRU_BLOCK_DIM), (g + 1) * rg // (2 * LRU_BLOCK_DIM))
        scan = functools.partial(_scan_call, conv_w=conv_w[:, cs], conv_b=conv_b[:, cs],
                                 wd=wd[:, pairs], b_r=0.5 * b_r[:, cs], b_i=0.5 * b_i[:, cs],
                                 lam=lam[:, cs], steps=ts["scan_steps"])
        _, _, cf, cb = scan(xb_c[g].reshape(-1, b, rg), zero, zero)
        hf_g, hb_g, _, _ = scan(xb[g].reshape(l, b, rg), cf, cb)
        hf.append(hf_g.reshape(l, b * rg))
        hb.append(hb_g.reshape(l, b * rg))
    router_padded = jnp.concatenate(
        _split_bf16(jnp.pad(router, ((0, 0), (0, LANES - N_EXPERTS)))), axis=1)
    x, h, info, info_t, counts = _outproj_call(gate, hf, hb, w_out, x, mod_x, g0, b0,
                                               router_padded, ts["moe_window"])
    plan = _dispatch_plan(info, info_t, counts, ts["moe_window"], ts["moe_rows"])
    xs = _dispatch_call(plan, h, ts["moe_window"], ts["moe_rows"], plan["n_rows"])
    y = _expert_call(plan, xs, w_gu, w_dn.astype(BF16), ts["moe_rows"], ts["ffn_cols"])
    return _combine_call(plan, y, x, mod_x, g1, b1, ts["moe_window"])


def kernel(x, c, ctx, c_ctx, w_mod, b_mod, ln_g, ln_b, attn_w_qkv, attn_w_o, attn_sink, ffn_w_gu,
           ffn_w_dn, lru_w_in, lru_conv_w, lru_conv_b, lru_lambda, lru_w_r, lru_b_r, lru_w_i,
           lru_b_i, lru_w_out, moe_router, moe_w_gu, moe_w_dn):
    b, l, d = x.shape
    assert w_mod.shape[0] == DEPTH and l % BLOCK == 0 and ctx.shape[1] % BLOCK == 0
    grid_w = 64
    ts = _tile_sizes()

    rows = -(-(b + 1) // SUBLANES) * SUBLANES
    cond = jnp.zeros((rows, d), F32).at[:b].set(c).at[b].set(c_ctx)
    mod = _mod_call(cond, w_mod, b_mod[:, None, :])
    mod_x = lambda i: mod[i, :b, None, :]
    mod_c = lambda i: mod[i, b:b + 1, None, :]

    x, ctx = _attention_layer(x, ctx, mod_x(0), mod_c(0), ln_g[0], ln_b[0], attn_w_qkv[0],
                              attn_w_o[0], attn_sink[0], ffn_w_gu[0], ffn_w_dn[0], grid_w, ts)
    return _recurrent_layer(x, ctx, mod_x(1), mod_c(1), ln_g[1], ln_b[1], lru_w_in[0],
                            lru_conv_w[0], lru_conv_b[0], lru_lambda[0], lru_w_r[0], lru_b_r[0],
                            lru_w_i[0], lru_b_i[0], lru_w_out[0], moe_router[0], moe_w_gu[0],
                            moe_w_dn[0], ts)
```

```python
import functools

import jax
import jax.numpy as jnp
import numpy as np
from jax import lax
from jax.experimental import pallas as pl
from jax.experimental.pallas import tpu as pltpu

N_HEADS = 8
N_KV_HEADS = 2
GROUP = N_HEADS // N_KV_HEADS
HEAD_DIM = 128
BLOCK = 128
AXIS_DIM = HEAD_DIM // 2
ROPE_BASE = 10000.0
LRU_BLOCKS = 8
LRU_BLOCK_DIM = 128
LRU_C = 8.0
N_EXPERTS = 8
TOP_K = 2
DEPTH = 2
ALPHA = (2 * DEPTH) ** 0.25
LN_EPS = 1e-5
NEG_INF = -1e30

LANES = 128
SUBLANES = 8
VMEM_LIMIT_BYTES = 56 * 1024 * 1024

F32 = jnp.float32
BF16 = jnp.bfloat16


def _tile_sizes():
    return dict(
        qkv_rows=1024,
        ffn_rows=1024,
        ffn_cols=512,
        proj_rows=1024,
        scan_steps=128,
        moe_window=512,
        moe_rows=1024,
    )


def _cparams(sem, vmem=None):
    return pltpu.CompilerParams(dimension_semantics=sem, vmem_limit_bytes=vmem)


def _layer_norm(h, g, b):
    mu = jnp.mean(h, axis=-1, keepdims=True)
    d = h - mu
    var = jnp.mean(d * d, axis=-1, keepdims=True)
    return d * lax.rsqrt(var + LN_EPS) * g + b


def _mod_slice(mod_ref, k, d):
    return mod_ref[:, k * d:(k + 1) * d]


def _mod_kernel(c_ref, w_ref, b_ref, o_ref):
    c = c_ref[...]
    s = c * jax.nn.sigmoid(c)
    o_ref[...] = jnp.dot(s, w_ref[...], preferred_element_type=F32,
                         precision=lax.Precision.HIGHEST) + b_ref[...]


def _mod_call(cond, w_mod, b_mod):
    rows, d = cond.shape
    depth, _, n = w_mod.shape
    tn = 1536
    return pl.pallas_call(
        _mod_kernel,
        grid=(depth, n // tn),
        in_specs=[
            pl.BlockSpec((rows, d), lambda i, j: (0, 0)),
            pl.BlockSpec((None, d, tn), lambda i, j: (i, 0, j)),
            pl.BlockSpec((None, 1, tn), lambda i, j: (i, 0, j)),
        ],
        out_specs=pl.BlockSpec((None, rows, tn), lambda i, j: (i, 0, j)),
        out_shape=jax.ShapeDtypeStruct((depth, rows, n), F32),
        compiler_params=_cparams(("arbitrary", "arbitrary")),
        name="mod_vectors",
    )(cond, w_mod, b_mod)


def _qkv_kernel(x_ref, mod_ref, w_ref, cos_ref, sa_ref, sb_ref, q_ref, k_ref, v_ref):
    d = x_ref.shape[-1]
    h = x_ref[...] * (1.0 + _mod_slice(mod_ref, 1, d)) + _mod_slice(mod_ref, 0, d)
    qkv = jnp.dot(h.astype(BF16), w_ref[...], preferred_element_type=F32)
    cos, sa, sb = cos_ref[...], sa_ref[...], sb_ref[...]
    scale = HEAD_DIM ** -0.5
    for head in range(N_HEADS + N_KV_HEADS):
        u = qkv[:, head * HEAD_DIM:(head + 1) * HEAD_DIM]
        r = (u * cos + pltpu.roll(u, HEAD_DIM - AXIS_DIM // 2, axis=1) * sa
             + pltpu.roll(u, AXIS_DIM // 2, axis=1) * sb)
        if head < N_HEADS:
            q_ref[:, head * HEAD_DIM:(head + 1) * HEAD_DIM] = (r * scale).astype(BF16)
        else:
            kh = head - N_HEADS
            k_ref[:, kh * HEAD_DIM:(kh + 1) * HEAD_DIM] = r.astype(BF16)
    kv = N_KV_HEADS * HEAD_DIM
    v_ref[...] = qkv[:, N_HEADS * HEAD_DIM + kv:].astype(BF16)


def _qkv_call(x, mod, w_qkv, tabs, rows):
    b, l, d = x.shape
    n_q, n_kv = N_HEADS * HEAD_DIM, N_KV_HEADS * HEAD_DIM
    tm = min(rows, l)
    tab_spec = pl.BlockSpec((tm, HEAD_DIM), lambda i, j: (j, 0))
    return pl.pallas_call(
        _qkv_kernel,
        grid=(b, l // tm),
        in_specs=[
            pl.BlockSpec((None, tm, d), lambda i, j: (i, j, 0)),
            pl.BlockSpec((None, 1, 6 * d), lambda i, j: (i, 0, 0)),
            pl.BlockSpec(w_qkv.shape, lambda i, j: (0, 0)),
            tab_spec, tab_spec, tab_spec,
        ],
        out_specs=[
            pl.BlockSpec((None, tm, n_q), lambda i, j: (i, j, 0)),
            pl.BlockSpec((None, tm, n_kv), lambda i, j: (i, j, 0)),
            pl.BlockSpec((None, tm, n_kv), lambda i, j: (i, j, 0)),
        ],
        out_shape=[
            jax.ShapeDtypeStruct((b, l, n_q), BF16),
            jax.ShapeDtypeStruct((b, l, n_kv), BF16),
            jax.ShapeDtypeStruct((b, l, n_kv), BF16),
        ],
        compiler_params=_cparams(("arbitrary", "arbitrary"), VMEM_LIMIT_BYTES),
        name="qkv_rope",
    )(x, mod, w_qkv, *tabs)


def _rope_tables(l, grid_w):
    f32 = np.float32
    t = np.arange(l)
    pos = np.stack([(t // grid_w).astype(f32), (t % grid_w).astype(f32)], axis=1)
    freqs = f32(ROPE_BASE) ** (-np.arange(0, AXIS_DIM, 2, dtype=f32) / f32(AXIS_DIM))
    lane = np.arange(HEAD_DIM)
    ang = (pos[:, lane // AXIS_DIM] * freqs[lane % (AXIS_DIM // 2)][None, :]).astype(f32)
    first = (lane % AXIS_DIM) < AXIS_DIM // 2
    cos, sin = np.cos(ang).astype(f32), np.sin(ang).astype(f32)
    zero = f32(0.0)
    return cos, np.where(first, -sin, zero), np.where(first, zero, sin)


def _no_rope_tables(l):
    z = np.zeros((l, HEAD_DIM), np.float32)
    return np.ones((l, HEAD_DIM), np.float32), z, z


def _nt_dot(a, b):
    return lax.dot_general(a, b, (((1,), (1,)), ((), ())), preferred_element_type=F32)


def _attend(qs, segs, sink_col):
    k_all = jnp.concatenate([k for k, _, _ in segs], axis=0)
    v_all = jnp.concatenate([v for _, v, _ in segs], axis=0)
    v_ext = jnp.concatenate([v_all, jnp.ones_like(v_all)], axis=1)
    s_all = _nt_dot(qs, k_all)
    tiles, col = [], 0
    for k, _, mask in segs:
        for c in range(col, col + k.shape[0], LANES):
            s = s_all[:, c:c + LANES]
            tiles.append(s if mask is None else jnp.where(mask, s, NEG_INF))
        col += k.shape[0]
    m_tile = tiles[0]
    for s in tiles[1:]:
        m_tile = jnp.maximum(m_tile, s)
    m = jnp.maximum(sink_col, jnp.max(m_tile, axis=-1, keepdims=True))
    p_all = jnp.concatenate([jnp.exp(s - m).astype(BF16) for s in tiles], axis=1)
    res = jnp.dot(p_all, v_ext, preferred_element_type=F32)
    hd = v_all.shape[1]
    return res[:, :hd] / (res[:, hd:hd + 1] + jnp.exp(sink_col - m))


def _attn_epilogue(o_scr, wo_ref, x_ref, mod_ref, g_ref, b_ref, out_ref):
    d = x_ref.shape[-1]
    proj = jnp.dot(o_scr[...], wo_ref[...], preferred_element_type=F32)
    out_ref[...] = _layer_norm(ALPHA * x_ref[...] + _mod_slice(mod_ref, 2, d) * proj,
                               g_ref[...], b_ref[...])


def _sink_column(sink_ref, kvh):
    return jnp.concatenate(
        [jnp.full((BLOCK, 1), sink_ref[kvh * GROUP + g], F32) for g in range(GROUP)], axis=0)


def _stack_group(q_ref, kvh):
    return jnp.concatenate(
        [q_ref[:, (kvh * GROUP + g) * HEAD_DIM:(kvh * GROUP + g + 1) * HEAD_DIM]
         for g in range(GROUP)], axis=0)


def _unstack_group(o, o_scr, kvh):
    for g in range(GROUP):
        col = (kvh * GROUP + g) * HEAD_DIM
        o_scr[:, col:col + HEAD_DIM] = o[g * BLOCK:(g + 1) * BLOCK].astype(BF16)


BAND_QUERY_BLOCKS = 2


def _band_attn_kernel(sink_ref, q_ref, *refs):
    nk = BAND_QUERY_BLOCKS + 2
    k_refs, v_refs = refs[:nk], refs[nk:2 * nk]
    kx_ref, vx_ref, wo_ref, x_ref, mod_ref, g_ref, b_ref, out_ref, o_scr = refs[2 * nk:]
    n, nsteps = pl.program_id(1), pl.num_programs(1)
    row = lax.broadcasted_iota(jnp.int32, (GROUP * BLOCK, BLOCK), 0) % BLOCK
    col = lax.broadcasted_iota(jnp.int32, (GROUP * BLOCK, BLOCK), 1)
    before, after = col >= row, col <= row
    for h in range(BAND_QUERY_BLOCKS):
        mask_prev = before & (n > 0) if h == 0 else before
        mask_next = after & (n < nsteps - 1) if h == BAND_QUERY_BLOCKS - 1 else after
        rows = slice(h * BLOCK, (h + 1) * BLOCK)
        for kvh in range(N_KV_HEADS):
            hs = slice(kvh * HEAD_DIM, (kvh + 1) * HEAD_DIM)
            segs = [(k_refs[h][:, hs], v_refs[h][:, hs], mask_prev),
                    (k_refs[h + 1][:, hs], v_refs[h + 1][:, hs], None),
                    (k_refs[h + 2][:, hs], v_refs[h + 2][:, hs], mask_next),
                    (kx_ref[:, hs], vx_ref[:, hs], None)]
            o = _attend(_stack_group(q_ref.at[rows], kvh), segs, _sink_column(sink_ref, kvh))
            _unstack_group(o, o_scr.at[rows], kvh)
    _attn_epilogue(o_scr, wo_ref, x_ref, mod_ref, g_ref, b_ref, out_ref)


def _ctx_attn_kernel(sink_ref, q_ref, kx_ref, vx_ref, wo_ref, x_ref, mod_ref, g_ref, b_ref,
                     out_ref, o_scr):
    for kvh in range(N_KV_HEADS):
        hs = slice(kvh * HEAD_DIM, (kvh + 1) * HEAD_DIM)
        o = _attend(_stack_group(q_ref, kvh), [(kx_ref[:, hs], vx_ref[:, hs], None)],
                    _sink_column(sink_ref, kvh))
        _unstack_group(o, o_scr, kvh)
    _attn_epilogue(o_scr, wo_ref, x_ref, mod_ref, g_ref, b_ref, out_ref)


def _attn_call(q, k, v, kx, vx, sink, w_o, x, mod, ln_g, ln_b, band):
    b, l, d = x.shape
    lc = kx.shape[1]
    nb = l // BLOCK
    n_q, n_kv = N_HEADS * HEAD_DIM, N_KV_HEADS * HEAD_DIM
    qb = BAND_QUERY_BLOCKS if band else 1
    tq = qb * BLOCK
    common_in = [
        pl.BlockSpec((None, lc, n_kv), lambda i, j: (i, 0, 0)),
        pl.BlockSpec((None, lc, n_kv), lambda i, j: (i, 0, 0)),
        pl.BlockSpec(w_o.shape, lambda i, j: (0, 0)),
        pl.BlockSpec((None, tq, d), lambda i, j: (i, j, 0)),
        pl.BlockSpec((None, 1, 6 * d), lambda i, j: (i, 0, 0)),
        pl.BlockSpec((1, d), lambda i, j: (0, 0)),
        pl.BlockSpec((1, d), lambda i, j: (0, 0)),
    ]
    head_in = [pl.BlockSpec(memory_space=pltpu.SMEM),
               pl.BlockSpec((None, tq, n_q), lambda i, j: (i, j, 0))]
    if band:
        kv_blk = lambda s: pl.BlockSpec(
            (None, BLOCK, n_kv), lambda i, j: (i, jnp.clip(qb * j + s, 0, nb - 1), 0))
        window = [kv_blk(s) for s in range(-1, qb + 1)]
        in_specs = head_in + window * 2 + common_in
        args = (sink, q) + (k,) * (qb + 2) + (v,) * (qb + 2) + (kx, vx, w_o, x, mod, ln_g, ln_b)
        body, name = _band_attn_kernel, "band_attention"
    else:
        in_specs = head_in + common_in
        args = (sink, q, kx, vx, w_o, x, mod, ln_g, ln_b)
        body, name = _ctx_attn_kernel, "context_attention"
    return pl.pallas_call(
        body,
        grid=(b, nb // qb),
        in_specs=in_specs,
        out_specs=pl.BlockSpec((None, tq, d), lambda i, j: (i, j, 0)),
        out_shape=jax.ShapeDtypeStruct((b, l, d), F32),
        scratch_shapes=[pltpu.VMEM((tq, n_q), BF16)],
        compiler_params=_cparams(("arbitrary", "arbitrary"), VMEM_LIMIT_BYTES),
        name=name,
    )(*args)


def _swiglu_chunk(hb, wg, wu, wd):
    g = jnp.dot(hb, wg, preferred_element_type=F32)
    u = jnp.dot(hb, wu, preferred_element_type=F32)
    a = (g * jax.nn.sigmoid(g)) * u
    return jnp.dot(a.astype(BF16), wd, preferred_element_type=F32)


def _ffn_kernel(xc_ref, xp_ref, modc_ref, modp_ref, wg_ref, wu_ref, wd_ref, g_ref, b_ref, out_ref,
                hb_scr, acc_scr):
    i, f = pl.program_id(0), pl.program_id(1)
    nt = pl.num_programs(0) - 1
    d = xc_ref.shape[-1]

    def finish_previous():
        out_ref[...] = _layer_norm(
            ALPHA * xp_ref[...] + _mod_slice(modp_ref, 5, d) * acc_scr[...], g_ref[...], b_ref[...])

    def start_current():
        h = xc_ref[...] * (1.0 + _mod_slice(modc_ref, 4, d)) + _mod_slice(modc_ref, 3, d)
        hb = h.astype(BF16)
        hb_scr[...] = hb
        acc_scr[...] = _swiglu_chunk(hb, wg_ref[...], wu_ref[...], wd_ref[...])

    @pl.when((f == 0) & (i == 0))
    def _():
        start_current()

    @pl.when((f == 0) & (i > 0) & (i < nt))
    def _():
        finish_previous()
        start_current()

    @pl.when((f == 0) & (i == nt))
    def _():
        finish_previous()

    @pl.when((f > 0) & (i < nt))
    def _():
        acc_scr[...] += _swiglu_chunk(hb_scr[...], wg_ref[...], wu_ref[...], wd_ref[...])


def _ffn_call(x, mod, w_gu, w_dn, ln_g, ln_b, rows, cols):
    b, l, d = x.shape
    ff = w_dn.shape[0]
    tm, fc = min(rows, l), cols
    nf = ff // fc
    lt = l // tm
    nt = b * lt
    cur = lambda i: jnp.minimum(i, nt - 1)
    prev = lambda i: jnp.maximum(i - 1, 0)
    chunk = lambda i, f: jnp.where(i < nt, f, nf - 1)
    x_spec = lambda t: pl.BlockSpec((None, tm, d), lambda i, f: (t(i) // lt, t(i) % lt, 0))
    mod_spec = lambda t: pl.BlockSpec((None, 1, 6 * d), lambda i, f: (t(i) // lt, 0, 0))
    return pl.pallas_call(
        _ffn_kernel,
        grid=(nt + 1, nf),
        in_specs=[
            x_spec(cur), x_spec(prev), mod_spec(cur), mod_spec(prev),
            pl.BlockSpec((d, fc), lambda i, f: (0, chunk(i, f))),
            pl.BlockSpec((d, fc), lambda i, f: (0, nf + chunk(i, f))),
            pl.BlockSpec((fc, d), lambda i, f: (chunk(i, f), 0)),
            pl.BlockSpec((1, d), lambda i, f: (0, 0)),
            pl.BlockSpec((1, d), lambda i, f: (0, 0)),
        ],
        out_specs=x_spec(prev),
        out_shape=jax.ShapeDtypeStruct((b, l, d), F32),
        scratch_shapes=[pltpu.VMEM((tm, d), BF16), pltpu.VMEM((tm, d), F32)],
        compiler_params=_cparams(("arbitrary", "arbitrary"), VMEM_LIMIT_BYTES),
        name="dense_swiglu",
    )(x, x, mod, mod, w_gu, w_gu, w_dn, ln_g, ln_b)


LRU_GROUPS = 2
SCAN_UNROLL = 8


def _inproj_kernel(x_ref, mod_ref, w_ref, gate_ref, *xb_refs):
    d = x_ref.shape[-1]
    r = gate_ref.shape[-1]
    rg = r // len(xb_refs)
    h = x_ref[...] * (1.0 + _mod_slice(mod_ref, 1, d)) + _mod_slice(mod_ref, 0, d)
    y = jnp.dot(h.astype(BF16), w_ref[...], preferred_element_type=F32)
    gate_ref[...] = y[:, :r]
    for g, xb_ref in enumerate(xb_refs):
        xb_ref[...] = y[:, r + g * rg:r + (g + 1) * rg]


def _inproj_call(x, mod, w_in, rows):
    b, l, d = x.shape
    r = w_in.shape[1] // 2
    rg = r // LRU_GROUPS
    tm = min(rows, l)
    return pl.pallas_call(
        _inproj_kernel,
        grid=(b, l // tm),
        in_specs=[
            pl.BlockSpec((None, tm, d), lambda i, j: (i, j, 0)),
            pl.BlockSpec((None, 1, 6 * d), lambda i, j: (i, 0, 0)),
            pl.BlockSpec(w_in.shape, lambda i, j: (0, 0)),
        ],
        out_specs=[pl.BlockSpec((None, tm, r), lambda i, j: (i, j, 0))]
        + [pl.BlockSpec((tm, rg), lambda i, j: (j, i))] * LRU_GROUPS,
        out_shape=[jax.ShapeDtypeStruct((b, l, r), F32)]
        + [jax.ShapeDtypeStruct((l, b * rg), F32)] * LRU_GROUPS,
        compiler_params=_cparams(("arbitrary", "arbitrary"), VMEM_LIMIT_BYTES),
        name="lru_in_proj",
    )(x, mod, w_in)


def _lru_coeffs(main_ref, prev_ref, next_ref, first, last, cw_ref, cb_ref, wd_ref, br_ref, bi_ref,
                lam_ref, a_scr, b_scr):
    tl, nb, r = main_ref.shape
    prev = jnp.where(first, 0.0, prev_ref[...])
    nxt = jnp.where(last, 0.0, next_ref[...])
    ext = jnp.concatenate([prev, main_ref[...], nxt], axis=0)
    u = cb_ref[...][None]
    for k in range(4):
        u = u + cw_ref[k:k + 1, :][None] * ext[k:k + tl]
    u = u.reshape(tl * nb, r)
    ub = u.astype(BF16)
    lam = lam_ref[...]
    softplus_neg = jnp.maximum(-lam, 0.0) + jnp.log1p(jnp.exp(-jnp.abs(lam)))
    half_rate = (-0.5 * LRU_C) * softplus_neg
    pw = 2 * LRU_BLOCK_DIM
    for p in range(r // pw):
        cs = slice(p * pw, (p + 1) * pw)
        z = jnp.dot(ub[:, cs], wd_ref[p], preferred_element_type=F32)
        tr = jnp.tanh(z[:, :pw] + br_ref[:, cs])
        ig = 0.5 + 0.5 * jnp.tanh(z[:, pw:] + bi_ref[:, cs])
        log_a = half_rate[:, cs] * tr + half_rate[:, cs]
        a = jnp.exp(log_a)
        gain = jnp.sqrt(-jnp.tanh(log_a) * (a * a + 1.0))
        a_scr[:, :, cs] = a.reshape(tl, nb, pw)
        b_scr[:, :, cs] = (gain * (ig * u[:, cs])).reshape(tl, nb, pw)


def _scan_kernel(fm_ref, fp_ref, fn_ref, bm_ref, bp_ref, bn_ref, h0f_ref, h0b_ref,
                 cw_ref, cb_ref, wdf_ref, wdb_ref, br_ref, bi_ref, lam_ref,
                 hf_ref, hb_ref, lastf_ref, lastb_ref,
                 af_scr, bf_scr, ab_scr, bb_scr, sf_scr, sb_scr):
    c, nc = pl.program_id(0), pl.num_programs(0)
    tl = fm_ref.shape[0]

    @pl.when(c == 0)
    def _():
        sf_scr[...] = h0f_ref[...]
        sb_scr[...] = h0b_ref[...]

    _lru_coeffs(fm_ref, fp_ref, fn_ref, c == 0, c == nc - 1, cw_ref, cb_ref, wdf_ref,
                br_ref.at[0:1], bi_ref.at[0:1], lam_ref.at[0:1], af_scr, bf_scr)
    _lru_coeffs(bm_ref, bp_ref, bn_ref, c == nc - 1, c == 0, cw_ref, cb_ref, wdb_ref,
                br_ref.at[1:2], bi_ref.at[1:2], lam_ref.at[1:2], ab_scr, bb_scr)

    def step(t, carry):
        hf, hb = carry
        hf = af_scr[t] * hf + bf_scr[t]
        hf_ref[t] = hf
        tb = tl - 1 - t
        hb = ab_scr[tb] * hb + bb_scr[tb]
        hb_ref[tb] = hb
        return hf, hb

    hf, hb = lax.fori_loop(0, tl, step, (sf_scr[...], sb_scr[...]), unroll=SCAN_UNROLL)
    sf_scr[...] = hf
    sb_scr[...] = hb

    @pl.when(c == nc - 1)
    def _():
        lastf_ref[...] = hf
        lastb_ref[...] = hb


def _scan_call(xb, h0f, h0b, conv_w, conv_b, wd, b_r, b_i, lam, steps):
    ls, nb, r = xb.shape
    tl = min(steps, ls)
    nc = ls // tl
    fwd, bwd = (lambda c: c), (lambda c: nc - 1 - c)
    main = lambda ch: pl.BlockSpec((tl, nb, r), lambda c: (ch(c), 0, 0))
    prev2 = lambda ch: pl.BlockSpec((2, nb, r), lambda c: (jnp.maximum(ch(c) * (tl // 2) - 1, 0), 0, 0))
    next1 = lambda ch: pl.BlockSpec((1, nb, r), lambda c: (jnp.minimum((ch(c) + 1) * tl, ls - 1), 0, 0))
    full = lambda a: pl.BlockSpec(a.shape, lambda c: (0,) * a.ndim)
    state = pl.BlockSpec((nb, r), lambda c: (0, 0))
    wd_spec = lambda dirn: pl.BlockSpec((None,) + wd.shape[1:], lambda c: (dirn, 0, 0, 0))
    return pl.pallas_call(
        _scan_kernel,
        grid=(nc,),
        in_specs=[main(fwd), prev2(fwd), next1(fwd), main(bwd), prev2(bwd), next1(bwd),
                  state, state, full(conv_w), full(conv_b), wd_spec(0), wd_spec(1),
                  full(b_r), full(b_i), full(lam)],
        out_specs=[main(fwd), main(bwd), state, state],
        out_shape=[jax.ShapeDtypeStruct((ls, nb, r), F32), jax.ShapeDtypeStruct((ls, nb, r), F32),
                   jax.ShapeDtypeStruct((nb, r), F32), jax.ShapeDtypeStruct((nb, r), F32)],
        scratch_shapes=[pltpu.VMEM((tl, nb, r), F32)] * 4 + [pltpu.VMEM((nb, r), F32)] * 2,
        compiler_params=_cparams(("arbitrary",), VMEM_LIMIT_BYTES),
        name="rglru_scan",
    )(xb, xb, xb, xb, xb, xb, h0f, h0b, conv_w, conv_b, wd, wd, b_r, b_i, lam)


def _paired_block_diag(w_r, w_i):
    def pair(w):
        ndir = w.shape[0]
        w = w.reshape(ndir, LRU_BLOCKS // 2, 2, LRU_BLOCK_DIM, LRU_BLOCK_DIM)
        z = jnp.zeros_like(w[:, :, 0])
        top = jnp.concatenate([w[:, :, 0], z], axis=-1)
        bot = jnp.concatenate([z, w[:, :, 1]], axis=-1)
        return jnp.concatenate([top, bot], axis=-2)
    return (0.5 * jnp.concatenate([pair(w_r), pair(w_i)], axis=-1)).astype(BF16)


ROUTE_IDX_LANE = N_EXPERTS
ROUTE_W_LANE = N_EXPERTS + TOP_K
ROUTE_ROW_LANE = N_EXPERTS + 2 * TOP_K


def _split_bf16(a):
    hi = a.astype(BF16)
    return hi, (a - hi.astype(F32)).astype(BF16)


def _route(h, wr_ref, info_ref, infot_ref, cnt_ref):
    h_hi, h_lo = _split_bf16(h)
    both = jnp.dot(h_hi, wr_ref[...], preferred_element_type=F32)
    logits = (both[:, :LANES] + both[:, LANES:]
              + jnp.dot(h_lo, wr_ref[:, :LANES], preferred_element_type=F32))
    lane = lax.broadcasted_iota(jnp.int32, logits.shape, 1)
    logits = jnp.where(lane < N_EXPERTS, logits, -jnp.inf)
    m1 = jnp.max(logits, axis=-1, keepdims=True)
    i1 = jnp.min(jnp.where(logits == m1, lane, LANES), axis=-1, keepdims=True)
    rest = jnp.where(lane == i1, -jnp.inf, logits)
    m2 = jnp.max(rest, axis=-1, keepdims=True)
    i2 = jnp.min(jnp.where(rest == m2, lane, LANES), axis=-1, keepdims=True)
    e2 = jnp.exp(m2 - m1)
    w1 = 1.0 / (1.0 + e2)
    w2 = e2 / (1.0 + e2)
    chosen = jnp.where((lane == i1) | (lane == i2), 1.0, 0.0)
    tm = h.shape[0]
    earlier = (lax.broadcasted_iota(jnp.int32, (tm, tm), 0)
               > lax.broadcasted_iota(jnp.int32, (tm, tm), 1))
    rank = jnp.dot(jnp.where(earlier, 1.0, 0.0).astype(BF16), chosen.astype(BF16),
                   preferred_element_type=F32)
    counts = jnp.sum(chosen, axis=0, keepdims=True)
    padded = jnp.ceil(counts * (1.0 / SEG_ALIGN)) * SEG_ALIGN
    before = (lax.broadcasted_iota(jnp.int32, (LANES, LANES), 0)
              < lax.broadcasted_iota(jnp.int32, (LANES, LANES), 1))
    start = jnp.dot(jnp.broadcast_to(padded, (SUBLANES, LANES)).astype(BF16),
                    jnp.where(before, 1.0, 0.0).astype(BF16), preferred_element_type=F32)[0:1]
    row = rank + start
    d1 = jnp.sum(jnp.where(lane == i1, row, 0.0), axis=-1, keepdims=True)
    d2 = jnp.sum(jnp.where(lane == i2, row, 0.0), axis=-1, keepdims=True)
    info = jnp.where(lane == i1, w1, jnp.where(lane == i2, w2, 0.0))
    for k, val in enumerate((i1.astype(F32), i2.astype(F32), w1, w2, d1, d2)):
        info = jnp.where(lane == ROUTE_IDX_LANE + k, val, info)
    info_ref[...] = info
    infot_ref[...] = info.T
    cnt_ref[...] = counts


def _outproj_kernel(gate_ref, *refs):
    state_refs, refs = refs[:2 * LRU_GROUPS], refs[2 * LRU_GROUPS:]
    (w_ref, x_ref, mod_ref, g_ref, b_ref, wr_ref,
     out_ref, h_ref, info_ref, infot_ref, cnt_ref) = refs
    d = x_ref.shape[-1]
    gt = gate_ref[...]
    gelu = 0.5 * gt * (1.0 + jnp.tanh(0.7978845608028654 * (gt + 0.044715 * (gt * gt * gt))))
    states = jnp.concatenate([state_refs[g][...] + state_refs[LRU_GROUPS + g][...]
                              for g in range(LRU_GROUPS)], axis=1)
    y = gelu * states
    proj = jnp.dot(y.astype(BF16), w_ref[...], preferred_element_type=F32)
    xn = _layer_norm(ALPHA * x_ref[...] + _mod_slice(mod_ref, 2, d) * proj, g_ref[...], b_ref[...])
    out_ref[...] = xn
    h = xn * (1.0 + _mod_slice(mod_ref, 4, d)) + _mod_slice(mod_ref, 3, d)
    h_ref[...] = h.astype(BF16)
    _route(h, wr_ref, info_ref, infot_ref, cnt_ref)


def _outproj_call(gate, hf, hb, w_out, x, mod, ln_g, ln_b, router_padded, rows):
    b, l, d = x.shape
    r = gate.shape[-1]
    tm = rows
    nt = l // tm
    tmajor = pl.BlockSpec((tm, r // LRU_GROUPS), lambda i, j: (j, i))
    return pl.pallas_call(
        _outproj_kernel,
        grid=(b, nt),
        in_specs=[
            pl.BlockSpec((None, tm, r), lambda i, j: (i, j, 0)),
            *([tmajor] * (2 * LRU_GROUPS)),
            pl.BlockSpec(w_out.shape, lambda i, j: (0, 0)),
            pl.BlockSpec((None, tm, d), lambda i, j: (i, j, 0)),
            pl.BlockSpec((None, 1, 6 * d), lambda i, j: (i, 0, 0)),
            pl.BlockSpec((1, d), lambda i, j: (0, 0)),
            pl.BlockSpec((1, d), lambda i, j: (0, 0)),
            pl.BlockSpec(router_padded.shape, lambda i, j: (0, 0)),
        ],
        out_specs=[
            pl.BlockSpec((None, tm, d), lambda i, j: (i, j, 0)),
            pl.BlockSpec((tm, d), lambda i, j: (i * nt + j, 0)),
            pl.BlockSpec((tm, LANES), lambda i, j: (i * nt + j, 0)),
            pl.BlockSpec((None, LANES, tm), lambda i, j: (i * nt + j, 0, 0)),
            pl.BlockSpec((None, 1, LANES), lambda i, j: (i * nt + j, 0, 0)),
        ],
        out_shape=[jax.ShapeDtypeStruct((b, l, d), F32), jax.ShapeDtypeStruct((b * l, d), BF16),
                   jax.ShapeDtypeStruct((b * l, LANES), F32),
                   jax.ShapeDtypeStruct((b * nt, LANES, tm), F32),
                   jax.ShapeDtypeStruct((b * nt, 1, LANES), F32)],
        compiler_params=_cparams(("arbitrary", "arbitrary"), VMEM_LIMIT_BYTES),
        name="lru_out_proj_route",
    )(gate, *hf, *hb, w_out, x, mod, ln_g, ln_b, router_padded)


SEG_ALIGN = SUBLANES
MAX_SEG_PAD = N_EXPERTS * (SEG_ALIGN - 1)
BF16_ROWS = 2 * SUBLANES


def _window_rows(window):
    return -(-(TOP_K * window + MAX_SEG_PAD) // BF16_ROWS) * BF16_ROWS


def _for_each_piece(seg_refs, w, window, fn):
    loff_ref, goff_ref, cpad_ref = seg_refs
    for e in range(N_EXPERTS):
        lo, go, c = (ref[w * N_EXPERTS + e] for ref in (loff_ref, goff_ref, cpad_ref))
        size = window
        while size >= SEG_ALIGN:
            done = c & ~(2 * size - 1)

            @pl.when((c & size) != 0)
            def _(lo=lo, go=go, done=done, size=size):
                fn(pl.multiple_of(lo + done, SEG_ALIGN), pl.multiple_of(go + done, SEG_ALIGN), size)

            size //= 2


def _for_each_tail_piece(tail_ref, len_ref, max_size, fn):
    for e in range(N_EXPERTS):
        go, c = tail_ref[e], len_ref[e]
        size = max_size
        while size >= SEG_ALIGN:
            done = c & ~(2 * size - 1)

            @pl.when((c & size) != 0)
            def _(go=go, done=done, size=size):
                fn(pl.multiple_of(go + done, SEG_ALIGN), size)

            size //= 2


def _dispatch_kernel(loff_ref, goff_ref, cpad_ref, tail_ref, tlen_ref, rest_ref, infot_ref,
                     h_ref, xs_hbm, buf, zbuf, sems):
    w, nw = pl.program_id(0), pl.num_programs(0)
    slot = w % 2
    window, d = h_ref.shape
    r = buf.shape[1]
    zrows = zbuf.shape[0]
    segs = (loff_ref, goff_ref, cpad_ref)

    def copy(buf_slot, lrow, grow, size):
        return pltpu.make_async_copy(buf.at[buf_slot, pl.ds(lrow, size)],
                                     xs_hbm.at[pl.ds(grow, size)], sems.at[buf_slot])

    def fill(grow, size):
        return pltpu.make_async_copy(zbuf.at[pl.ds(0, size)], xs_hbm.at[pl.ds(grow, size)],
                                     sems.at[2])

    def for_each_fill(act):
        _for_each_tail_piece(tail_ref, tlen_ref, zrows, lambda g, s: act(fill(g, s)))

        def body(k, _):
            act(fill(pl.multiple_of(rest_ref[0] + k * zrows, SEG_ALIGN), zrows))
            return 0
        lax.fori_loop(0, (xs_hbm.shape[0] - rest_ref[0]) // zrows, body, 0)

    @pl.when(w == 0)
    def _():
        zbuf[...] = jnp.zeros_like(zbuf)
        for_each_fill(lambda cp: cp.start())

    row = lax.broadcasted_iota(jnp.int32, (r, window), 0)
    dest = infot_ref[ROUTE_ROW_LANE:ROUTE_ROW_LANE + TOP_K, :].astype(jnp.int32)
    wts = infot_ref[ROUTE_W_LANE:ROUTE_W_LANE + TOP_K, :]
    hit0 = row == dest[0:1, :]
    hit1 = row == dest[1:2, :]
    perm = jnp.where(hit0 | hit1, 1.0, 0.0).astype(BF16)
    buf[slot, :, :d] = jnp.dot(perm, h_ref[...], preferred_element_type=F32)
    roww = jnp.sum(jnp.where(hit0, wts[0:1, :], 0.0) + jnp.where(hit1, wts[1:2, :], 0.0),
                   axis=1, keepdims=True)
    buf[slot, :, d:] = jnp.broadcast_to(roww, (r, LANES))

    _for_each_piece(segs, w, window, lambda l, g, s: copy(slot, l, g, s).start())

    @pl.when(w > 0)
    def _():
        _for_each_piece(segs, w - 1, window, lambda l, g, s: copy(1 - slot, l, g, s).wait())

    @pl.when(w == nw - 1)
    def _():
        _for_each_piece(segs, w, window, lambda l, g, s: copy(slot, l, g, s).wait())
        for_each_fill(lambda cp: cp.wait())


def _dispatch_call(plan, h, window, rows, n_rows):
    n, d = h.shape
    nw = n // window
    r = _window_rows(window)
    width = d + LANES
    grid_spec = pltpu.PrefetchScalarGridSpec(
        num_scalar_prefetch=6,
        grid=(nw,),
        in_specs=[
            pl.BlockSpec((None, LANES, window), lambda w, *_: (w, 0, 0)),
            pl.BlockSpec((window, d), lambda w, *_: (w, 0)),
        ],
        out_specs=pl.BlockSpec(memory_space=pl.ANY),
        scratch_shapes=[pltpu.VMEM((2, r, width), F32), pltpu.VMEM((rows // 2, width), F32),
                        pltpu.SemaphoreType.DMA((3,))],
    )
    return pl.pallas_call(
        _dispatch_kernel,
        grid_spec=grid_spec,
        out_shape=jax.ShapeDtypeStruct((n_rows, width), F32),
        compiler_params=_cparams(("arbitrary",), VMEM_LIMIT_BYTES),
        name="expert_dispatch",
    )(plan["loff"], plan["goff"], plan["cpad"], plan["tail"], plan["tail_len"], plan["rest"],
      plan["info_t"], h)


def _expert_kernel(te_ref, nv_ref, xs_ref, wg_ref, wu_ref, wd_ref, y_ref, hb_scr, acc_scr, w_scr):
    i, f = pl.program_id(0), pl.program_id(1)
    nv = nv_ref[0]
    d = y_ref.shape[-1]

    def chunk(hb):
        return _swiglu_chunk(hb, wg_ref[...].astype(BF16), wu_ref[...].astype(BF16),
                             wd_ref[...].astype(BF16))

    def finish_previous():
        y_ref[...] = acc_scr[...] * w_scr[:, 0:1]

    def start_current():
        hb = xs_ref[:, :d].astype(BF16)
        hb_scr[...] = hb
        w_scr[...] = xs_ref[:, d:]
        acc_scr[...] = chunk(hb)

    @pl.when((f == 0) & (i == 0))
    def _():
        start_current()

    @pl.when((f == 0) & (i > 0) & (i < nv))
    def _():
        finish_previous()
        start_current()

    @pl.when((f == 0) & (i > 0) & (i == nv))
    def _():
        finish_previous()

    @pl.when((f == 0) & (i > nv))
    def _():
        y_ref[...] = jnp.zeros_like(y_ref)

    @pl.when((f > 0) & (i < nv))
    def _():
        acc_scr[...] += chunk(hb_scr[...])


def _expert_call(plan, xs, w_gu, w_dn, rows, cols):
    n_rows, width = xs.shape
    d = width - LANES
    nt = n_rows // rows
    ff = w_dn.shape[1]
    fc = cols
    nf = ff // fc
    tile = lambda i, nv: jnp.minimum(i, nv[0] - 1)
    chunk = lambda i, f, nv: jnp.where(i < nv[0], f, nf - 1)
    grid_spec = pltpu.PrefetchScalarGridSpec(
        num_scalar_prefetch=2,
        grid=(nt + 1, nf),
        in_specs=[
            pl.BlockSpec((rows, width), lambda i, f, te, nv: (tile(i, nv), 0)),
            pl.BlockSpec((None, d, fc), lambda i, f, te, nv: (te[tile(i, nv)], 0, chunk(i, f, nv))),
            pl.BlockSpec((None, d, fc),
                         lambda i, f, te, nv: (te[tile(i, nv)], 0, nf + chunk(i, f, nv))),
            pl.BlockSpec((None, fc, d), lambda i, f, te, nv: (te[tile(i, nv)], chunk(i, f, nv), 0)),
        ],
        out_specs=pl.BlockSpec((rows, d), lambda i, f, te, nv: (jnp.maximum(i - 1, 0), 0)),
        scratch_shapes=[pltpu.VMEM((rows, d), BF16), pltpu.VMEM((rows, d), F32),
                        pltpu.VMEM((rows, LANES), F32)],
    )
    return pl.pallas_call(
        _expert_kernel,
        grid_spec=grid_spec,
        out_shape=jax.ShapeDtypeStruct((n_rows, d), F32),
        compiler_params=_cparams(("arbitrary", "arbitrary"), VMEM_LIMIT_BYTES),
        name="expert_swiglu",
    )(plan["tile_expert"], plan["n_valid"], xs, w_gu, w_gu, w_dn)


def _combine_kernel(loff_ref, goff_ref, cpad_ref, info_ref, y_hbm, x_ref, mod_ref, g_ref, b_ref,
                    out_ref, ybuf, sems):
    w, nw = pl.program_id(0), pl.num_programs(0)
    slot = w % 2
    window, d = x_ref.shape
    r = ybuf.shape[1]
    segs = (loff_ref, goff_ref, cpad_ref)

    def copy(buf_slot, lrow, grow, size):
        return pltpu.make_async_copy(y_hbm.at[pl.ds(grow, size)],
                                     ybuf.at[buf_slot, pl.ds(lrow, size)], sems.at[buf_slot])

    @pl.when(w == 0)
    def _():
        ybuf[...] = jnp.zeros_like(ybuf)
        _for_each_piece(segs, 0, window, lambda l, g, s: copy(0, l, g, s).start())

    @pl.when(w + 1 < nw)
    def _():
        _for_each_piece(segs, w + 1, window, lambda l, g, s: copy(1 - slot, l, g, s).start())

    _for_each_piece(segs, w, window, lambda l, g, s: copy(slot, l, g, s).wait())

    lane = lax.broadcasted_iota(jnp.int32, (window, r), 1)
    src = info_ref[:, ROUTE_ROW_LANE:ROUTE_ROW_LANE + TOP_K].astype(jnp.int32)
    hit = (lane == src[:, 0:1]) | (lane == src[:, 1:2])
    perm = jnp.where(hit, 1.0, 0.0).astype(BF16)
    hi, lo = _split_bf16(ybuf[slot])
    f = (jnp.dot(perm, hi, preferred_element_type=F32)
         + jnp.dot(perm, lo, preferred_element_type=F32))
    out_ref[...] = _layer_norm(ALPHA * x_ref[...] + _mod_slice(mod_ref, 5, d) * f,
                               g_ref[...], b_ref[...])


def _combine_call(plan, y, x, mod, ln_g, ln_b, window):
    b, l, d = x.shape
    wpb = l // window
    r = _window_rows(window)
    grid_spec = pltpu.PrefetchScalarGridSpec(
        num_scalar_prefetch=3,
        grid=(b * wpb,),
        in_specs=[
            pl.BlockSpec((window, LANES), lambda w, *_: (w, 0)),
            pl.BlockSpec(memory_space=pl.ANY),
            pl.BlockSpec((None, window, d), lambda w, *_: (w // wpb, w % wpb, 0)),
            pl.BlockSpec((None, 1, 6 * d), lambda w, *_: (w // wpb, 0, 0)),
            pl.BlockSpec((1, d), lambda w, *_: (0, 0)),
            pl.BlockSpec((1, d), lambda w, *_: (0, 0)),
        ],
        out_specs=pl.BlockSpec((None, window, d), lambda w, *_: (w // wpb, w % wpb, 0)),
        scratch_shapes=[pltpu.VMEM((2, r, d), F32), pltpu.SemaphoreType.DMA((2,))],
    )
    return pl.pallas_call(
        _combine_kernel,
        grid_spec=grid_spec,
        out_shape=jax.ShapeDtypeStruct((b, l, d), F32),
        compiler_params=_cparams(("arbitrary",), VMEM_LIMIT_BYTES),
        name="expert_combine",
    )(plan["loff"], plan["goff"], plan["cpad"], plan["info"], y, x, mod, ln_g, ln_b)


def _dispatch_plan(info, info_t, counts, window, rows):
    n = info.shape[0]
    nw = n // window
    count = counts[:, 0, :N_EXPERTS].astype(jnp.int32)
    cpad = -(-count // SEG_ALIGN) * SEG_ALIGN
    loff = jnp.cumsum(cpad, axis=1) - cpad
    group = -(-jnp.sum(cpad, axis=0) // rows) * rows
    gend = jnp.cumsum(group)
    goff = (gend - group)[None, :] + jnp.cumsum(cpad, axis=0) - cpad
    n_tiles = -(-(TOP_K * n + nw * MAX_SEG_PAD) // rows) + N_EXPERTS
    tile_start = jnp.arange(n_tiles) * rows
    tile_expert = jnp.minimum(jnp.sum(tile_start[:, None] >= gend[None, :], axis=1), N_EXPERTS - 1)
    flat = lambda a: a.reshape(-1).astype(jnp.int32)
    return dict(
        loff=flat(loff), goff=flat(goff), cpad=flat(cpad),
        info=info, info_t=info_t,
        tail=flat(goff[-1] + cpad[-1]), tail_len=flat(gend - goff[-1] - cpad[-1]),
        rest=flat(gend[-1:]),
        tile_expert=tile_expert.astype(jnp.int32),
        n_valid=(gend[-1] // rows).reshape(1).astype(jnp.int32),
        n_rows=n_tiles * rows,
    )


def _attention_layer(x, ctx, mod_x, mod_c, ln_g, ln_b, w_qkv, w_o, sink, w_gu, w_dn, grid_w, ts):
    b, l, d = x.shape
    lc = ctx.shape[1]
    g0, b0, g1, b1 = ln_g[0:1], ln_b[0:1], ln_g[1:2], ln_b[1:2]
    w_qkv, w_o, w_gu, w_dn = (w.astype(BF16) for w in (w_qkv, w_o, w_gu, w_dn))
    ctx_flat = ctx.reshape(1, b * lc, d)
    n_kv = N_KV_HEADS * HEAD_DIM

    q, k, v = _qkv_call(x, mod_x, w_qkv, _rope_tables(l, grid_w), ts["qkv_rows"])
    qc, kc, vc = _qkv_call(ctx_flat, mod_c, w_qkv, _no_rope_tables(b * lc), ts["qkv_rows"])
    qc = qc.reshape(b, lc, -1)
    kc, vc = kc.reshape(b, lc, n_kv), vc.reshape(b, lc, n_kv)
    mod_cb = jnp.broadcast_to(mod_c, (b,) + mod_c.shape[1:])

    x = _attn_call(q, k, v, kc, vc, sink, w_o, x, mod_x, g0, b0, band=True)
    ctx = _attn_call(qc, None, None, kc, vc, sink, w_o, ctx, mod_cb, g0, b0, band=False)

    x = _ffn_call(x, mod_x, w_gu, w_dn, g1, b1, ts["ffn_rows"], ts["ffn_cols"])
    ctx = _ffn_call(ctx.reshape(1, b * lc, d), mod_c, w_gu, w_dn, g1, b1,
                    ts["ffn_rows"], ts["ffn_cols"]).reshape(b, lc, d)
    return x, ctx


def _recurrent_layer(x, ctx, mod_x, mod_c, ln_g, ln_b, w_in, conv_w, conv_b, lam, w_r, b_r, w_i,
                     b_i, w_out, router, w_gu, w_dn, ts):
    b, l, d = x.shape
    g0, b0, g1, b1 = ln_g[0:1], ln_b[0:1], ln_g[1:2], ln_b[1:2]
    w_in, w_out = w_in.astype(BF16), w_out.astype(BF16)
    wd = _paired_block_diag(w_r, w_i)
    conv_b = conv_b.reshape(1, -1)
    mod_cb = jnp.broadcast_to(mod_c, (b,) + mod_c.shape[1:])

    r = w_out.shape[0]
    rg = r // LRU_GROUPS
    gate, *xb = _inproj_call(x, mod_x, w_in, ts["proj_rows"])
    _, *xb_c = _inproj_call(ctx, mod_cb, w_in, ts["proj_rows"])
    zero = jnp.zeros((b, rg), F32)
    hf, hb = [], []
    for g in range(LRU_GROUPS):
        cs = slice(g * rg, (g + 1) * rg)
        pairs = slice(g * rg // (2 * LRU_BLOCK_DIM), (g + 1) * rg // (2 * LRU_BLOCK_DIM))
        scan = functools.partial(_scan_call, conv_w=conv_w[:, cs], conv_b=conv_b[:, cs],
                                 wd=wd[:, pairs], b_r=0.5 * b_r[:, cs], b_i=0.5 * b_i[:, cs],
                                 lam=lam[:, cs], steps=ts["scan_steps"])
        _, _, cf, cb = scan(xb_c[g].reshape(-1, b, rg), zero, zero)
        hf_g, hb_g, _, _ = scan(xb[g].reshape(l, b, rg), cf, cb)
        hf.append(hf_g.reshape(l, b * rg))
        hb.append(hb_g.reshape(l, b * rg))
    router_padded = jnp.concatenate(
        _split_bf16(jnp.pad(router, ((0, 0), (0, LANES - N_EXPERTS)))), axis=1)
    x, h, info, info_t, counts = _outproj_call(gate, hf, hb, w_out, x, mod_x, g0, b0,
                                               router_padded, ts["moe_window"])
    plan = _dispatch_plan(info, info_t, counts, ts["moe_window"], ts["moe_rows"])
    xs = _dispatch_call(plan, h, ts["moe_window"], ts["moe_rows"], plan["n_rows"])
    y = _expert_call(plan, xs, w_gu, w_dn, ts["moe_rows"], ts["ffn_cols"])
    return _combine_call(plan, y, x, mod_x, g1, b1, ts["moe_window"])


def kernel(x, c, ctx, c_ctx, w_mod, b_mod, ln_g, ln_b, attn_w_qkv, attn_w_o, attn_sink, ffn_w_gu,
           ffn_w_dn, lru_w_in, lru_conv_w, lru_conv_b, lru_lambda, lru_w_r, lru_b_r, lru_w_i,
           lru_b_i, lru_w_out, moe_router, moe_w_gu, moe_w_dn):
    b, l, d = x.shape
    assert w_mod.shape[0] == DEPTH and ctx.shape[1] % BLOCK == 0
    assert l % (BAND_QUERY_BLOCKS * BLOCK) == 0
    grid_w = 64
    ts = _tile_sizes()

    rows = -(-(b + 1) // SUBLANES) * SUBLANES
    cond = jnp.zeros((rows, d), F32).at[:b].set(c).at[b].set(c_ctx)
    mod = _mod_call(cond, w_mod, b_mod[:, None, :])
    mod_x = lambda i: mod[i, :b, None, :]
    mod_c = lambda i: mod[i, b:b + 1, None, :]

    x, ctx = _attention_layer(x, ctx, mod_x(0), mod_c(0), ln_g[0], ln_b[0], attn_w_qkv[0],
                              attn_w_o[0], attn_sink[0], ffn_w_gu[0], ffn_w_dn[0], grid_w, ts)
    return _recurrent_layer(x, ctx, mod_x(1), mod_c(1), ln_g[1], ln_b[1], lru_w_in[0],
                            lru_conv_w[0], lru_conv_b[0], lru_lambda[0], lru_w_r[0], lru_b_r[0],
                            lru_w_i[0], lru_b_i[0], lru_w_out[0], moe_router[0], moe_w_gu[0],
                            moe_w_dn[0], ts)
```
